```python
import jax, jax.numpy as jnp
from jax import lax
import numpy as np

D_MODEL = 1024
BATCH = 2
SEQ = 8192
DEPTH = 2

HEAD_DIM = 64
ROPE_THETA = 10000.0
NORM_EPS = 1e-6

NSA_HEADS = 6
NSA_KV_HEADS = 2
NSA_GROUP = NSA_HEADS // NSA_KV_HEADS
NSA_WIDTH = NSA_HEADS * HEAD_DIM
NSA_KV_WIDTH = NSA_KV_HEADS * HEAD_DIM
CMP_BLOCK = 32
CMP_STRIDE = 16
SLC_BLOCK = 64
SLC_TOPN = 16
WINDOW = 512
NSA_Q_BLOCK = 128
FORCE_SCORE = 1e6

MOBA_HEADS = 4
MOBA_WIDTH = MOBA_HEADS * HEAD_DIM
MOBA_BLOCK = 256
MOBA_TOPK = 3
MOBA_Q_BLOCK = 64

RWKV_WIDTH = D_MODEL - NSA_WIDTH - MOBA_WIDTH
RWKV_HEADS = RWKV_WIDTH // HEAD_DIM
DECAY_LORA = 64
AAA_LORA = 64
GATE_LORA = 128
RWKV_SHIFT_COLS = 3 * RWKV_WIDTH + DECAY_LORA + AAA_LORA + GATE_LORA
DECAY_SCALE = 0.606531
GN_EPS = 64e-5

MIX_WIDTH = NSA_WIDTH + MOBA_WIDTH + RWKV_WIDTH
IN_SPLITS = (NSA_WIDTH,) + (NSA_KV_WIDTH,) * 6 + (NSA_HEADS * 3,) + (MOBA_WIDTH,) * 3 + (RWKV_SHIFT_COLS,)
IN_COLS = sum(IN_SPLITS)

N_EXPERTS = 16
N_EXPERT_GROUPS = 4
EXPERTS_PER_GROUP = N_EXPERTS // N_EXPERT_GROUPS
TOP_K = 2
EXPERT_FF = 256

kernel_name = 'hybrid_nsa_moba_rwkv7_groupmoe'


def split_cols(t, sizes):
    offs, acc = [], 0
    for s in sizes[:-1]:
        acc += s
        offs.append(acc)
    return jnp.split(t, offs, axis=-1)


def rms_norm(x, gain):
    xf = x.astype(jnp.float32)
    y = xf * lax.rsqrt(jnp.mean(xf * xf, axis=-1, keepdims=True) + NORM_EPS)
    return (y * gain.astype(jnp.float32)).astype(x.dtype)


def rope_tables(positions):
    inv = ROPE_THETA ** (-jnp.arange(0, HEAD_DIM, 2, dtype=jnp.float32) / HEAD_DIM)
    ang = positions.astype(jnp.float32)[..., None] * inv
    return jnp.cos(ang), jnp.sin(ang)


def apply_rope(x, cos, sin):
    x1, x2 = jnp.split(x, 2, axis=-1)
    c = cos[:, :, None, :].astype(x.dtype)
    s = sin[:, :, None, :].astype(x.dtype)
    return jnp.concatenate([x1 * c - x2 * s, x2 * c + x1 * s], axis=-1)


def masked_softmax(s, mask):
    s = jnp.where(mask, s.astype(jnp.float32), -jnp.inf)
    m = jnp.max(s, axis=-1, keepdims=True)
    e = jnp.where(mask, jnp.exp(s - jnp.where(jnp.isfinite(m), m, 0.0)), 0.0)
    return e / jnp.maximum(jnp.sum(e, axis=-1, keepdims=True), 1e-30)


def gather_blocks(blocks, idx):
    return jax.vmap(jax.vmap(lambda bl, ix: bl[ix]))(blocks, idx)


def nsa_compress(t, w1, w2, pos):
    S = t.shape[1]
    n_cmp = (S - CMP_BLOCK) // CMP_STRIDE + 1
    idx = jnp.arange(n_cmp)[:, None] * CMP_STRIDE + jnp.arange(CMP_BLOCK)[None, :]
    blk = t[:, idx] + pos[None, None, :, None, :]
    h = jax.nn.gelu(jnp.einsum('bnlhd,lde->bnhe', blk, w1))
    return jnp.einsum('bnhe,ef->bnhf', h, w2)


def nsa_mixer(q, kc, vc, ks, vs, kw, vw, gates, cos, sin, phi_w1, phi_w2, phi_pos):
    B, S = q.shape[:2]
    KV, G, QB = NSA_KV_HEADS, NSA_GROUP, NSA_Q_BLOCK
    scale = HEAD_DIM ** -0.5
    n_cmp = (S - CMP_BLOCK) // CMP_STRIDE + 1
    n_slc = S // SLC_BLOCK
    n_top = min(SLC_TOPN, n_slc)
    cmp_end = jnp.arange(n_cmp) * CMP_STRIDE + CMP_BLOCK - 1
    k_cmp = apply_rope(nsa_compress(kc, phi_w1[0], phi_w2[0], phi_pos[0]), cos[:, cmp_end], sin[:, cmp_end])
    v_cmp = nsa_compress(vc, phi_w1[1], phi_w2[1], phi_pos[1])
    c_start = jnp.arange(n_cmp) * CMP_STRIDE
    s_start = jnp.arange(n_slc) * SLC_BLOCK
    overlap = (jnp.minimum(c_start[:, None] + CMP_BLOCK, s_start[None, :] + SLC_BLOCK)
               - jnp.maximum(c_start[:, None], s_start[None, :]))
    cmp_to_slc = jnp.clip(overlap, 0, None).astype(jnp.float32) / CMP_BLOCK
    ks_blk = ks.reshape(B, n_slc, SLC_BLOCK, KV, HEAD_DIM).transpose(0, 3, 1, 2, 4)
    vs_blk = vs.reshape(B, n_slc, SLC_BLOCK, KV, HEAD_DIM).transpose(0, 3, 1, 2, 4)
    kw_pad = jnp.pad(kw, ((0, 0), (WINDOW, 0), (0, 0), (0, 0)))
    vw_pad = jnp.pad(vw, ((0, 0), (WINDOW, 0), (0, 0), (0, 0)))
    blk_id = jnp.arange(n_slc)

    def chunk(ci):
        s0 = ci * QB
        t = s0 + jnp.arange(QB)
        qb = lax.dynamic_slice_in_dim(q, s0, QB, axis=1).reshape(B, QB, KV, G, HEAD_DIM)
        gb = jax.nn.sigmoid(lax.dynamic_slice_in_dim(gates, s0, QB, axis=1)).reshape(B, QB, KV, G, 3)
        s_c = jnp.einsum('bqhgd,bnhd->bhgqn', qb, k_cmp) * scale
        p_c = masked_softmax(s_c, cmp_end[None, :] <= t[:, None])
        o_c = jnp.einsum('bhgqn,bnhd->bqhgd', p_c.astype(v_cmp.dtype), v_cmp)
        imp = jnp.einsum('bhgqn,ns->bhqs', p_c, cmp_to_slc)
        cur = t // SLC_BLOCK
        cand = blk_id[None, :] <= cur[:, None]
        forced = (blk_id[None, :] == 0) | (blk_id[None, :] == cur[:, None]) | (blk_id[None, :] == cur[:, None] - 1)
        score = jnp.where(cand, jnp.where(forced, FORCE_SCORE, imp), -1.0)
        _, idx = lax.top_k(score, n_top)
        k_sel = gather_blocks(ks_blk, idx)
        v_sel = gather_blocks(vs_blk, idx)
        kpos = idx[..., None] * SLC_BLOCK + jnp.arange(SLC_BLOCK)
        m_s = (idx <= cur[:, None])[..., None] & (kpos <= t[:, None, None])
        n_keys = n_top * SLC_BLOCK
        s_s = jnp.einsum('bqhgd,bhqnkd->bhgqnk', qb, k_sel) * scale
        p_s = masked_softmax(s_s.reshape(B, KV, G, QB, n_keys), m_s.reshape(B, KV, 1, QB, n_keys))
        o_s = jnp.einsum('bhgqm,bhqmd->bqhgd', p_s.astype(vs.dtype), v_sel.reshape(B, KV, QB, n_keys, HEAD_DIM))
        kwb = lax.dynamic_slice_in_dim(kw_pad, s0, QB + WINDOW, axis=1)
        vwb = lax.dynamic_slice_in_dim(vw_pad, s0, QB + WINDOW, axis=1)
        kpos_w = s0 - WINDOW + jnp.arange(QB + WINDOW)
        dist = t[:, None] - kpos_w[None, :]
        m_w = (kpos_w[None, :] >= 0) & (dist >= 0) & (dist < WINDOW)
        s_w = jnp.einsum('bqhgd,bkhd->bhgqk', qb, kwb) * scale
        p_w = masked_softmax(s_w, m_w)
        o_w = jnp.einsum('bhgqk,bkhd->bqhgd', p_w.astype(vw.dtype), vwb)
        o = gb[..., 0:1] * o_c + gb[..., 1:2] * o_s + gb[..., 2:3] * o_w
        return o.reshape(B, QB, NSA_WIDTH)

    out = lax.map(chunk, jnp.arange(S // QB))
    return jnp.moveaxis(out, 0, 1).reshape(B, S, NSA_WIDTH)


def moba_mixer(q, k, v):
    B, S = q.shape[:2]
    H, QB = MOBA_HEADS, MOBA_Q_BLOCK
    scale = HEAD_DIM ** -0.5
    nb = -(-S // MOBA_BLOCK)
    pad = nb * MOBA_BLOCK - S
    kp = jnp.pad(k, ((0, 0), (0, pad), (0, 0), (0, 0)))
    vp = jnp.pad(v, ((0, 0), (0, pad), (0, 0), (0, 0)))
    k_blk = kp.reshape(B, nb, MOBA_BLOCK, H, HEAD_DIM).transpose(0, 3, 1, 2, 4)
    v_blk = vp.reshape(B, nb, MOBA_BLOCK, H, HEAD_DIM).transpose(0, 3, 1, 2, 4)
    k_mean = jnp.mean(k_blk.astype(jnp.float32), axis=3).astype(k.dtype)
    n_top = min(MOBA_TOPK, nb)
    n_sel = n_top * MOBA_BLOCK
    blk_id = jnp.arange(nb)

    def chunk(ci):
        s0 = ci * QB
        t = s0 + jnp.arange(QB)
        own = s0 // MOBA_BLOCK
        qb = lax.dynamic_slice_in_dim(q, s0, QB, axis=1)
        gate = jnp.einsum('bqhd,bhnd->bhqn', qb, k_mean).astype(jnp.float32)
        gate = jnp.where(blk_id < own, gate, -jnp.inf)
        _, idx = lax.top_k(gate, n_top)
        valid = idx < own
        k_sel = gather_blocks(k_blk, idx)
        v_sel = gather_blocks(v_blk, idx)
        s_sel = jnp.einsum('bqhd,bhqnkd->bhqnk', qb, k_sel).reshape(B, H, QB, n_sel)
        m_sel = jnp.broadcast_to(valid[..., None], (B, H, QB, n_top, MOBA_BLOCK)).reshape(B, H, QB, n_sel)
        k_own = lax.dynamic_slice_in_dim(kp, own * MOBA_BLOCK, MOBA_BLOCK, axis=1)
        v_own = lax.dynamic_slice_in_dim(vp, own * MOBA_BLOCK, MOBA_BLOCK, axis=1)
        s_own = jnp.einsum('bqhd,bkhd->bhqk', qb, k_own)
        m_own = jnp.broadcast_to((own * MOBA_BLOCK + jnp.arange(MOBA_BLOCK))[None, :] <= t[:, None], (B, H, QB, MOBA_BLOCK))
        p = masked_softmax(jnp.concatenate([s_sel, s_own], axis=-1) * scale,
                           jnp.concatenate([m_sel, m_own], axis=-1)).astype(v.dtype)
        o = (jnp.einsum('bhqm,bhqmd->bqhd', p[..., :n_sel], v_sel.reshape(B, H, QB, n_sel, HEAD_DIM))
             + jnp.einsum('bhqk,bkhd->bqhd', p[..., n_sel:], v_own))
        return o.reshape(B, QB, MOBA_WIDTH)

    out = lax.map(chunk, jnp.arange(S // QB))
    return jnp.moveaxis(out, 0, 1).reshape(B, S, MOBA_WIDTH)


def _rwkv7_step(state, inp):
    r_t, w_t, k_t, v_t, kk_t, a_t = inp
    sa = jnp.einsum('bhvk,bhk->bhv', state, -kk_t)
    state = (state * w_t[:, :, None, :] + sa[..., None] * (kk_t * a_t)[:, :, None, :]
             + v_t[..., None] * k_t[:, :, None, :])
    return state, jnp.einsum('bhvk,bhk->bhv', state, r_t)


def rwkv7_mixer(feat, mu, w_up, w0, a_up, a0, g_up, k_k, k_a, r_k, ln_w, ln_b):
    B, S, _ = feat.shape
    H, N = RWKV_HEADS, HEAD_DIM
    prev = jnp.pad(feat, ((0, 0), (1, 0), (0, 0)))[:, :-1]
    feat = feat + (prev - feat) * mu
    r, k, v, wd, ad, gd = split_cols(feat, (RWKV_WIDTH,) * 3 + (DECAY_LORA, AAA_LORA, GATE_LORA))
    decay = jnp.exp(-DECAY_SCALE * jax.nn.sigmoid((w0 + jnp.tanh(wd) @ w_up).astype(jnp.float32)))
    a = jax.nn.sigmoid(a0 + ad @ a_up)
    g = jax.nn.sigmoid(gd) @ g_up
    heads = lambda t: t.reshape(B, S, H, N).astype(jnp.float32)
    kk = heads(k * k_k)
    kk = kk / jnp.maximum(jnp.sqrt(jnp.sum(kk * kk, axis=-1, keepdims=True)), 1e-12)
    k = k * (1 + (a - 1) * k_a)
    rh, kh, vh = heads(r), heads(k), heads(v)
    xs = tuple(jnp.moveaxis(t, 1, 0) for t in (rh, heads(decay), kh, vh, kk, heads(a)))
    _, ys = lax.scan(_rwkv7_step, jnp.zeros((B, H, N, N), jnp.float32), xs)
    y = jnp.moveaxis(ys, 0, 1)
    mean = jnp.mean(y, axis=-1, keepdims=True)
    var = jnp.mean(jnp.square(y - mean), axis=-1, keepdims=True)
    y = ((y - mean) * lax.rsqrt(var + GN_EPS)).reshape(B, S, RWKV_WIDTH) * ln_w + ln_b
    bonus = (jnp.sum(rh * kh * r_k, axis=-1, keepdims=True) * vh).reshape(B, S, RWKV_WIDTH)
    return ((y + bonus) * g).astype(feat.dtype)


def moe_ffn(h, w_router, router_bias, w_gate, w_up, w_down):
    B, S, D = h.shape
    ht = h.reshape(B * S, D)
    T = ht.shape[0]
    aff = jax.nn.sigmoid((ht @ w_router).astype(jnp.float32))
    biased = (aff + router_bias.astype(jnp.float32)).reshape(T, N_EXPERT_GROUPS, EXPERTS_PER_GROUP)
    grp_score = jnp.sum(lax.top_k(biased, TOP_K)[0], axis=-1)
    best = jnp.argmax(grp_score, axis=-1)
    in_grp = jnp.take_along_axis(biased, best[:, None, None], axis=1)[:, 0]
    _, loc = lax.top_k(in_grp, TOP_K)
    eidx = best[:, None] * EXPERTS_PER_GROUP + loc
    wsel = jnp.take_along_axis(aff, eidx, axis=-1)
    wsel = wsel / jnp.sum(wsel, axis=-1, keepdims=True)
    combine = jnp.sum(jax.nn.one_hot(eidx, N_EXPERTS, dtype=jnp.float32) * wsel[..., None], axis=1).astype(ht.dtype)
    y = jnp.zeros_like(ht)
    for e in range(N_EXPERTS):
        he = jax.nn.silu(ht @ w_gate[e]) * (ht @ w_up[e])
        y = y + combine[:, e:e + 1] * (he @ w_down[e])
    return y.reshape(B, S, D)


def setup_inputs(seed: int = 0) -> dict:
    key = jax.random.key(seed)
    ks = jax.random.split(key, 40)
    f32 = jnp.float32
    nrm = lambda k, shape, s: jax.random.normal(k, shape, f32) * s
    D, L = D_MODEL, DEPTH
    offset = jax.random.randint(ks[2], (BATCH, 1), 0, 4096, dtype=jnp.int32)
    return {
        'x': nrm(ks[0], (BATCH, SEQ, D), 1.0),
        'c': nrm(ks[1], (BATCH, D), 1.0),
        'positions': offset + jnp.arange(SEQ, dtype=jnp.int32)[None, :],
        'w_mod': nrm(ks[3], (L, D, 6 * D), 0.5 * D ** -0.5),
        'b_mod': nrm(ks[4], (L, 6 * D), 0.02),
        'norm_mix': 1.0 + nrm(ks[5], (L, D), 0.05),
        'norm_ffn': 1.0 + nrm(ks[6], (L, D), 0.05),
        'w_in': nrm(ks[7], (L, D, IN_COLS), D ** -0.5),
        'nsa_phi_w1': nrm(ks[8], (L, 2, CMP_BLOCK, HEAD_DIM, HEAD_DIM), (CMP_BLOCK * HEAD_DIM) ** -0.5),
        'nsa_phi_w2': nrm(ks[9], (L, 2, HEAD_DIM, HEAD_DIM), HEAD_DIM ** -0.5),
        'nsa_phi_pos': nrm(ks[10], (L, 2, CMP_BLOCK, HEAD_DIM), 0.1),
        'rwkv_mu': jax.random.uniform(ks[11], (L, RWKV_SHIFT_COLS), f32),
        'rwkv_w_up': nrm(ks[12], (L, DECAY_LORA, RWKV_WIDTH), DECAY_LORA ** -0.5),
        'rwkv_w0': nrm(ks[13], (L, RWKV_WIDTH), 0.5),
        'rwkv_a_up': nrm(ks[14], (L, AAA_LORA, RWKV_WIDTH), 0.5 * AAA_LORA ** -0.5),
        'rwkv_a0': nrm(ks[15], (L, RWKV_WIDTH), 0.5),
        'rwkv_g_up': nrm(ks[16], (L, GATE_LORA, RWKV_WIDTH), GATE_LORA ** -0.5),
        'rwkv_k_k': 0.85 + nrm(ks[17], (L, RWKV_WIDTH), 0.05),
        'rwkv_k_a': 1.0 + nrm(ks[18], (L, RWKV_WIDTH), 0.05),
        'rwkv_r_k': nrm(ks[19], (L, RWKV_HEADS, HEAD_DIM), 0.1),
        'rwkv_ln_w': 1.0 + nrm(ks[20], (L, RWKV_WIDTH), 0.05),
        'rwkv_ln_b': nrm(ks[21], (L, RWKV_WIDTH), 0.02),
        'norm_nsa_out': 1.0 + nrm(ks[22], (L, NSA_WIDTH), 0.05),
        'norm_moba_out': 1.0 + nrm(ks[23], (L, MOBA_WIDTH), 0.05),
        'w_out': nrm(ks[24], (L, MIX_WIDTH, D), MIX_WIDTH ** -0.5),
        'w_router': nrm(ks[25], (D, N_EXPERTS), D ** -0.5),
        'router_bias': nrm(ks[26], (N_EXPERTS,), 0.01),
        'moe_w_gate': nrm(ks[27], (L, N_EXPERTS, D, EXPERT_FF), D ** -0.5),
        'moe_w_up': nrm(ks[28], (L, N_EXPERTS, D, EXPERT_FF), D ** -0.5),
        'moe_w_down': nrm(ks[29], (L, N_EXPERTS, EXPERT_FF, D), EXPERT_FF ** -0.5),
        'norm_final': 1.0 + nrm(ks[30], (D,), 0.05),
    }


def reference(x, c, positions, w_mod, b_mod, norm_mix, norm_ffn, w_in, nsa_phi_w1, nsa_phi_w2,
              nsa_phi_pos, rwkv_mu, rwkv_w_up, rwkv_w0, rwkv_a_up, rwkv_a0, rwkv_g_up, rwkv_k_k,
              rwkv_k_a, rwkv_r_k, rwkv_ln_w, rwkv_ln_b, norm_nsa_out, norm_moba_out, w_out,
              w_router, router_bias, moe_w_gate, moe_w_up, moe_w_down, norm_final):
    B, S, D = x.shape
    cos, sin = rope_tables(positions)
    c_act = jax.nn.silu(c)
    for l in range(DEPTH):
        mod = (c_act @ w_mod[l] + b_mod[l])[:, None, :]
        sh1, sc1, g1, sh2, sc2, g2 = jnp.split(mod, 6, axis=-1)
        h = rms_norm(x, norm_mix[l]) * (1 + sc1) + sh1
        proj = h @ w_in[l]
        (nq, nkc, nvc, nks, nvs, nkw, nvw, ngate, mq, mk, mv, rfeat) = split_cols(proj, IN_SPLITS)
        kvh = lambda t: t.reshape(B, S, NSA_KV_HEADS, HEAD_DIM)
        mh = lambda t: t.reshape(B, S, MOBA_HEADS, HEAD_DIM)
        o_nsa = nsa_mixer(apply_rope(nq.reshape(B, S, NSA_HEADS, HEAD_DIM), cos, sin),
                          kvh(nkc), kvh(nvc), apply_rope(kvh(nks), cos, sin), kvh(nvs),
                          apply_rope(kvh(nkw), cos, sin), kvh(nvw),
                          ngate.reshape(B, S, NSA_HEADS, 3), cos, sin,
                          nsa_phi_w1[l], nsa_phi_w2[l], nsa_phi_pos[l])
        o_moba = moba_mixer(apply_rope(mh(mq), cos, sin), apply_rope(mh(mk), cos, sin), mh(mv))
        o_rwkv = rwkv7_mixer(rfeat, rwkv_mu[l], rwkv_w_up[l], rwkv_w0[l], rwkv_a_up[l], rwkv_a0[l],
                             rwkv_g_up[l], rwkv_k_k[l], rwkv_k_a[l], rwkv_r_k[l], rwkv_ln_w[l], rwkv_ln_b[l])
        mix = jnp.concatenate([rms_norm(o_nsa, norm_nsa_out[l]), rms_norm(o_moba, norm_moba_out[l]), o_rwkv], axis=-1)
        x = x + g1 * (mix @ w_out[l])
        h2 = rms_norm(x, norm_ffn[l]) * (1 + sc2) + sh2
        x = x + g2 * moe_ffn(h2, w_router, router_bias, moe_w_gate[l], moe_w_up[l], moe_w_down[l])
    return rms_norm(x, norm_final)
```

```python
import functools

import jax
import jax.numpy as jnp
from jax import lax
from jax.experimental import pallas as pl
from jax.experimental.pallas import tpu as pltpu

F32 = jnp.float32
BF16 = jnp.bfloat16

HEAD_DIM = 64
LANES = 128
ROPE_THETA = 10000.0
NORM_EPS = 1e-6
NEG = -1e30

NSA_HEADS = 6
NSA_KV_HEADS = 2
NSA_GROUP = NSA_HEADS // NSA_KV_HEADS
NSA_WIDTH = NSA_HEADS * HEAD_DIM
NSA_KV_WIDTH = NSA_KV_HEADS * HEAD_DIM
CMP_BLOCK = 32
CMP_STRIDE = 16
SLC_BLOCK = 64
SLC_TOPN = 16
WINDOW = 512
NSA_QB = 128
NSA_TK = 2 * NSA_QB

MOBA_HEADS = 4
MOBA_WIDTH = MOBA_HEADS * HEAD_DIM
MOBA_BLOCK = 256
MOBA_TOPK = 3
MOBA_NBP = 128

RWKV_HEADS = 6
RWKV_WIDTH = RWKV_HEADS * HEAD_DIM
DECAY_LORA = 64
AAA_LORA = 64
GATE_LORA = 128
RWKV_COLS = 3 * RWKV_WIDTH + DECAY_LORA + AAA_LORA + GATE_LORA
DECAY_SCALE = 0.606531
GN_EPS = 64e-5
RWKV_CHUNK = 64
RWKV_TB = 128

N_EXPERTS = 16
N_EXPERT_GROUPS = 4
EXPERTS_PER_GROUP = N_EXPERTS // N_EXPERT_GROUPS
EXPERT_FF = 256

G_NSA_Q = 0
G_SLC = 6
G_WIN = 8
G_KC = 10
G_VC = 11
G_MOBA_Q = 12
G_MOBA_KV = 16
N_GROUPS = 20
ROPED_GROUPS = tuple(range(0, 10)) + tuple(range(12, 20))
P_COLS = N_GROUPS * LANES
R_GATE_BLOCK = RWKV_COLS // LANES
R_COLS = RWKV_COLS + NSA_KV_HEADS * LANES

VMEM_LIMIT = 56 * 1024 * 1024


def _cparams(sem):
    return pltpu.CompilerParams(dimension_semantics=sem, vmem_limit_bytes=VMEM_LIMIT)


def _dot(a, b):
    return jnp.dot(a, b, preferred_element_type=F32)


def _dot_nt(a, b):
    return lax.dot_general(a, b, (((1,), (1,)), ((), ())), preferred_element_type=F32)


def _dot_tn(a, b):
    return lax.dot_general(a, b, (((0,), (0,)), ((), ())), preferred_element_type=F32)


def _split2(x):
    hi = x.astype(BF16)
    lo = (x - hi.astype(F32)).astype(BF16)
    return hi, lo


def _split3(x):
    hi = x.astype(BF16)
    r1 = x - hi.astype(F32)
    mid = r1.astype(BF16)
    lo = (r1 - mid.astype(F32)).astype(BF16)
    return hi, mid, lo


def _mm3(a, b, dot=_dot):
    ah, al = _split2(a)
    bh, bl = _split2(b)
    return dot(ah, bh) + dot(ah, bl) + dot(al, bh)


def _mm3_exact_rhs(a, b_bf16):
    hi, mid, lo = _split3(a)
    return _dot(hi, b_bf16) + _dot(mid, b_bf16) + _dot(lo, b_bf16)


def _rope(y, cos_t, sin_s):
    lane = lax.broadcasted_iota(jnp.int32, y.shape, 1)
    rot = jnp.where(lane < HEAD_DIM // 2, pltpu.roll(y, LANES - HEAD_DIM // 2, 1), pltpu.roll(y, HEAD_DIM // 2, 1))
    return y * cos_t + rot * sin_s


def _mod_kernel(c_ref, w_ref, b_ref, o_ref):
    c = c_ref[...]
    ca = c * jax.nn.sigmoid(c)
    o_ref[0] = _dot(ca.astype(BF16), w_ref[0].astype(BF16)) + b_ref[0]


def _modulation(c, w_mod, b_mod):
    B, D = c.shape
    L, _, N = w_mod.shape
    tn = 512
    c8 = jnp.zeros((8, D), F32).at[:B].set(c)
    return pl.pallas_call(
        _mod_kernel,
        grid=(L, N // tn),
        in_specs=[
            pl.BlockSpec((8, D), lambda l, j: (0, 0)),
            pl.BlockSpec((1, D, tn), lambda l, j: (l, 0, j)),
            pl.BlockSpec((1, 1, tn), lambda l, j: (l, 0, j)),
        ],
        out_specs=pl.BlockSpec((1, 8, tn), lambda l, j: (l, 0, j)),
        out_shape=jax.ShapeDtypeStruct((L, 8, N), F32),
        compiler_params=_cparams(("arbitrary", "arbitrary")),
        name="modulation",
    )(c8, w_mod, b_mod.reshape(L, 1, N))


def _inproj_kernel(x_ref, gain_ref, sc_ref, sh_ref, cos_ref, sin_ref, w_ref, p_ref, r_ref, km_ref):
    x = x_ref[0]
    ms = jnp.mean(x * x, axis=-1, keepdims=True)
    h = x * lax.rsqrt(ms + NORM_EPS) * gain_ref[...]
    h = h * (1.0 + sc_ref[0]) + sh_ref[0]
    hb = h.astype(BF16)
    cos_t = cos_ref[0]
    sin_s = sin_ref[0]
    per = 4
    for g0 in range(0, N_GROUPS, per):
        y4 = _dot(hb, w_ref[:, g0 * LANES:(g0 + per) * LANES])
        for j in range(per):
            g = g0 + j
            y = y4[:, j * LANES:(j + 1) * LANES]
            if g in ROPED_GROUPS:
                y = _rope(y, cos_t, sin_s)
            if g >= G_MOBA_KV:
                km = jnp.mean(y, axis=0, keepdims=True)
                km_ref[0, 0, :, (g - G_MOBA_KV) * LANES:(g - G_MOBA_KV + 1) * LANES] = jnp.broadcast_to(km, (8, LANES))
            p_ref[0, :, g * LANES:(g + 1) * LANES] = y.astype(BF16)
    r_ref[0] = _dot(hb, w_ref[:, P_COLS:])


def _inproj(x, gain, sc, sh, cos_t, sin_s, w):
    B, S, D = x.shape
    tm = MOBA_BLOCK
    return pl.pallas_call(
        _inproj_kernel,
        grid=(B, S // tm),
        in_specs=[
            pl.BlockSpec((1, tm, D), lambda b, i: (b, i, 0)),
            pl.BlockSpec((1, D), lambda b, i: (0, 0)),
            pl.BlockSpec((1, 1, D), lambda b, i: (b, 0, 0)),
            pl.BlockSpec((1, 1, D), lambda b, i: (b, 0, 0)),
            pl.BlockSpec((1, tm, LANES), lambda b, i: (b, i, 0)),
            pl.BlockSpec((1, tm, LANES), lambda b, i: (b, i, 0)),
            pl.BlockSpec((D, P_COLS + R_COLS), lambda b, i: (0, 0)),
        ],
        out_specs=[
            pl.BlockSpec((1, tm, P_COLS), lambda b, i: (b, i, 0)),
            pl.BlockSpec((1, tm, R_COLS), lambda b, i: (b, i, 0)),
            pl.BlockSpec((1, 1, 8, MOBA_HEADS * LANES), lambda b, i: (b, i, 0, 0)),
        ],
        out_shape=[
            jax.ShapeDtypeStruct((B, S, P_COLS), BF16),
            jax.ShapeDtypeStruct((B, S, R_COLS), F32),
            jax.ShapeDtypeStruct((B, S // tm, 8, MOBA_HEADS * LANES), F32),
        ],
        compiler_params=_cparams(("arbitrary", "arbitrary")),
        name="inproj",
    )(x, gain, sc, sh, cos_t, sin_s, w)


def _prep_w_in(w):
    D = w.shape[0]
    o = 0
    parts = {}
    for name, width in (("nq", NSA_WIDTH), ("nkc", NSA_KV_WIDTH), ("nvc", NSA_KV_WIDTH), ("nks", NSA_KV_WIDTH),
                        ("nvs", NSA_KV_WIDTH), ("nkw", NSA_KV_WIDTH), ("nvw", NSA_KV_WIDTH), ("ngate", NSA_HEADS * 3),
                        ("mq", MOBA_WIDTH), ("mk", MOBA_WIDTH), ("mv", MOBA_WIDTH), ("rf", RWKV_COLS)):
        parts[name] = w[:, o:o + width]
        o += width
    hd = lambda t, h: t[:, h * HEAD_DIM:(h + 1) * HEAD_DIM]
    z = jnp.zeros((D, HEAD_DIM), F32)
    scale = HEAD_DIM ** -0.5
    cols = []
    for h in range(NSA_HEADS):
        cols += [hd(parts["nq"], h) * scale, z]
    for h in range(NSA_KV_HEADS):
        cols += [hd(parts["nks"], h), hd(parts["nvs"], h)]
    for h in range(NSA_KV_HEADS):
        cols += [hd(parts["nkw"], h), hd(parts["nvw"], h)]
    cols += [parts["nkc"], parts["nvc"]]
    for h in range(MOBA_HEADS):
        cols += [hd(parts["mq"], h) * scale, z]
    for h in range(MOBA_HEADS):
        cols += [hd(parts["mk"], h), hd(parts["mv"], h)]
    cols += [parts["rf"]]
    per = NSA_GROUP * 3
    for h in range(NSA_KV_HEADS):
        cols += [parts["ngate"][:, h * per:(h + 1) * per], jnp.zeros((D, LANES - per), F32)]
    return jnp.concatenate(cols, axis=1).astype(BF16)


def _cmp_kernel(kc_ref, vc_ref, pos_ref, w1_ref, w2_ref, cos_ref, sin_ref, o_ref):
    nc = kc_ref.shape[1]

    def hidden(t_ref, ia, ib):
        t = t_ref[0].astype(F32)
        a = _dot((t + pos_ref[ia:ia + 1, :]).astype(BF16), w1_ref[ia])
        b = _dot((t + pos_ref[ib:ib + 1, :]).astype(BF16), w1_ref[ib])
        return jax.nn.gelu(a + pltpu.roll(b, nc - 1, 0)).astype(BF16)

    gk = hidden(kc_ref, 0, 1)
    gv = hidden(vc_ref, 2, 3)
    for h in range(NSA_KV_HEADS):
        y = _dot(gk, w2_ref[2 * h]) + _dot(gv, w2_ref[2 * h + 1])
        o_ref[0, h] = _rope(y, cos_ref[0], sin_ref[0]).astype(BF16)


def _compress(kc16, vc16, pos4, w1, w2, cos_c, sin_c):
    B, nc, K = kc16.shape
    return pl.pallas_call(
        _cmp_kernel,
        grid=(B,),
        in_specs=[
            pl.BlockSpec((1, nc, K), lambda b: (b, 0, 0)),
            pl.BlockSpec((1, nc, K), lambda b: (b, 0, 0)),
            pl.BlockSpec((4, K), lambda b: (0, 0)),
            pl.BlockSpec((4, K, LANES), lambda b: (0, 0, 0)),
            pl.BlockSpec((4, LANES, LANES), lambda b: (0, 0, 0)),
            pl.BlockSpec((1, nc, LANES), lambda b: (b, 0, 0)),
            pl.BlockSpec((1, nc, LANES), lambda b: (b, 0, 0)),
        ],
        out_specs=pl.BlockSpec((1, NSA_KV_HEADS, nc, LANES), lambda b: (b, 0, 0, 0)),
        out_shape=jax.ShapeDtypeStruct((B, NSA_KV_HEADS, nc, LANES), BF16),
        compiler_params=_cparams(("arbitrary",)),
        name="nsa_compress",
    )(kc16, vc16, pos4, w1, w2, cos_c, sin_c)


def _prep_compress(phi_w1, phi_w2, phi_pos):
    half = CMP_BLOCK // 2
    eye = jnp.eye(NSA_KV_HEADS, dtype=F32)
    w1, pos = [], []
    for t in range(2):
        for part in range(2):
            w = phi_w1[t, part * half:(part + 1) * half]
            w1.append(jnp.einsum("lde,kK->lkdKe", w, eye).reshape(half * NSA_KV_WIDTH, NSA_KV_WIDTH))
            p = phi_pos[t, part * half:(part + 1) * half]
            pos.append(jnp.broadcast_to(p[:, None, :], (half, NSA_KV_HEADS, HEAD_DIM)).reshape(-1))
    w2 = []
    for h in range(NSA_KV_HEADS):
        for t in range(2):
            m = jnp.zeros((LANES, LANES), F32)
            m = m.at[h * HEAD_DIM:(h + 1) * HEAD_DIM, t * HEAD_DIM:(t + 1) * HEAD_DIM].set(phi_w2[t])
            w2.append(m)
    return jnp.stack(w1).astype(BF16), jnp.stack(w2).astype(BF16), jnp.stack(pos)


def _softmax_rows(s, mask):
    sm = jnp.where(mask, s, NEG)
    mx = jnp.max(sm, axis=-1, keepdims=True)
    e = jnp.where(mask, jnp.exp(sm - mx), 0.0)
    den = jnp.maximum(jnp.sum(e, axis=-1, keepdims=True), 1e-30)
    return e / den


def _online_update(s, kv, m_scr, l_scr, acc_scr):
    m_prev = m_scr[...]
    m_new = jnp.maximum(m_prev, jnp.max(s, axis=-1, keepdims=True))
    alpha = jnp.exp(m_prev - m_new)
    p = jnp.exp(s - m_new[:, 0:1])
    l_scr[...] = alpha * l_scr[...] + jnp.sum(p, axis=-1, keepdims=True)
    acc_scr[...] = alpha * acc_scr[...] + _dot(p.astype(BF16), kv)
    m_scr[...] = m_new


def _nsa_kernel(q0_ref, q1_ref, q2_ref, kvc_ref, kvs_ref, kvw_ref, gate_ref, m_ref, e_ref, o_ref,
                m_scr, l_scr, acc_scr, *, n_pick):
    ci = pl.program_id(2)
    qb = NSA_QB
    rows = NSA_GROUP * qb
    q = jnp.concatenate([q0_ref[0], q1_ref[0], q2_ref[0]], axis=0)

    def t_of_rows(shape):
        return ci * qb + (lax.broadcasted_iota(jnp.int32, shape, 0) & (qb - 1))

    kc = kvc_ref[0, 0]
    nc = kc.shape[0]
    s = _dot_nt(q, kc)
    cend = lax.broadcasted_iota(jnp.int32, (rows, nc), 1) * CMP_STRIDE + (CMP_BLOCK - 1)
    p_c = _softmax_rows(s, cend <= t_of_rows((rows, nc)))
    o_c = _dot(p_c.astype(BF16), kc)

    psum = p_c[0:qb] + p_c[qb:2 * qb] + p_c[2 * qb:3 * qb]
    imp_t = _mm3_exact_rhs(psum, m_ref[...]).T
    blk = lax.broadcasted_iota(jnp.int32, imp_t.shape, 0)
    cur = (ci * qb + lax.broadcasted_iota(jnp.int32, imp_t.shape, 1)) // SLC_BLOCK
    forced = (blk == 0) | (blk == cur) | (blk == cur - 1)
    free = (blk <= cur) & jnp.logical_not(forced)
    sel = jnp.where(forced, 1.0, 0.0)
    sc = jnp.where(free, imp_t, -1.0)
    for _ in range(n_pick):
        best = jnp.max(sc, axis=0, keepdims=True)
        idx = jnp.min(jnp.where(sc == best, blk, imp_t.shape[0]), axis=0, keepdims=True)
        pick = (blk == idx) & (best > -0.5)
        sel = jnp.where(pick, 1.0, sel)
        sc = jnp.where(pick, -2.0, sc)
    bias = jnp.where(sel > 0.5, 0.0, NEG).T.astype(BF16)
    bias3 = jnp.concatenate([bias] * NSA_GROUP, axis=0)

    m_scr[...] = jnp.full(m_scr.shape, -jnp.inf, F32)
    l_scr[...] = jnp.zeros(l_scr.shape, F32)
    acc_scr[...] = jnp.zeros(acc_scr.shape, F32)
    tk = NSA_TK

    def slc_tile(j, causal):
        kv = kvs_ref[0, pl.ds(pl.multiple_of(j * tk, tk), tk), :]
        s = _dot_nt(q, kv) + _dot(bias3, e_ref[j])
        if causal:
            kpos = j * tk + lax.broadcasted_iota(jnp.int32, (rows, tk), 1)
            s = jnp.where(kpos <= t_of_rows((rows, tk)), s, NEG)
        _online_update(s, kv, m_scr, l_scr, acc_scr)

    last = (ci * qb) // tk

    def body(j, carry):
        slc_tile(j, False)
        return carry

    lax.fori_loop(0, last, body, 0)
    slc_tile(last, True)
    o_s = acc_scr[...] / l_scr[...]

    start = pl.multiple_of(jnp.maximum(ci * qb - WINDOW, 0), qb)
    span = WINDOW + qb
    kvw = kvw_ref[0, pl.ds(start, span), :]
    s = _dot_nt(q, kvw)
    dist = t_of_rows((rows, span)) - (start + lax.broadcasted_iota(jnp.int32, (rows, span), 1))
    p_w = _softmax_rows(s, (dist >= 0) & (dist < WINDOW))
    o_w = _dot(p_w.astype(BF16), kvw)

    gt = jax.nn.sigmoid(gate_ref[0])
    for g in range(NSA_GROUP):
        r = slice(g * qb, (g + 1) * qb)
        o = (gt[:, 3 * g:3 * g + 1] * o_c[r] + gt[:, 3 * g + 1:3 * g + 2] * o_s[r]
             + gt[:, 3 * g + 2:3 * g + 3] * o_w[r])
        o_ref[0, :, g * LANES:(g + 1) * LANES] = o


def _nsa(p, r, kvcmp, cmp_to_slc, e3):
    B, S, _ = p.shape
    nc = kvcmp.shape[2]
    qb = NSA_QB
    n_top = min(SLC_TOPN, S // SLC_BLOCK)
    qspec = lambda g: pl.BlockSpec((1, qb, LANES), lambda b, h, i: (b, i, G_NSA_Q + NSA_GROUP * h + g))
    rows = NSA_GROUP * qb
    return pl.pallas_call(
        functools.partial(_nsa_kernel, n_pick=max(n_top - 3, 0)),
        grid=(B, NSA_KV_HEADS, S // qb),
        in_specs=[
            qspec(0), qspec(1), qspec(2),
            pl.BlockSpec((1, 1, nc, LANES), lambda b, h, i: (b, h, 0, 0)),
            pl.BlockSpec((1, S, LANES), lambda b, h, i: (b, 0, G_SLC + h)),
            pl.BlockSpec((1, S, LANES), lambda b, h, i: (b, 0, G_WIN + h)),
            pl.BlockSpec((1, qb, LANES), lambda b, h, i: (b, i, R_GATE_BLOCK + h)),
            pl.BlockSpec(cmp_to_slc.shape, lambda b, h, i: (0, 0)),
            pl.BlockSpec(e3.shape, lambda b, h, i: (0, 0, 0)),
        ],
        out_specs=pl.BlockSpec((1, qb, NSA_GROUP * LANES), lambda b, h, i: (b, i, h)),
        out_shape=jax.ShapeDtypeStruct((B, S, NSA_HEADS * LANES), F32),
        scratch_shapes=[pltpu.VMEM((rows, LANES), F32), pltpu.VMEM((rows, LANES), F32), pltpu.VMEM((rows, LANES), F32)],
        compiler_params=_cparams(("arbitrary", "arbitrary", "arbitrary")),
        name="nsa_attention",
    )(p, p, p, kvcmp, p, p, r, cmp_to_slc, e3)


def _nsa_constants(S):
    n_cmp = (S - CMP_BLOCK) // CMP_STRIDE + 1
    nc = S // CMP_STRIDE
    n_slc = S // SLC_BLOCK
    c_start = jnp.arange(nc) * CMP_STRIDE
    s_start = jnp.arange(LANES) * SLC_BLOCK
    overlap = (jnp.minimum(c_start[:, None] + CMP_BLOCK, s_start[None, :] + SLC_BLOCK)
               - jnp.maximum(c_start[:, None], s_start[None, :]))
    m = jnp.clip(overlap, 0, None).astype(F32) / CMP_BLOCK
    m = jnp.where((jnp.arange(nc)[:, None] < n_cmp) & (jnp.arange(LANES)[None, :] < n_slc), m, 0.0)
    key_blk = (jnp.arange(S) // SLC_BLOCK).reshape(S // NSA_TK, 1, NSA_TK)
    e3 = (key_blk == jnp.arange(LANES)[None, :, None]).astype(BF16)
    return m.astype(BF16), e3


def _moba_kernel(q_ref, kv_ref, km_ref, e_ref, o_ref, m_scr, l_scr, acc_scr):
    ci = pl.program_id(2)
    qb = MOBA_BLOCK
    q = q_ref[0]
    km = km_ref[0]
    km_hi, km_lo = _split2(km)
    gate_t = _dot_nt(km_hi, q) + _dot_nt(km_lo, q)
    blk = lax.broadcasted_iota(jnp.int32, gate_t.shape, 0)
    sc = jnp.where(blk < ci, gate_t, -jnp.inf)
    sel = jnp.zeros(gate_t.shape, F32)
    for _ in range(MOBA_TOPK):
        best = jnp.max(sc, axis=0, keepdims=True)
        idx = jnp.min(jnp.where(sc == best, blk, gate_t.shape[0]), axis=0, keepdims=True)
        pick = (blk == idx) & (best > -jnp.inf)
        sel = jnp.where(pick, 1.0, sel)
        sc = jnp.where(pick, -jnp.inf, sc)
    bias = jnp.where(sel > 0.5, 0.0, NEG).T.astype(BF16)

    kv = kv_ref[0, pl.ds(pl.multiple_of(ci * qb, qb), qb), :]
    s = _dot_nt(q, kv)
    causal = lax.broadcasted_iota(jnp.int32, s.shape, 1) <= lax.broadcasted_iota(jnp.int32, s.shape, 0)
    s = jnp.where(causal, s, NEG)
    mx = jnp.max(s, axis=-1, keepdims=True)
    p = jnp.exp(s - mx)
    m_scr[...] = jnp.broadcast_to(mx, m_scr.shape)
    l_scr[...] = jnp.broadcast_to(jnp.sum(p, axis=-1, keepdims=True), l_scr.shape)
    acc_scr[...] = _dot(p.astype(BF16), kv)

    def body(j, carry):
        kvj = kv_ref[0, pl.ds(pl.multiple_of(j * qb, qb), qb), :]
        sj = _dot_nt(q, kvj) + _dot(bias, e_ref[j])
        _online_update(sj, kvj, m_scr, l_scr, acc_scr)
        return carry

    lax.fori_loop(0, ci, body, 0)
    o_ref[0] = acc_scr[...] / l_scr[...]


def _moba(p, kmean, e3):
    B, S, _ = p.shape
    qb = MOBA_BLOCK
    return pl.pallas_call(
        _moba_kernel,
        grid=(B, MOBA_HEADS, S // qb),
        in_specs=[
            pl.BlockSpec((1, qb, LANES), lambda b, h, i: (b, i, G_MOBA_Q + h)),
            pl.BlockSpec((1, S, LANES), lambda b, h, i: (b, 0, G_MOBA_KV + h)),
            pl.BlockSpec((1, MOBA_NBP, LANES), lambda b, h, i: (b, 0, h)),
            pl.BlockSpec(e3.shape, lambda b, h, i: (0, 0, 0)),
        ],
        out_specs=pl.BlockSpec((1, qb, LANES), lambda b, h, i: (b, i, h)),
        out_shape=jax.ShapeDtypeStruct((B, S, MOBA_HEADS * LANES), F32),
        scratch_shapes=[pltpu.VMEM((qb, LANES), F32), pltpu.VMEM((qb, LANES), F32), pltpu.VMEM((qb, LANES), F32)],
        compiler_params=_cparams(("arbitrary", "arbitrary", "arbitrary")),
        name="moba_attention",
    )(p, p, kmean, e3)


def _moba_constants(S):
    nb = S // MOBA_BLOCK
    e3 = (jnp.arange(nb)[:, None, None] == jnp.arange(MOBA_NBP)[None, :, None])
    return jnp.broadcast_to(e3, (nb, MOBA_NBP, MOBA_BLOCK)).astype(BF16)


def _rwkv_kernel(f_ref, mu_ref, wup_ref, aup_ref, gup_ref, vec_ref, bd_ref, tri_ref, o_ref, carry_scr, st_scr):
    i = pl.program_id(1)
    tb = f_ref.shape[1]
    C = RWKV_CHUNK
    W = RWKV_WIDTH

    @pl.when(i == 0)
    def _():
        carry_scr[...] = jnp.zeros(carry_scr.shape, F32)
        st_scr[...] = jnp.zeros(st_scr.shape, F32)

    feat = f_ref[0]
    rowi = lax.broadcasted_iota(jnp.int32, feat.shape, 0)
    prev = jnp.where(rowi == 0, carry_scr[0:1, :], pltpu.roll(feat, 1, 0))
    carry_scr[0:1, :] = feat[tb - 1:tb, :]
    xs = feat + (prev - feat) * mu_ref[...]
    r = xs[:, 0:W]
    k = xs[:, W:2 * W]
    v = xs[:, 2 * W:3 * W]
    wa = xs[:, 3 * W:3 * W + DECAY_LORA + AAA_LORA]
    gd = xs[:, 3 * W + DECAY_LORA + AAA_LORA:]
    w0, a0, k_k, k_a, r_k, ln_w, ln_b = (vec_ref[n:n + 1, :] for n in range(7))
    bd = bd_ref[...]
    hsum = lambda t: _mm3_exact_rhs(t, bd)

    logw = -DECAY_SCALE * jax.nn.sigmoid(w0 + _mm3(jnp.tanh(wa), wup_ref[...]))
    a = jax.nn.sigmoid(a0 + _mm3(wa, aup_ref[...]))
    gate = _mm3(jax.nn.sigmoid(gd), gup_ref[...])
    kk = k * k_k
    kk = kk / jnp.maximum(jnp.sqrt(hsum(kk * kk)), 1e-12)
    k = k * (1.0 + (a - 1.0) * k_a)
    bonus = hsum(r * k * r_k) * v
    kka = kk * a

    ri = lax.broadcasted_iota(jnp.int32, (C, C), 0)
    cj = lax.broadcasted_iota(jnp.int32, (C, C), 1)
    strict = cj < ri
    incl = cj <= ri
    eye = jnp.where(ri == cj, 1.0, 0.0)
    tri = tri_ref[...]

    for c in range(tb // C):
        cs = slice(c * C, (c + 1) * C)
        lw = logw[cs]
        hi, mid, lo = _split3(lw)
        cum = _dot(tri, hi) + _dot(tri, mid) + _dot(tri, lo)
        e_cum = jnp.exp(cum)
        e_neg = jnp.exp(-cum)
        e_end = jnp.exp(cum[C - 1:C, :] - cum)
        a_t = -kk[cs] * jnp.exp(cum - lw)
        r_t = r[cs] * e_cum
        b_t = kka[cs] * e_neg
        k_t = k[cs] * e_neg
        b_e = kka[cs] * e_end
        k_e = k[cs] * e_end
        d_end = jnp.exp(cum[C - 1:C, :])
        ys = []
        for h in range(RWKV_HEADS):
            hs = slice(h * HEAD_DIM, (h + 1) * HEAD_DIM)
            ah, rh, bh, kh, vh = a_t[:, hs], r_t[:, hs], b_t[:, hs], k_t[:, hs], v[cs, hs]
            ar = jnp.concatenate([ah, rh], axis=0)
            xb = _mm3(ar, bh, _dot_nt)
            xk = _mm3(ar, kh, _dot_nt)
            n = jnp.where(strict, xb[0:C], 0.0)
            m_ak = jnp.where(strict, xk[0:C], 0.0)
            m_rb = jnp.where(incl, xb[C:], 0.0)
            m_rk = jnp.where(incl, xk[C:], 0.0)
            tinv = eye + n
            npow = n
            step = 1
            while 2 * step < C:
                npow = _mm3(npow, npow)
                tinv = tinv + _mm3(tinv, npow)
                step *= 2
            g = _mm3(tinv, ah)
            u0 = _mm3(tinv, _mm3(m_ak, vh))
            r_y = rh + _mm3(m_rb, g)
            y0 = _mm3(m_rb, u0) + _mm3(m_rk, vh)
            p_st = eye * d_end[:, hs] + _mm3(b_e[:, hs], g, _dot_tn)
            q_st = _mm3(b_e[:, hs], u0, _dot_tn) + _mm3(k_e[:, hs], vh, _dot_tn)
            st = st_scr[h]
            y = _mm3(r_y, st) + y0
            st_scr[h] = _mm3(p_st, st) + q_st
            mean = jnp.mean(y, axis=-1, keepdims=True)
            yc = y - mean
            var = jnp.mean(yc * yc, axis=-1, keepdims=True)
            ys.append(yc * lax.rsqrt(var + GN_EPS))
        yn = jnp.concatenate(ys, axis=1)
        o_ref[0, cs, :] = (yn * ln_w + ln_b + bonus[cs]) * gate[cs]


def _rwkv(r, mu, wup, aup, gup, vecs, bd, tri):
    B, S, _ = r.shape
    tb = RWKV_TB
    full = lambda a: pl.BlockSpec(a.shape, lambda b, i: (0,) * a.ndim)
    return pl.pallas_call(
        _rwkv_kernel,
        grid=(B, S // tb),
        in_specs=[pl.BlockSpec((1, tb, RWKV_COLS), lambda b, i: (b, i, 0)),
                  full(mu), full(wup), full(aup), full(gup), full(vecs), full(bd), full(tri)],
        out_specs=pl.BlockSpec((1, tb, RWKV_WIDTH), lambda b, i: (b, i, 0)),
        out_shape=jax.ShapeDtypeStruct((B, S, RWKV_WIDTH), F32),
        scratch_shapes=[pltpu.VMEM((8, RWKV_COLS), F32), pltpu.VMEM((RWKV_HEADS, HEAD_DIM, HEAD_DIM), F32)],
        compiler_params=_cparams(("arbitrary", "arbitrary")),
        name="rwkv7",
    )(r, mu, wup, aup, gup, vecs, bd, tri)


def _rwkv_constants():
    head = jnp.arange(RWKV_WIDTH) // HEAD_DIM
    bd = (head[:, None] == head[None, :]).astype(BF16)
    t = jnp.arange(RWKV_CHUNK)
    tri = (t[None, :] <= t[:, None]).astype(BF16)
    return bd, tri


def _outproj_kernel(mn_ref, mm_ref, rw_ref, x_ref, g1_ref, gn_ref, gm_ref, wn_ref, wm_ref, wr_ref, o_ref):
    def head_norm(o, gain, width):
        lane = lax.broadcasted_iota(jnp.int32, o.shape, 1)
        o = jnp.where((lane & (LANES - 1)) >= HEAD_DIM, o, 0.0)
        ms = jnp.sum(o * o, axis=-1, keepdims=True) * (1.0 / width)
        return (o * lax.rsqrt(ms + NORM_EPS) * gain).astype(BF16)

    z = _dot(head_norm(mn_ref[0], gn_ref[...], NSA_WIDTH), wn_ref[...])
    z = z + _dot(head_norm(mm_ref[0], gm_ref[...], MOBA_WIDTH), wm_ref[...])
    z = z + _dot(rw_ref[0].astype(BF16), wr_ref[...])
    o_ref[0] = x_ref[0] + g1_ref[0] * z


def _outproj(mixn, mixm, orw, x, g1, gn, gm, wn, wm, wr):
    B, S, D = x.shape
    tm = 512
    full = lambda a: pl.BlockSpec(a.shape, lambda b, i: (0,) * a.ndim)
    tok = lambda a: pl.BlockSpec((1, tm, a.shape[2]), lambda b, i: (b, i, 0))
    return pl.pallas_call(
        _outproj_kernel,
        grid=(B, S // tm),
        in_specs=[tok(mixn), tok(mixm), tok(orw), tok(x), pl.BlockSpec((1, 1, D), lambda b, i: (b, 0, 0)),
                  full(gn), full(gm), full(wn), full(wm), full(wr)],
        out_specs=tok(x),
        out_shape=jax.ShapeDtypeStruct((B, S, D), F32),
        compiler_params=_cparams(("arbitrary", "arbitrary")),
        name="outproj",
    )(mixn, mixm, orw, x, g1, gn, gm, wn, wm, wr)


def _pad_heads(t, n_heads):
    t = t.reshape((n_heads, HEAD_DIM) + t.shape[1:])
    return jnp.concatenate([jnp.zeros_like(t), t], axis=1).reshape((n_heads * LANES,) + t.shape[2:])


def _route(logit_t, bias_col):
    aff = jax.nn.sigmoid(logit_t)
    biased = aff + bias_col
    row = lambda t, e: t[e:e + 1, :]
    gp = EXPERTS_PER_GROUP
    scores = []
    for g in range(N_EXPERT_GROUPS):
        a_, b_, c_, d_ = (row(biased, g * gp + j) for j in range(gp))
        hi1, lo1, hi2, lo2 = jnp.maximum(a_, b_), jnp.minimum(a_, b_), jnp.maximum(c_, d_), jnp.minimum(c_, d_)
        scores.append(jnp.maximum(hi1, hi2) + jnp.maximum(jnp.minimum(hi1, hi2), jnp.maximum(lo1, lo2)))
    best = jnp.zeros_like(scores[0], dtype=jnp.int32)
    best_s = scores[0]
    for g in range(1, N_EXPERT_GROUPS):
        better = scores[g] > best_s
        best = jnp.where(better, g, best)
        best_s = jnp.where(better, scores[g], best_s)

    def in_group(t, j):
        out = row(t, j)
        for g in range(1, N_EXPERT_GROUPS):
            out = jnp.where(best == g, row(t, g * gp + j), out)
        return out

    vals = [in_group(biased, j) for j in range(gp)]
    affs = [in_group(aff, j) for j in range(gp)]

    def first_argmax(vs):
        top = functools.reduce(jnp.maximum, vs)
        idx = jnp.full(top.shape, gp, jnp.int32)
        for j in reversed(range(gp)):
            idx = jnp.where(vs[j] == top, j, idx)
        return idx

    i1 = first_argmax(vals)
    i2 = first_argmax([jnp.where(i1 == j, -jnp.inf, vals[j]) for j in range(gp)])
    pick = lambda idx: functools.reduce(lambda acc, j: jnp.where(idx == j, affs[j], acc), range(gp), jnp.zeros_like(affs[0]))
    w1, w2 = pick(i1), pick(i2)
    tot = w1 + w2
    e_iota = lax.broadcasted_iota(jnp.int32, logit_t.shape, 0)
    e1 = best * gp + i1
    e2 = best * gp + i2
    return jnp.where(e_iota == e1, w1 / tot, 0.0) + jnp.where(e_iota == e2, w2 / tot, 0.0)


def _moe_kernel(x_ref, gain_ref, sc_ref, sh_ref, g2_ref, wrt_ref, rb_ref, wg_ref, wu_ref, wd_ref, fin_ref,
                o_ref, hb_scr, cb_scr, acc_scr, *, final):
    e = pl.program_id(1)
    tm = x_ref.shape[1]

    @pl.when(e == 0)
    def _():
        x = x_ref[0]
        ms = jnp.mean(x * x, axis=-1, keepdims=True)
        h = x * lax.rsqrt(ms + NORM_EPS) * gain_ref[...]
        h = h * (1.0 + sc_ref[0]) + sh_ref[0]
        hb_scr[...] = h.astype(BF16)
        comb = _route(_mm3(wrt_ref[...], h, _dot_nt), rb_ref[...])
        comb_t = jnp.concatenate([comb, jnp.zeros((LANES - N_EXPERTS, tm), F32)], axis=0).T
        for ee in range(N_EXPERTS):
            cb_scr[ee] = jnp.broadcast_to(comb_t[:, ee:ee + 1], (tm, LANES))
        acc_scr[...] = jnp.zeros(acc_scr.shape, F32)

    hb = hb_scr[...]
    hg = _dot(hb, wg_ref[0])
    hu = _dot(hb, wu_ref[0])
    cbe = cb_scr[e]
    he = hg * jax.nn.sigmoid(hg) * hu * jnp.concatenate([cbe] * (EXPERT_FF // LANES), axis=1)
    acc_scr[...] += _dot(he.astype(BF16), wd_ref[0])

    @pl.when(e == N_EXPERTS - 1)
    def _():
        out = x_ref[0] + g2_ref[0] * acc_scr[...]
        if final:
            ms = jnp.mean(out * out, axis=-1, keepdims=True)
            out = out * lax.rsqrt(ms + NORM_EPS) * fin_ref[...]
        o_ref[0] = out


def _moe(x, gain, sc, sh, g2, wrt, rb, wg, wu, wd, fin, final):
    B, S, D = x.shape
    tm = 512
    tpb = S // tm
    tok = pl.BlockSpec((1, tm, D), lambda i, e: (i // tpb, i % tpb, 0))
    per_b = pl.BlockSpec((1, 1, D), lambda i, e: (i // tpb, 0, 0))
    full = lambda a: pl.BlockSpec(a.shape, lambda i, e: (0,) * a.ndim)
    return pl.pallas_call(
        functools.partial(_moe_kernel, final=final),
        grid=(B * tpb, N_EXPERTS),
        in_specs=[tok, full(gain), per_b, per_b, per_b, full(wrt), full(rb),
                  pl.BlockSpec((1, D, EXPERT_FF), lambda i, e: (e, 0, 0)),
                  pl.BlockSpec((1, D, EXPERT_FF), lambda i, e: (e, 0, 0)),
                  pl.BlockSpec((1, EXPERT_FF, D), lambda i, e: (e, 0, 0)),
                  full(fin)],
        out_specs=tok,
        out_shape=jax.ShapeDtypeStruct((B, S, D), F32),
        scratch_shapes=[pltpu.VMEM((tm, D), BF16), pltpu.VMEM((N_EXPERTS, tm, LANES), F32), pltpu.VMEM((tm, D), F32)],
        compiler_params=_cparams(("arbitrary", "arbitrary")),
        name="moe",
    )(x, gain, sc, sh, g2, wrt, rb, wg, wu, wd, fin)


def kernel(x, c, positions, w_mod, b_mod, norm_mix, norm_ffn, w_in, nsa_phi_w1, nsa_phi_w2, nsa_phi_pos, rwkv_mu, rwkv_w_up, rwkv_w0, rwkv_a_up, rwkv_a0, rwkv_g_up, rwkv_k_k, rwkv_k_a, rwkv_r_k, rwkv_ln_w, rwkv_ln_b, norm_nsa_out, norm_moba_out, w_out, w_router, router_bias, moe_w_gate, moe_w_up, moe_w_down, norm_final):
    B, S, D = x.shape
    depth = w_in.shape[0]
    assert S % NSA_TK == 0 and S % MOBA_BLOCK == 0 and S >= WINDOW + NSA_QB
    assert S // SLC_BLOCK <= LANES and S // MOBA_BLOCK <= MOBA_NBP

    inv = ROPE_THETA ** (-jnp.arange(0, HEAD_DIM, 2, dtype=F32) / HEAD_DIM)
    ang = positions.astype(F32)[..., None] * inv
    cos, sin = jnp.cos(ang), jnp.sin(ang)
    one, zero = jnp.ones((B, S, HEAD_DIM), F32), jnp.zeros((B, S, HEAD_DIM), F32)
    cos_t = jnp.concatenate([cos, cos, one], axis=-1)
    sin_s = jnp.concatenate([-sin, sin, zero], axis=-1)
    nc = S // CMP_STRIDE
    n_cmp = (S - CMP_BLOCK) // CMP_STRIDE + 1
    pad_c = lambda t, fill: jnp.concatenate([t[:, CMP_BLOCK - 1::CMP_STRIDE][:, :n_cmp],
                                             jnp.full((B, nc - n_cmp, LANES), fill, F32)], axis=1)
    cos_c, sin_c = pad_c(cos_t, 1.0), pad_c(sin_s, 0.0)

    cmp_to_slc, e3_nsa = _nsa_constants(S)
    e3_moba = _moba_constants(S)
    bd, tri = _rwkv_constants()
    mod = _modulation(c, w_mod, b_mod)
    wrt = w_router.T
    rb = router_bias.reshape(N_EXPERTS, 1)
    fin = norm_final.reshape(1, D)

    for l in range(depth):
        sh1, sc1, g1, sh2, sc2, g2 = (mod[l, :B, i * D:(i + 1) * D].reshape(B, 1, D) for i in range(6))
        p, r, km = _inproj(x, norm_mix[l].reshape(1, D), sc1, sh1, cos_t, sin_s, _prep_w_in(w_in[l]))

        w1, w2, pos4 = _prep_compress(nsa_phi_w1[l], nsa_phi_w2[l], nsa_phi_pos[l])
        chunks = lambda g: p[:, :, g * LANES:(g + 1) * LANES].reshape(B, nc, CMP_STRIDE * LANES)
        kvcmp = _compress(chunks(G_KC), chunks(G_VC), pos4, w1, w2, cos_c, sin_c)
        mixn = _nsa(p, r, kvcmp, cmp_to_slc, e3_nsa)

        kmean = km[:, :, 0, :]
        kmean = jnp.concatenate([kmean, jnp.zeros((B, MOBA_NBP - kmean.shape[1], kmean.shape[2]), F32)], axis=1)
        mixm = _moba(p, kmean, e3_moba)

        zl = jnp.zeros((DECAY_LORA, RWKV_WIDTH), F32)
        wup = jnp.concatenate([rwkv_w_up[l], zl], axis=0)
        aup = jnp.concatenate([zl, rwkv_a_up[l]], axis=0)
        vecs = jnp.stack([rwkv_w0[l], rwkv_a0[l], rwkv_k_k[l], rwkv_k_a[l], rwkv_r_k[l].reshape(-1),
                          rwkv_ln_w[l], rwkv_ln_b[l], jnp.zeros((RWKV_WIDTH,), F32)])
        orw = _rwkv(r, rwkv_mu[l].reshape(1, -1), wup, aup, rwkv_g_up[l], vecs, bd, tri)

        wo = w_out[l]
        x = _outproj(mixn, mixm, orw, x, g1,
                     _pad_heads(norm_nsa_out[l], NSA_HEADS).reshape(1, -1),
                     _pad_heads(norm_moba_out[l], MOBA_HEADS).reshape(1, -1),
                     _pad_heads(wo[:NSA_WIDTH], NSA_HEADS).astype(BF16),
                     _pad_heads(wo[NSA_WIDTH:NSA_WIDTH + MOBA_WIDTH], MOBA_HEADS).astype(BF16),
                     wo[NSA_WIDTH + MOBA_WIDTH:].astype(BF16))
        x = _moe(x, norm_ffn[l].reshape(1, D), sc2, sh2, g2, wrt, rb,
                 moe_w_gate[l].astype(BF16), moe_w_up[l].astype(BF16), moe_w_down[l].astype(BF16),
                 fin, final=(l == depth - 1))
    return x
```

```python
import functools

import jax
import jax.numpy as jnp
from jax import lax
from jax.experimental import pallas as pl
from jax.experimental.pallas import tpu as pltpu

F32 = jnp.float32
BF16 = jnp.bfloat16

HEAD_DIM = 64
LANES = 128
ROPE_THETA = 10000.0
NORM_EPS = 1e-6
NEG = -1e30

NSA_HEADS = 6
NSA_KV_HEADS = 2
NSA_GROUP = NSA_HEADS // NSA_KV_HEADS
NSA_WIDTH = NSA_HEADS * HEAD_DIM
NSA_KV_WIDTH = NSA_KV_HEADS * HEAD_DIM
CMP_BLOCK = 32
CMP_STRIDE = 16
SLC_BLOCK = 64
SLC_TOPN = 16
WINDOW = 512
NSA_QB = 128
NSA_TK = 2 * NSA_QB
NSA_OUT_LANES = 256

MOBA_HEADS = 4
MOBA_WIDTH = MOBA_HEADS * HEAD_DIM
MOBA_BLOCK = 256
MOBA_TOPK = 3
MOBA_NBP = 128

RWKV_HEADS = 6
RWKV_WIDTH = RWKV_HEADS * HEAD_DIM
DECAY_LORA = 64
AAA_LORA = 64
GATE_LORA = 128
RWKV_COLS = 3 * RWKV_WIDTH + DECAY_LORA + AAA_LORA + GATE_LORA
DECAY_SCALE = 0.606531
GN_EPS = 64e-5
RWKV_CHUNK = 64
RWKV_TB = 256

N_EXPERTS = 16
N_EXPERT_GROUPS = 4
EXPERTS_PER_GROUP = N_EXPERTS // N_EXPERT_GROUPS
EXPERT_FF = 256

G_NSA_Q = 0
G_SLC = 6
G_WIN = 8
G_KC = 10
G_VC = 11
G_MOBA_Q = 12
G_MOBA_KV = 16
N_GROUPS = 20
ROPED_GROUPS = tuple(range(0, 10)) + tuple(range(12, 20))
P_COLS = N_GROUPS * LANES
R_GATE_BLOCK = RWKV_COLS // LANES
R_COLS = RWKV_COLS + NSA_KV_HEADS * LANES

VMEM_LIMIT = 56 * 1024 * 1024


def _cparams(sem):
    return pltpu.CompilerParams(dimension_semantics=sem, vmem_limit_bytes=VMEM_LIMIT)


def _dot(a, b):
    return jnp.dot(a, b, preferred_element_type=F32)


def _dot_nt(a, b):
    return lax.dot_general(a, b, (((1,), (1,)), ((), ())), preferred_element_type=F32)


def _dot_tn(a, b):
    return lax.dot_general(a, b, (((0,), (0,)), ((), ())), preferred_element_type=F32)


def _split2(x):
    hi = x.astype(BF16)
    lo = (x - hi.astype(F32)).astype(BF16)
    return hi, lo


def _split3(x):
    hi = x.astype(BF16)
    r1 = x - hi.astype(F32)
    mid = r1.astype(BF16)
    lo = (r1 - mid.astype(F32)).astype(BF16)
    return hi, mid, lo


def _mm3(a, b, dot=_dot):
    ah, al = _split2(a)
    bh, bl = _split2(b)
    return dot(ah, bh) + dot(ah, bl) + dot(al, bh)


def _mm1(a, b, dot=_dot):
    return dot(a.astype(BF16), b.astype(BF16))


def _mm3_exact_rhs(a, b_bf16):
    hi, mid, lo = _split3(a)
    return _dot(hi, b_bf16) + _dot(mid, b_bf16) + _dot(lo, b_bf16)


def _rope(y, cos_t, sin_s):
    lane = lax.broadcasted_iota(jnp.int32, y.shape, 1)
    rot = jnp.where(lane < HEAD_DIM // 2, pltpu.roll(y, LANES - HEAD_DIM // 2, 1), pltpu.roll(y, HEAD_DIM // 2, 1))
    return y * cos_t + rot * sin_s


def _mod_kernel(c_ref, w_ref, b_ref, o_ref):
    c = c_ref[...]
    ca = c * jax.nn.sigmoid(c)
    o_ref[0] = _dot(ca.astype(BF16), w_ref[0].astype(BF16)) + b_ref[0]


def _modulation(c, w_mod, b_mod):
    B, D = c.shape
    L, _, N = w_mod.shape
    tn = 512
    c8 = jnp.zeros((8, D), F32).at[:B].set(c)
    return pl.pallas_call(
        _mod_kernel,
        grid=(L, N // tn),
        in_specs=[
            pl.BlockSpec((8, D), lambda l, j: (0, 0)),
            pl.BlockSpec((1, D, tn), lambda l, j: (l, 0, j)),
            pl.BlockSpec((1, 1, tn), lambda l, j: (l, 0, j)),
        ],
        out_specs=pl.BlockSpec((1, 8, tn), lambda l, j: (l, 0, j)),
        out_shape=jax.ShapeDtypeStruct((L, 8, N), F32),
        compiler_params=_cparams(("arbitrary", "arbitrary")),
        name="modulation",
    )(c8, w_mod, b_mod.reshape(L, 1, N))


def _ones_over_values_t(y):
    yt = y.T
    row = lax.broadcasted_iota(jnp.int32, yt.shape, 0)
    return jnp.where(row < HEAD_DIM, 1.0, yt).astype(BF16)


def _inproj_kernel(x_ref, gain_ref, sc_ref, sh_ref, cos_ref, sin_ref, w_ref, p_ref, r_ref, km_ref,
                   vs_ref, vw_ref, mv_ref):
    x = x_ref[0]
    ms = jnp.mean(x * x, axis=-1, keepdims=True)
    h = x * lax.rsqrt(ms + NORM_EPS) * gain_ref[...]
    h = h * (1.0 + sc_ref[0]) + sh_ref[0]
    hb = h.astype(BF16)
    cos_t = cos_ref[0]
    sin_s = sin_ref[0]
    per = 4
    for g0 in range(0, N_GROUPS, per):
        y4 = _dot(hb, w_ref[:, g0 * LANES:(g0 + per) * LANES])
        for j in range(per):
            g = g0 + j
            y = y4[:, j * LANES:(j + 1) * LANES]
            if g in ROPED_GROUPS:
                y = _rope(y, cos_t, sin_s)
            if g >= G_MOBA_KV:
                km = jnp.mean(y, axis=0, keepdims=True)
                km_ref[0, 0, :, (g - G_MOBA_KV) * LANES:(g - G_MOBA_KV + 1) * LANES] = jnp.broadcast_to(km, (8, LANES))
                mv_ref[0, g - G_MOBA_KV, 0] = _ones_over_values_t(y)
            elif G_SLC <= g < G_WIN:
                vs_ref[0, g - G_SLC, 0] = _ones_over_values_t(y)
            elif G_WIN <= g < G_KC:
                yt = _ones_over_values_t(y)
                for t in range(y.shape[0] // NSA_QB):
                    vw_ref[0, g - G_WIN, t] = yt[:, t * NSA_QB:(t + 1) * NSA_QB]
            p_ref[0, :, g * LANES:(g + 1) * LANES] = y.astype(BF16)
    r_ref[0] = _dot(hb, w_ref[:, P_COLS:])


def _inproj(x, gain, sc, sh, cos_t, sin_s, w):
    B, S, D = x.shape
    tm = MOBA_BLOCK
    return pl.pallas_call(
        _inproj_kernel,
        grid=(B, S // tm),
        in_specs=[
            pl.BlockSpec((1, tm, D), lambda b, i: (b, i, 0)),
            pl.BlockSpec((1, D), lambda b, i: (0, 0)),
            pl.BlockSpec((1, 1, D), lambda b, i: (b, 0, 0)),
            pl.BlockSpec((1, 1, D), lambda b, i: (b, 0, 0)),
            pl.BlockSpec((1, tm, LANES), lambda b, i: (b, i, 0)),
            pl.BlockSpec((1, tm, LANES), lambda b, i: (b, i, 0)),
            pl.BlockSpec((D, P_COLS + R_COLS), lambda b, i: (0, 0)),
        ],
        out_specs=[
            pl.BlockSpec((1, tm, P_COLS), lambda b, i: (b, i, 0)),
            pl.BlockSpec((1, tm, R_COLS), lambda b, i: (b, i, 0)),
            pl.BlockSpec((1, 1, 8, MOBA_HEADS * LANES), lambda b, i: (b, i, 0, 0)),
            pl.BlockSpec((1, NSA_KV_HEADS, 1, LANES, tm), lambda b, i: (b, 0, i, 0, 0)),
            pl.BlockSpec((1, NSA_KV_HEADS, tm // NSA_QB, LANES, NSA_QB), lambda b, i: (b, 0, i, 0, 0)),
            pl.BlockSpec((1, MOBA_HEADS, 1, LANES, tm), lambda b, i: (b, 0, i, 0, 0)),
        ],
        out_shape=[
            jax.ShapeDtypeStruct((B, S, P_COLS), BF16),
            jax.ShapeDtypeStruct((B, S, R_COLS), F32),
            jax.ShapeDtypeStruct((B, S // tm, 8, MOBA_HEADS * LANES), F32),
            jax.ShapeDtypeStruct((B, NSA_KV_HEADS, S // tm, LANES, tm), BF16),
            jax.ShapeDtypeStruct((B, NSA_KV_HEADS, S // NSA_QB, LANES, NSA_QB), BF16),
            jax.ShapeDtypeStruct((B, MOBA_HEADS, S // tm, LANES, tm), BF16),
        ],
        compiler_params=_cparams(("arbitrary", "arbitrary")),
        name="inproj",
    )(x, gain, sc, sh, cos_t, sin_s, w)


def _prep_w_in(w):
    D = w.shape[0]
    o = 0
    parts = {}
    for name, width in (("nq", NSA_WIDTH), ("nkc", NSA_KV_WIDTH), ("nvc", NSA_KV_WIDTH), ("nks", NSA_KV_WIDTH),
                        ("nvs", NSA_KV_WIDTH), ("nkw", NSA_KV_WIDTH), ("nvw", NSA_KV_WIDTH), ("ngate", NSA_HEADS * 3),
                        ("mq", MOBA_WIDTH), ("mk", MOBA_WIDTH), ("mv", MOBA_WIDTH), ("rf", RWKV_COLS)):
        parts[name] = w[:, o:o + width]
        o += width
    hd = lambda t, h: t[:, h * HEAD_DIM:(h + 1) * HEAD_DIM]
    z = jnp.zeros((D, HEAD_DIM), F32)
    scale = HEAD_DIM ** -0.5
    cols = []
    for h in range(NSA_HEADS):
        cols += [hd(parts["nq"], h) * scale, z]
    for h in range(NSA_KV_HEADS):
        cols += [hd(parts["nks"], h), hd(parts["nvs"], h)]
    for h in range(NSA_KV_HEADS):
        cols += [hd(parts["nkw"], h), hd(parts["nvw"], h)]
    cols += [parts["nkc"], parts["nvc"]]
    for h in range(MOBA_HEADS):
        cols += [hd(parts["mq"], h) * scale, z]
    for h in range(MOBA_HEADS):
        cols += [hd(parts["mk"], h), hd(parts["mv"], h)]
    cols += [parts["rf"]]
    per = NSA_GROUP * 3
    for h in range(NSA_KV_HEADS):
        cols += [parts["ngate"][:, h * per:(h + 1) * per], jnp.zeros((D, LANES - per), F32)]
    return jnp.concatenate(cols, axis=1).astype(BF16)


def _cmp_kernel(kc_ref, vc_ref, pos_ref, w1_ref, w2_ref, cos_ref, sin_ref, o_ref, ot_ref):
    nc = kc_ref.shape[1]

    def hidden(t_ref, ia, ib):
        t = t_ref[0].astype(F32)
        a = _dot((t + pos_ref[ia:ia + 1, :]).astype(BF16), w1_ref[ia])
        b = _dot((t + pos_ref[ib:ib + 1, :]).astype(BF16), w1_ref[ib])
        return jax.nn.gelu(a + pltpu.roll(b, nc - 1, 0)).astype(BF16)

    gk = hidden(kc_ref, 0, 1)
    gv = hidden(vc_ref, 2, 3)
    for h in range(NSA_KV_HEADS):
        y = _dot(gk, w2_ref[2 * h]) + _dot(gv, w2_ref[2 * h + 1])
        y = _rope(y, cos_ref[0], sin_ref[0])
        o_ref[0, h] = y.astype(BF16)
        ot_ref[0, h] = _ones_over_values_t(y)


def _compress(kc16, vc16, pos4, w1, w2, cos_c, sin_c):
    B, nc, K = kc16.shape
    return pl.pallas_call(
        _cmp_kernel,
        grid=(B,),
        in_specs=[
            pl.BlockSpec((1, nc, K), lambda b: (b, 0, 0)),
            pl.BlockSpec((1, nc, K), lambda b: (b, 0, 0)),
            pl.BlockSpec((4, K), lambda b: (0, 0)),
            pl.BlockSpec((4, K, LANES), lambda b: (0, 0, 0)),
            pl.BlockSpec((4, LANES, LANES), lambda b: (0, 0, 0)),
            pl.BlockSpec((1, nc, LANES), lambda b: (b, 0, 0)),
            pl.BlockSpec((1, nc, LANES), lambda b: (b, 0, 0)),
        ],
        out_specs=[pl.BlockSpec((1, NSA_KV_HEADS, nc, LANES), lambda b: (b, 0, 0, 0)),
                   pl.BlockSpec((1, NSA_KV_HEADS, LANES, nc), lambda b: (b, 0, 0, 0))],
        out_shape=[jax.ShapeDtypeStruct((B, NSA_KV_HEADS, nc, LANES), BF16),
                   jax.ShapeDtypeStruct((B, NSA_KV_HEADS, LANES, nc), BF16)],
        compiler_params=_cparams(("arbitrary",)),
        name="nsa_compress",
    )(kc16, vc16, pos4, w1, w2, cos_c, sin_c)


def _prep_compress(phi_w1, phi_w2, phi_pos):
    half = CMP_BLOCK // 2
    eye = jnp.eye(NSA_KV_HEADS, dtype=F32)
    w1, pos = [], []
    for t in range(2):
        for part in range(2):
            w = phi_w1[t, part * half:(part + 1) * half]
            w1.append(jnp.einsum("lde,kK->lkdKe", w, eye).reshape(half * NSA_KV_WIDTH, NSA_KV_WIDTH))
            p = phi_pos[t, part * half:(part + 1) * half]
            pos.append(jnp.broadcast_to(p[:, None, :], (half, NSA_KV_HEADS, HEAD_DIM)).reshape(-1))
    w2 = []
    for h in range(NSA_KV_HEADS):
        for t in range(2):
            m = jnp.zeros((LANES, LANES), F32)
            m = m.at[h * HEAD_DIM:(h + 1) * HEAD_DIM, t * HEAD_DIM:(t + 1) * HEAD_DIM].set(phi_w2[t])
            w2.append(m)
    return jnp.stack(w1).astype(BF16), jnp.stack(w2).astype(BF16), jnp.stack(pos)


def _softmax_keys(s_t, mask):
    sm = jnp.where(mask, s_t, NEG)
    mx = jnp.max(sm, axis=0, keepdims=True)
    e = jnp.where(mask, jnp.exp(sm - mx), 0.0)
    den = jnp.maximum(jnp.sum(e, axis=0, keepdims=True), 1e-30)
    return e * (1.0 / den)


def _attend_tiles(n_loop, scores, mask_last, values_t, s_a, s_b, p_scr, alpha_scr, m_scr, acc_scr):
    m_scr[...] = jnp.full(m_scr.shape, -jnp.inf, F32)
    alpha_scr[...] = jnp.ones(alpha_scr.shape, F32)
    acc_scr[...] = jnp.zeros(acc_scr.shape, F32)
    p_scr[...] = jnp.zeros(p_scr.shape, BF16)

    def apply_weights(j):
        acc_scr[...] = alpha_scr[0:1, :] * acc_scr[...] + _dot(values_t(j), p_scr[...])

    def softmax_tile(s_t):
        m_prev = m_scr[0:1, :]
        m_new = jnp.maximum(m_prev, jnp.max(s_t, axis=0, keepdims=True))
        alpha_scr[0:1, :] = jnp.exp(m_prev - m_new)
        p_scr[...] = jnp.exp(s_t - m_new).astype(BF16)
        m_scr[0:1, :] = m_new

    def step(j, cur, nxt):
        apply_weights(jnp.maximum(j - 1, 0))
        softmax_tile(cur[...])
        nxt[...] = scores(j + 1)

    s_a[...] = scores(0)

    def pair(i, carry):
        step(2 * i, s_a, s_b)
        step(2 * i + 1, s_b, s_a)
        return carry

    lax.fori_loop(0, n_loop // 2, pair, 0)

    @pl.when(n_loop % 2 == 1)
    def _():
        step(n_loop - 1, s_a, s_a)

    apply_weights(jnp.maximum(n_loop - 1, 0))
    softmax_tile(mask_last(s_a[...]))
    apply_weights(n_loop)
    return acc_scr[...]


def _attend_scratch(tk, nq):
    return [pltpu.VMEM((tk, nq), F32), pltpu.VMEM((tk, nq), F32), pltpu.VMEM((tk, nq), BF16),
            pltpu.VMEM((8, nq), F32), pltpu.VMEM((8, nq), F32), pltpu.VMEM((LANES, nq), F32)]


def _nsa_kernel(q0_ref, q1_ref, q2_ref, kvc_ref, kvct_ref, kvs_ref, vst_ref, kvw_ref, vwt_ref, gate_ref, mt_ref,
                et_ref, o_ref, s_a, s_b, p_scr, alpha_scr, m_scr, acc_scr, *, n_pick):
    ci = pl.program_id(2)
    qb = NSA_QB
    nq = NSA_GROUP * qb
    q = jnp.concatenate([q0_ref[0], q1_ref[0], q2_ref[0]], axis=0)

    def t_of_lanes(shape):
        return ci * qb + (lax.broadcasted_iota(jnp.int32, shape, 1) & (qb - 1))

    kc = kvc_ref[0, 0]
    nc = kc.shape[0]
    s_t = _dot_nt(kc, q)
    cend = lax.broadcasted_iota(jnp.int32, (nc, nq), 0) * CMP_STRIDE + (CMP_BLOCK - 1)
    p_c = _softmax_keys(s_t, cend <= t_of_lanes((nc, nq)))
    o_c = _dot(kvct_ref[0, 0], p_c.astype(BF16))
    o_w = _nsa_window(ci, q, kvw_ref, vwt_ref, t_of_lanes)

    hi, mid, lo = _split3(p_c[:, 0:qb] + p_c[:, qb:2 * qb] + p_c[:, 2 * qb:3 * qb])
    mt = mt_ref[...]
    imp_t = _dot(mt, hi) + _dot(mt, mid) + _dot(mt, lo)
    blk = lax.broadcasted_iota(jnp.int32, imp_t.shape, 0)
    cur = (ci * qb + lax.broadcasted_iota(jnp.int32, imp_t.shape, 1)) // SLC_BLOCK
    forced = (blk == 0) | (blk == cur) | (blk == cur - 1)
    free = (blk <= cur) & jnp.logical_not(forced)
    picked = jnp.zeros(imp_t.shape, F32)
    sc = jnp.where(free, imp_t, -1.0)
    for _ in range(n_pick):
        best = jnp.max(sc, axis=0, keepdims=True)
        idx = jnp.min(jnp.where(sc == best, blk, imp_t.shape[0]), axis=0, keepdims=True)
        pick = blk == idx
        picked = jnp.where(pick, 1.0, picked)
        sc = jnp.where(pick, -2.0, sc)
    sel = jnp.where(free, picked, jnp.where(forced, 1.0, 0.0))
    bias = jnp.where(sel > 0.5, 0.0, NEG).T.astype(BF16)
    w_nt = jnp.concatenate([q, jnp.concatenate([bias] * NSA_GROUP, axis=0)], axis=1)

    tk = NSA_TK
    last = (ci * qb) // tk

    def scores(j):
        kv = kvs_ref[0, pl.ds(pl.multiple_of(j * tk, tk), tk), :]
        return _dot_nt(jnp.concatenate([kv, et_ref[j]], axis=1), w_nt)

    def causal_edge(s_t):
        kpos = last * tk + lax.broadcasted_iota(jnp.int32, (tk, nq), 0)
        return jnp.where(kpos <= t_of_lanes((tk, nq)), s_t, NEG)

    acc = _attend_tiles(last, scores, causal_edge, lambda j: vst_ref[0, 0, j],
                        s_a, s_b, p_scr, alpha_scr, m_scr, acc_scr)
    o_s = acc[HEAD_DIM:] * (1.0 / acc[0:1])

    gt = jax.nn.sigmoid(gate_ref[0]).T
    outs = []
    for g in range(NSA_GROUP):
        ls = slice(g * qb, (g + 1) * qb)
        outs.append(gt[3 * g:3 * g + 1] * o_c[HEAD_DIM:, ls] + gt[3 * g + 1:3 * g + 2] * o_s[:, ls]
                    + gt[3 * g + 2:3 * g + 3] * o_w[:, ls])
    outs.append(jnp.zeros((NSA_OUT_LANES - NSA_GROUP * HEAD_DIM, qb), F32))
    o_ref[0] = jnp.concatenate(outs, axis=0).T


def _nsa_window(ci, q, kvw_ref, vwt_ref, t_of_lanes):
    qb = NSA_QB
    nq = q.shape[0]
    n_wt = (WINDOW + qb) // qb
    first = jnp.maximum(ci - WINDOW // qb, 0)
    span = n_wt * qb
    kvw = kvw_ref[0, pl.ds(pl.multiple_of(first * qb, qb), span), :]
    s_t = _dot_nt(kvw, q)
    dist = t_of_lanes((span, nq)) - (first * qb + lax.broadcasted_iota(jnp.int32, (span, nq), 0))
    mask = (dist >= 0) & (dist < WINDOW)
    sm = jnp.where(mask, s_t, NEG)
    e_w = jnp.where(mask, jnp.exp(sm - jnp.max(sm, axis=0, keepdims=True)), 0.0)
    vw_t = jnp.concatenate([vwt_ref[0, 0, first + t] for t in range(n_wt)], axis=1)
    acc_w = _dot(vw_t, e_w.astype(BF16))
    return acc_w[HEAD_DIM:] * (1.0 / acc_w[0:1])


def _nsa(p, r, kvcmp, kvcmp_t, vs_t, vw_t, cmp_to_slc_t, e3):
    B, S, _ = p.shape
    nc = kvcmp.shape[2]
    qb = NSA_QB
    n_top = min(SLC_TOPN, S // SLC_BLOCK)
    qspec = lambda g: pl.BlockSpec((1, qb, LANES), lambda b, h, i: (b, i, G_NSA_Q + NSA_GROUP * h + g))
    nq = NSA_GROUP * qb
    return pl.pallas_call(
        functools.partial(_nsa_kernel, n_pick=max(n_top - 3, 0)),
        grid=(B, NSA_KV_HEADS, S // qb),
        in_specs=[
            qspec(0), qspec(1), qspec(2),
            pl.BlockSpec((1, 1, nc, LANES), lambda b, h, i: (b, h, 0, 0)),
            pl.BlockSpec((1, 1, LANES, nc), lambda b, h, i: (b, h, 0, 0)),
            pl.BlockSpec((1, S, LANES), lambda b, h, i: (b, 0, G_SLC + h)),
            pl.BlockSpec((1, 1) + vs_t.shape[2:], lambda b, h, i: (b, h, 0, 0, 0)),
            pl.BlockSpec((1, S, LANES), lambda b, h, i: (b, 0, G_WIN + h)),
            pl.BlockSpec((1, 1) + vw_t.shape[2:], lambda b, h, i: (b, h, 0, 0, 0)),
            pl.BlockSpec((1, qb, LANES), lambda b, h, i: (b, i, R_GATE_BLOCK + h)),
            pl.BlockSpec(cmp_to_slc_t.shape, lambda b, h, i: (0, 0)),
            pl.BlockSpec(e3.shape, lambda b, h, i: (0, 0, 0)),
        ],
        out_specs=pl.BlockSpec((1, qb, NSA_OUT_LANES), lambda b, h, i: (b, i, h)),
        out_shape=jax.ShapeDtypeStruct((B, S, NSA_KV_HEADS * NSA_OUT_LANES), F32),
        scratch_shapes=_attend_scratch(NSA_TK, nq),
        compiler_params=_cparams(("arbitrary", "arbitrary", "arbitrary")),
        name="nsa_attention",
    )(p, p, p, kvcmp, kvcmp_t, p, vs_t, p, vw_t, r, cmp_to_slc_t, e3)


def _nsa_constants(S):
    n_cmp = (S - CMP_BLOCK) // CMP_STRIDE + 1
    nc = S // CMP_STRIDE
    n_slc = S // SLC_BLOCK
    c_start = jnp.arange(nc) * CMP_STRIDE
    s_start = jnp.arange(LANES) * SLC_BLOCK
    overlap = (jnp.minimum(c_start[None, :] + CMP_BLOCK, s_start[:, None] + SLC_BLOCK)
               - jnp.maximum(c_start[None, :], s_start[:, None]))
    m_t = jnp.clip(overlap, 0, None).astype(F32) / CMP_BLOCK
    m_t = jnp.where((jnp.arange(nc)[None, :] < n_cmp) & (jnp.arange(LANES)[:, None] < n_slc), m_t, 0.0)
    key_blk = (jnp.arange(S) // SLC_BLOCK).reshape(S // NSA_TK, NSA_TK, 1)
    e3 = (key_blk == jnp.arange(LANES)[None, None, :]).astype(BF16)
    return m_t.astype(BF16), e3


def _moba_kernel(q_ref, kv_ref, vt_ref, km_ref, o_ref, sel_scr, s_a, s_b, p_scr, alpha_scr, m_scr, acc_scr):
    ci = pl.program_id(2)
    qb = MOBA_BLOCK
    q = q_ref[0]
    km = km_ref[0]
    km_hi, km_lo = _split2(km)
    gate_t = _dot_nt(km_hi, q) + _dot_nt(km_lo, q)
    blk = lax.broadcasted_iota(jnp.int32, gate_t.shape, 0)
    valid = blk < ci
    sc = jnp.where(valid, gate_t, -jnp.inf)
    picked = jnp.zeros(gate_t.shape, F32)
    for _ in range(MOBA_TOPK):
        best = jnp.max(sc, axis=0, keepdims=True)
        idx = jnp.min(jnp.where(sc == best, blk, gate_t.shape[0]), axis=0, keepdims=True)
        pick = blk == idx
        picked = jnp.where(pick, 1.0, picked)
        sc = jnp.where(pick, -jnp.inf, sc)
    sel = jnp.where(valid, picked, jnp.where(blk == ci, 1.0, 0.0))
    sel_scr[...] = jnp.where(sel > 0.5, 0.0, NEG)

    def scores(j):
        kvj = kv_ref[0, pl.ds(pl.multiple_of(j * qb, qb), qb), :]
        return _dot_nt(kvj, q) + sel_scr[pl.ds(j, 1), :]

    def causal_edge(s_t):
        causal = lax.broadcasted_iota(jnp.int32, s_t.shape, 0) <= lax.broadcasted_iota(jnp.int32, s_t.shape, 1)
        return jnp.where(causal, s_t, NEG)

    acc = _attend_tiles(ci, scores, causal_edge, lambda j: vt_ref[0, 0, j],
                        s_a, s_b, p_scr, alpha_scr, m_scr, acc_scr)
    o_ref[0] = (acc * (1.0 / acc[0:1])).T


def _moba(p, mv_t, kmean):
    B, S, _ = p.shape
    qb = MOBA_BLOCK
    return pl.pallas_call(
        _moba_kernel,
        grid=(B, MOBA_HEADS, S // qb),
        in_specs=[
            pl.BlockSpec((1, qb, LANES), lambda b, h, i: (b, i, G_MOBA_Q + h)),
            pl.BlockSpec((1, S, LANES), lambda b, h, i: (b, 0, G_MOBA_KV + h)),
            pl.BlockSpec((1, 1) + mv_t.shape[2:], lambda b, h, i: (b, h, 0, 0, 0)),
            pl.BlockSpec((1, MOBA_NBP, LANES), lambda b, h, i: (b, 0, h)),
        ],
        out_specs=pl.BlockSpec((1, qb, LANES), lambda b, h, i: (b, i, h)),
        out_shape=jax.ShapeDtypeStruct((B, S, MOBA_HEADS * LANES), F32),
        scratch_shapes=[pltpu.VMEM((MOBA_NBP, qb), F32)] + _attend_scratch(qb, qb),
        compiler_params=_cparams(("arbitrary", "arbitrary", "arbitrary")),
        name="moba_attention",
    )(p, p, mv_t, kmean)


def _rwkv_kernel(f_ref, mu_ref, wup_ref, aup_ref, gup_ref, vec_ref, bd_ref, tri_ref, o_ref, carry_scr, st_scr):
    i = pl.program_id(1)
    tb = f_ref.shape[1]
    C = RWKV_CHUNK
    W = RWKV_WIDTH

    @pl.when(i == 0)
    def _():
        carry_scr[...] = jnp.zeros(carry_scr.shape, F32)
        st_scr[...] = jnp.zeros(st_scr.shape, F32)

    feat = f_ref[0]
    rowi = lax.broadcasted_iota(jnp.int32, feat.shape, 0)
    prev = jnp.where(rowi == 0, carry_scr[0:1, :], pltpu.roll(feat, 1, 0))
    carry_scr[0:1, :] = feat[tb - 1:tb, :]
    xs = feat + (prev - feat) * mu_ref[...]
    r = xs[:, 0:W]
    k = xs[:, W:2 * W]
    v = xs[:, 2 * W:3 * W]
    wa = xs[:, 3 * W:3 * W + DECAY_LORA + AAA_LORA]
    gd = xs[:, 3 * W + DECAY_LORA + AAA_LORA:]
    w0, a0, k_k, k_a, r_k, ln_w, ln_b = (vec_ref[n:n + 1, :] for n in range(7))
    bd = bd_ref[...]

    def hsum(t):
        hi, lo = _split2(t)
        return _dot(hi, bd) + _dot(lo, bd)

    logw = -DECAY_SCALE * jax.nn.sigmoid(w0 + _mm3(jnp.tanh(wa), wup_ref[...]))
    a = jax.nn.sigmoid(a0 + _mm1(wa, aup_ref[...]))
    gate = _mm1(jax.nn.sigmoid(gd), gup_ref[...])
    kk = k * k_k
    kk = kk / jnp.maximum(jnp.sqrt(hsum(kk * kk)), 1e-12)
    k = k * (1.0 + (a - 1.0) * k_a)
    bonus = hsum(r * k * r_k) * v
    kka = kk * a

    ri = lax.broadcasted_iota(jnp.int32, (tb, tb), 0)
    cj = lax.broadcasted_iota(jnp.int32, (tb, tb), 1)
    same = (ri // C) == (cj // C)
    strict = same & (cj < ri)
    incl = same & (cj <= ri)
    eye = jnp.where(ri == cj, 1.0, 0.0)
    eye_c = eye[0:C, 0:C]
    hi, mid, lo = _split3(logw)
    tri = tri_ref[0]
    blk = tri_ref[1]
    cum = _dot(tri, hi) + _dot(tri, mid) + _dot(tri, lo)
    tot = _dot(blk, hi) + _dot(blk, mid) + _dot(blk, lo)
    e_neg = jnp.exp(-cum)
    e_end = jnp.exp(tot - cum)
    d_end = jnp.exp(tot)
    a_t = -kk * jnp.exp(cum - logw)
    r_t = r * jnp.exp(cum)
    b_t = kka * e_neg
    k_t = k * e_neg
    b_e = kka * e_end
    k_e = k * e_end
    ys = []
    for h in range(RWKV_HEADS):
        hs = slice(h * HEAD_DIM, (h + 1) * HEAD_DIM)
        ah, rh, vh = a_t[:, hs], r_t[:, hs], v[:, hs]
        ar = jnp.concatenate([ah, rh], axis=0)
        xb = _mm1(ar, b_t[:, hs], _dot_nt)
        xk = _mm1(ar, k_t[:, hs], _dot_nt)
        n = jnp.where(strict, xb[0:tb], 0.0)
        m_ak = jnp.where(strict, xk[0:tb], 0.0)
        m_rb = jnp.where(incl, xb[tb:], 0.0)
        m_rk = jnp.where(incl, xk[tb:], 0.0)
        tinv = eye + n
        npow = n
        step = 1
        while 2 * step < C:
            npow = _mm1(npow, npow)
            tinv = tinv + _mm1(tinv, npow)
            step *= 2
        g = _mm1(tinv, ah)
        u0 = _mm1(tinv, _mm1(m_ak, vh))
        r_y = rh + _mm1(m_rb, g)
        y0 = _mm1(m_rb, u0) + _mm1(m_rk, vh)
        beh, keh = b_e[:, hs], k_e[:, hs]
        st = st_scr[h]
        yh = []
        for c in range(tb // C):
            cs = slice(c * C, (c + 1) * C)
            p_st = eye_c * d_end[c * C:c * C + 1, hs] + _mm1(beh[cs], g[cs], _dot_tn)
            q_st = _mm1(beh[cs], u0[cs], _dot_tn) + _mm1(keh[cs], vh[cs], _dot_tn)
            yh.append(_mm3(r_y[cs], st) + y0[cs])
            st = _mm3(p_st, st) + q_st
        st_scr[h] = st
        y = jnp.concatenate(yh, axis=0)
        mean = jnp.mean(y, axis=-1, keepdims=True)
        yc = y - mean
        var = jnp.mean(yc * yc, axis=-1, keepdims=True)
        ys.append(yc * lax.rsqrt(var + GN_EPS))
    yn = jnp.concatenate(ys, axis=1)
    o_ref[0] = (yn * ln_w + ln_b + bonus) * gate


def _rwkv(r, mu, wup, aup, gup, vecs, bd, tri):
    B, S, _ = r.shape
    tb = RWKV_TB
    full = lambda a: pl.BlockSpec(a.shape, lambda b, i: (0,) * a.ndim)
    return pl.pallas_call(
        _rwkv_kernel,
        grid=(B, S // tb),
        in_specs=[pl.BlockSpec((1, tb, RWKV_COLS), lambda b, i: (b, i, 0)),
                  full(mu), full(wup), full(aup), full(gup), full(vecs), full(bd), full(tri)],
        out_specs=pl.BlockSpec((1, tb, RWKV_WIDTH), lambda b, i: (b, i, 0)),
        out_shape=jax.ShapeDtypeStruct((B, S, RWKV_WIDTH), F32),
        scratch_shapes=[pltpu.VMEM((8, RWKV_COLS), F32), pltpu.VMEM((RWKV_HEADS, HEAD_DIM, HEAD_DIM), F32)],
        compiler_params=_cparams(("arbitrary", "arbitrary")),
        name="rwkv7",
    )(r, mu, wup, aup, gup, vecs, bd, tri)


def _rwkv_constants():
    head = jnp.arange(RWKV_WIDTH) // HEAD_DIM
    bd = (head[:, None] == head[None, :]).astype(BF16)
    t = jnp.arange(RWKV_TB)
    same = (t[None, :] // RWKV_CHUNK) == (t[:, None] // RWKV_CHUNK)
    tri = jnp.stack([same & (t[None, :] <= t[:, None]), same]).astype(BF16)
    return bd, tri


def _outproj_kernel(mn_ref, mm_ref, rw_ref, x_ref, g1_ref, gn_ref, gm_ref, wn_ref, wm_ref, wr_ref, o_ref):
    def head_norm(o, gain_ref, width):
        o = jnp.where(gain_ref[1:2, :] > 0.5, o, 0.0)
        ms = jnp.sum(o * o, axis=-1, keepdims=True) * (1.0 / width)
        return (o * lax.rsqrt(ms + NORM_EPS) * gain_ref[0:1, :]).astype(BF16)

    z = _dot(head_norm(mn_ref[0], gn_ref, NSA_WIDTH), wn_ref[...])
    z = z + _dot(head_norm(mm_ref[0], gm_ref, MOBA_WIDTH), wm_ref[...])
    z = z + _dot(rw_ref[0].astype(BF16), wr_ref[...])
    o_ref[0] = x_ref[0] + g1_ref[0] * z


def _outproj(mixn, mixm, orw, x, g1, gn, gm, wn, wm, wr):
    B, S, D = x.shape
    tm = 512
    full = lambda a: pl.BlockSpec(a.shape, lambda b, i: (0,) * a.ndim)
    tok = lambda a: pl.BlockSpec((1, tm, a.shape[2]), lambda b, i: (b, i, 0))
    return pl.pallas_call(
        _outproj_kernel,
        grid=(B, S // tm),
        in_specs=[tok(mixn), tok(mixm), tok(orw), tok(x), pl.BlockSpec((1, 1, D), lambda b, i: (b, 0, 0)),
                  full(gn), full(gm), full(wn), full(wm), full(wr)],
        out_specs=tok(x),
        out_shape=jax.ShapeDtypeStruct((B, S, D), F32),
        compiler_params=_cparams(("arbitrary", "arbitrary")),
        name="outproj",
    )(mixn, mixm, orw, x, g1, gn, gm, wn, wm, wr)


def _pad_moba(t):
    t = t.reshape((MOBA_HEADS, HEAD_DIM) + t.shape[1:])
    return jnp.concatenate([jnp.zeros_like(t), t], axis=1).reshape((MOBA_HEADS * LANES,) + t.shape[2:])


def _pad_nsa(t):
    w = NSA_GROUP * HEAD_DIM
    t = t.reshape((NSA_KV_HEADS, w) + t.shape[1:])
    pad = jnp.zeros((NSA_KV_HEADS, NSA_OUT_LANES - w) + t.shape[2:], t.dtype)
    return jnp.concatenate([t, pad], axis=1).reshape((NSA_KV_HEADS * NSA_OUT_LANES,) + t.shape[2:])


def _gain_and_mask(gain, pad):
    return jnp.stack([pad(gain), pad(jnp.ones_like(gain))])


def _route(logit_t, bias_col):
    aff = jax.nn.sigmoid(logit_t)
    biased = aff + bias_col
    row = lambda t, e: t[e:e + 1, :]
    gp = EXPERTS_PER_GROUP
    scores = []
    for g in range(N_EXPERT_GROUPS):
        a_, b_, c_, d_ = (row(biased, g * gp + j) for j in range(gp))
        hi1, lo1, hi2, lo2 = jnp.maximum(a_, b_), jnp.minimum(a_, b_), jnp.maximum(c_, d_), jnp.minimum(c_, d_)
        scores.append(jnp.maximum(hi1, hi2) + jnp.maximum(jnp.minimum(hi1, hi2), jnp.maximum(lo1, lo2)))
    best = jnp.zeros_like(scores[0], dtype=jnp.int32)
    best_s = scores[0]
    for g in range(1, N_EXPERT_GROUPS):
        better = scores[g] > best_s
        best = jnp.where(better, g, best)
        best_s = jnp.where(better, scores[g], best_s)

    def in_group(t, j):
        out = row(t, j)
        for g in range(1, N_EXPERT_GROUPS):
            out = jnp.where(best == g, row(t, g * gp + j), out)
        return out

    vals = [in_group(biased, j) for j in range(gp)]
    affs = [in_group(aff, j) for j in range(gp)]

    def first_argmax(vs):
        top = functools.reduce(jnp.maximum, vs)
        idx = jnp.full(top.shape, gp, jnp.int32)
        for j in reversed(range(gp)):
            idx = jnp.where(vs[j] == top, j, idx)
        return idx

    i1 = first_argmax(vals)
    i2 = first_argmax([jnp.where(i1 == j, -jnp.inf, vals[j]) for j in range(gp)])
    pick = lambda idx: functools.reduce(lambda acc, j: jnp.where(idx == j, affs[j], acc), range(gp), jnp.zeros_like(affs[0]))
    w1, w2 = pick(i1), pick(i2)
    tot = w1 + w2
    e_iota = lax.broadcasted_iota(jnp.int32, logit_t.shape, 0)
    e1 = best * gp + i1
    e2 = best * gp + i2
    return jnp.where(e_iota == e1, w1 / tot, 0.0) + jnp.where(e_iota == e2, w2 / tot, 0.0)


def _moe_kernel(x_ref, gain_ref, sc_ref, sh_ref, g2_ref, wrt_ref, rb_ref, wg_ref, wu_ref, wd_ref, fin_ref,
                o_ref, hb_scr, cb_scr, acc_scr, *, final):
    e = pl.program_id(1)
    tm = x_ref.shape[1]

    @pl.when(e == 0)
    def _():
        x = x_ref[0]
        ms = jnp.mean(x * x, axis=-1, keepdims=True)
        h = x * lax.rsqrt(ms + NORM_EPS) * gain_ref[...]
        h = h * (1.0 + sc_ref[0]) + sh_ref[0]
        hb_scr[...] = h.astype(BF16)
        comb = _route(_mm3(wrt_ref[...], h, _dot_nt), rb_ref[...])
        comb_t = jnp.concatenate([comb, jnp.zeros((LANES - N_EXPERTS, tm), F32)], axis=0).T
        for ee in range(N_EXPERTS):
            cb_scr[ee] = jnp.broadcast_to(comb_t[:, ee:ee + 1], (tm, LANES))
        acc_scr[...] = jnp.zeros(acc_scr.shape, F32)

    hb = hb_scr[...]
    hg = _dot(hb, wg_ref[0])
    hu = _dot(hb, wu_ref[0])
    cbe = cb_scr[e]
    he = hg * jax.nn.sigmoid(hg) * hu * jnp.concatenate([cbe] * (EXPERT_FF // LANES), axis=1)
    acc_scr[...] += _dot(he.astype(BF16), wd_ref[0])

    @pl.when(e == N_EXPERTS - 1)
    def _():
        out = x_ref[0] + g2_ref[0] * acc_scr[...]
        if final:
            ms = jnp.mean(out * out, axis=-1, keepdims=True)
            out = out * lax.rsqrt(ms + NORM_EPS) * fin_ref[...]
        o_ref[0] = out


def _moe(x, gain, sc, sh, g2, wrt, rb, wg, wu, wd, fin, final):
    B, S, D = x.shape
    tm = 512
    tpb = S // tm
    tok = pl.BlockSpec((1, tm, D), lambda i, e: (i // tpb, i % tpb, 0))
    per_b = pl.BlockSpec((1, 1, D), lambda i, e: (i // tpb, 0, 0))
    full = lambda a: pl.BlockSpec(a.shape, lambda i, e: (0,) * a.ndim)
    return pl.pallas_call(
        functools.partial(_moe_kernel, final=final),
        grid=(B * tpb, N_EXPERTS),
        in_specs=[tok, full(gain), per_b, per_b, per_b, full(wrt), full(rb),
                  pl.BlockSpec((1, D, EXPERT_FF), lambda i, e: (e, 0, 0)),
                  pl.BlockSpec((1, D, EXPERT_FF), lambda i, e: (e, 0, 0)),
                  pl.BlockSpec((1, EXPERT_FF, D), lambda i, e: (e, 0, 0)),
                  full(fin)],
        out_specs=tok,
        out_shape=jax.ShapeDtypeStruct((B, S, D), F32),
        scratch_shapes=[pltpu.VMEM((tm, D), BF16), pltpu.VMEM((N_EXPERTS, tm, LANES), F32), pltpu.VMEM((tm, D), F32)],
        compiler_params=_cparams(("arbitrary", "arbitrary")),
        name="moe",
    )(x, gain, sc, sh, g2, wrt, rb, wg, wu, wd, fin)


def kernel(x, c, positions, w_mod, b_mod, norm_mix, norm_ffn, w_in, nsa_phi_w1, nsa_phi_w2, nsa_phi_pos, rwkv_mu, rwkv_w_up, rwkv_w0, rwkv_a_up, rwkv_a0, rwkv_g_up, rwkv_k_k, rwkv_k_a, rwkv_r_k, rwkv_ln_w, rwkv_ln_b, norm_nsa_out, norm_moba_out, w_out, w_router, router_bias, moe_w_gate, moe_w_up, moe_w_down, norm_final):
    B, S, D = x.shape
    depth = w_in.shape[0]
    assert S % NSA_TK == 0 and S % MOBA_BLOCK == 0 and S % RWKV_TB == 0 and S >= WINDOW + NSA_QB
    assert S // SLC_BLOCK <= LANES and S // MOBA_BLOCK <= MOBA_NBP

    inv = ROPE_THETA ** (-jnp.arange(0, HEAD_DIM, 2, dtype=F32) / HEAD_DIM)
    ang = positions.astype(F32)[..., None] * inv
    cos, sin = jnp.cos(ang), jnp.sin(ang)
    one, zero = jnp.ones((B, S, HEAD_DIM), F32), jnp.zeros((B, S, HEAD_DIM), F32)
    cos_t = jnp.concatenate([cos, cos, one], axis=-1)
    sin_s = jnp.concatenate([-sin, sin, zero], axis=-1)
    nc = S // CMP_STRIDE
    n_cmp = (S - CMP_BLOCK) // CMP_STRIDE + 1
    pad_c = lambda t, fill: jnp.concatenate([t[:, CMP_BLOCK - 1::CMP_STRIDE][:, :n_cmp],
                                             jnp.full((B, nc - n_cmp, LANES), fill, F32)], axis=1)
    cos_c, sin_c = pad_c(cos_t, 1.0), pad_c(sin_s, 0.0)

    cmp_to_slc_t, e3_nsa = _nsa_constants(S)
    bd, tri = _rwkv_constants()
    mod = _modulation(c, w_mod, b_mod)
    wrt = w_router.T
    rb = router_bias.reshape(N_EXPERTS, 1)
    fin = norm_final.reshape(1, D)

    for l in range(depth):
        sh1, sc1, g1, sh2, sc2, g2 = (mod[l, :B, i * D:(i + 1) * D].reshape(B, 1, D) for i in range(6))
        p, r, km, vs_t, vw_t, mv_t = _inproj(x, norm_mix[l].reshape(1, D), sc1, sh1, cos_t, sin_s, _prep_w_in(w_in[l]))

        w1, w2, pos4 = _prep_compress(nsa_phi_w1[l], nsa_phi_w2[l], nsa_phi_pos[l])
        chunks = lambda g: p[:, :, g * LANES:(g + 1) * LANES].reshape(B, nc, CMP_STRIDE * LANES)
        kvcmp, kvcmp_t = _compress(chunks(G_KC), chunks(G_VC), pos4, w1, w2, cos_c, sin_c)
        mixn = _nsa(p, r, kvcmp, kvcmp_t, vs_t, vw_t, cmp_to_slc_t, e3_nsa)

        kmean = km[:, :, 0, :]
        kmean = jnp.concatenate([kmean, jnp.zeros((B, MOBA_NBP - kmean.shape[1], kmean.shape[2]), F32)], axis=1)
        mixm = _moba(p, mv_t, kmean)

        zl = jnp.zeros((DECAY_LORA, RWKV_WIDTH), F32)
        wup = jnp.concatenate([rwkv_w_up[l], zl], axis=0)
        aup = jnp.concatenate([zl, rwkv_a_up[l]], axis=0)
        vecs = jnp.stack([rwkv_w0[l], rwkv_a0[l], rwkv_k_k[l], rwkv_k_a[l], rwkv_r_k[l].reshape(-1),
                          rwkv_ln_w[l], rwkv_ln_b[l], jnp.zeros((RWKV_WIDTH,), F32)])
        orw = _rwkv(r, rwkv_mu[l].reshape(1, -1), wup, aup, rwkv_g_up[l], vecs, bd, tri)

        wo = w_out[l]
        x = _outproj(mixn, mixm, orw, x, g1,
                     _gain_and_mask(norm_nsa_out[l], _pad_nsa),
                     _gain_and_mask(norm_moba_out[l], _pad_moba),
                     _pad_nsa(wo[:NSA_WIDTH]).astype(BF16),
                     _pad_moba(wo[NSA_WIDTH:NSA_WIDTH + MOBA_WIDTH]).astype(BF16),
                     wo[NSA_WIDTH + MOBA_WIDTH:].astype(BF16))
        x = _moe(x, norm_ffn[l].reshape(1, D), sc2, sh2, g2, wrt, rb,
                 moe_w_gate[l].astype(BF16), moe_w_up[l].astype(BF16), moe_w_down[l].astype(BF16),
                 fin, final=(l == depth - 1))
    return x
```

```python
import functools

import jax
import jax.numpy as jnp
from jax import lax
from jax.experimental import pallas as pl
from jax.experimental.pallas import tpu as pltpu

F32 = jnp.float32
BF16 = jnp.bfloat16

HEAD_DIM = 64
LANES = 128
ROPE_THETA = 10000.0
NORM_EPS = 1e-6
NEG = -1e30

NSA_HEADS = 6
NSA_KV_HEADS = 2
NSA_GROUP = NSA_HEADS // NSA_KV_HEADS
NSA_WIDTH = NSA_HEADS * HEAD_DIM
NSA_KV_WIDTH = NSA_KV_HEADS * HEAD_DIM
CMP_BLOCK = 32
CMP_STRIDE = 16
SLC_BLOCK = 64
SLC_TOPN = 16
WINDOW = 512
NSA_QB = 128
NSA_TK = 2 * NSA_QB
NSA_OUT_LANES = 256
NSA_WIN_SPAN = WINDOW + NSA_QB
ATTEND_UNROLL = 4

MOBA_HEADS = 4
MOBA_WIDTH = MOBA_HEADS * HEAD_DIM
MOBA_BLOCK = 256
MOBA_TOPK = 3
MOBA_NBP = 128

RWKV_HEADS = 6
RWKV_WIDTH = RWKV_HEADS * HEAD_DIM
DECAY_LORA = 64
AAA_LORA = 64
GATE_LORA = 128
RWKV_COLS = 3 * RWKV_WIDTH + DECAY_LORA + AAA_LORA + GATE_LORA
DECAY_SCALE = 0.606531
GN_EPS = 64e-5
RWKV_CHUNK = 64
RWKV_TB = 256

N_EXPERTS = 16
N_EXPERT_GROUPS = 4
EXPERTS_PER_GROUP = N_EXPERTS // N_EXPERT_GROUPS
EXPERT_FF = 256
MOE_EXPERTS_PER_STEP = 4

G_NSA_Q = 0
G_SLC = 6
G_WIN = 8
G_KC = 10
G_VC = 11
G_MOBA_Q = 12
G_MOBA_KV = 16
N_GROUPS = 20
ROPED_GROUPS = tuple(range(0, 10)) + tuple(range(12, 20))
P_COLS = N_GROUPS * LANES
R_GATE_BLOCK = RWKV_COLS // LANES
R_COLS = RWKV_COLS + NSA_KV_HEADS * LANES

VMEM_LIMIT = 56 * 1024 * 1024


def _cparams(sem):
    return pltpu.CompilerParams(dimension_semantics=sem, vmem_limit_bytes=VMEM_LIMIT)


def _dot(a, b):
    return jnp.dot(a, b, preferred_element_type=F32)


def _dot_nt(a, b):
    return lax.dot_general(a, b, (((1,), (1,)), ((), ())), preferred_element_type=F32)


def _dot_tn(a, b):
    return lax.dot_general(a, b, (((0,), (0,)), ((), ())), preferred_element_type=F32)


def _split2(x):
    hi = x.astype(BF16)
    lo = (x - hi.astype(F32)).astype(BF16)
    return hi, lo


def _split3(x):
    hi = x.astype(BF16)
    r1 = x - hi.astype(F32)
    mid = r1.astype(BF16)
    lo = (r1 - mid.astype(F32)).astype(BF16)
    return hi, mid, lo


def _mm3(a, b, dot=_dot):
    ah, al = _split2(a)
    bh, bl = _split2(b)
    return dot(ah, bh) + dot(ah, bl) + dot(al, bh)


def _mm1(a, b, dot=_dot):
    return dot(a.astype(BF16), b.astype(BF16))


def _mm3_exact_rhs(a, b_bf16):
    hi, mid, lo = _split3(a)
    return _dot(hi, b_bf16) + _dot(mid, b_bf16) + _dot(lo, b_bf16)


def _rope(y, cos_t, sin_s):
    lane = lax.broadcasted_iota(jnp.int32, y.shape, 1)
    rot = jnp.where(lane < HEAD_DIM // 2, pltpu.roll(y, LANES - HEAD_DIM // 2, 1), pltpu.roll(y, HEAD_DIM // 2, 1))
    return y * cos_t + rot * sin_s


def _mod_kernel(c_ref, w_ref, b_ref, o_ref):
    c = c_ref[...]
    ca = c * jax.nn.sigmoid(c)
    o_ref[0] = _dot(ca.astype(BF16), w_ref[0].astype(BF16)) + b_ref[0]


def _modulation(c, w_mod, b_mod):
    B, D = c.shape
    L, _, N = w_mod.shape
    tn = 512
    c8 = jnp.zeros((8, D), F32).at[:B].set(c)
    return pl.pallas_call(
        _mod_kernel,
        grid=(L, N // tn),
        in_specs=[
            pl.BlockSpec((8, D), lambda l, j: (0, 0)),
            pl.BlockSpec((1, D, tn), lambda l, j: (l, 0, j)),
            pl.BlockSpec((1, 1, tn), lambda l, j: (l, 0, j)),
        ],
        out_specs=pl.BlockSpec((1, 8, tn), lambda l, j: (l, 0, j)),
        out_shape=jax.ShapeDtypeStruct((L, 8, N), F32),
        compiler_params=_cparams(("arbitrary", "arbitrary")),
        name="modulation",
    )(c8, w_mod, b_mod.reshape(L, 1, N))


def _ones_over_values_t(y):
    yt = y.T
    row = lax.broadcasted_iota(jnp.int32, yt.shape, 0)
    return jnp.where(row < HEAD_DIM, 1.0, yt).astype(BF16)


def _inproj_kernel(x_ref, gain_ref, sc_ref, sh_ref, cos_ref, sin_ref, w_ref, p_ref, r_ref, km_ref,
                   vs_ref, vw_ref, mv_ref):
    x = x_ref[0]
    ms = jnp.mean(x * x, axis=-1, keepdims=True)
    h = x * lax.rsqrt(ms + NORM_EPS) * gain_ref[...]
    h = h * (1.0 + sc_ref[0]) + sh_ref[0]
    hb = h.astype(BF16)
    cos_t = cos_ref[0]
    sin_s = sin_ref[0]
    per = 4
    for g0 in range(0, N_GROUPS, per):
        y4 = _dot(hb, w_ref[:, g0 * LANES:(g0 + per) * LANES])
        for j in range(per):
            g = g0 + j
            y = y4[:, j * LANES:(j + 1) * LANES]
            if g in ROPED_GROUPS:
                y = _rope(y, cos_t, sin_s)
            if g >= G_MOBA_KV:
                km = jnp.mean(y, axis=0, keepdims=True)
                km_ref[0, 0, :, (g - G_MOBA_KV) * LANES:(g - G_MOBA_KV + 1) * LANES] = jnp.broadcast_to(km, (8, LANES))
                mv_ref[0, g - G_MOBA_KV, 0] = _ones_over_values_t(y)
            elif G_SLC <= g < G_WIN:
                vs_ref[0, g - G_SLC, 0] = _ones_over_values_t(y)
            elif G_WIN <= g < G_KC:
                yt = _ones_over_values_t(y)
                for t in range(y.shape[0] // NSA_QB):
                    vw_ref[0, g - G_WIN, t] = yt[:, t * NSA_QB:(t + 1) * NSA_QB]
            p_ref[0, :, g * LANES:(g + 1) * LANES] = y.astype(BF16)
    r_ref[0] = _dot(hb, w_ref[:, P_COLS:])


def _inproj(x, gain, sc, sh, cos_t, sin_s, w):
    B, S, D = x.shape
    tm = MOBA_BLOCK
    return pl.pallas_call(
        _inproj_kernel,
        grid=(B, S // tm),
        in_specs=[
            pl.BlockSpec((1, tm, D), lambda b, i: (b, i, 0)),
            pl.BlockSpec((1, D), lambda b, i: (0, 0)),
            pl.BlockSpec((1, 1, D), lambda b, i: (b, 0, 0)),
            pl.BlockSpec((1, 1, D), lambda b, i: (b, 0, 0)),
            pl.BlockSpec((1, tm, LANES), lambda b, i: (b, i, 0)),
            pl.BlockSpec((1, tm, LANES), lambda b, i: (b, i, 0)),
            pl.BlockSpec((D, P_COLS + R_COLS), lambda b, i: (0, 0)),
        ],
        out_specs=[
            pl.BlockSpec((1, tm, P_COLS), lambda b, i: (b, i, 0)),
            pl.BlockSpec((1, tm, R_COLS), lambda b, i: (b, i, 0)),
            pl.BlockSpec((1, 1, 8, MOBA_HEADS * LANES), lambda b, i: (b, i, 0, 0)),
            pl.BlockSpec((1, NSA_KV_HEADS, 1, LANES, tm), lambda b, i: (b, 0, i, 0, 0)),
            pl.BlockSpec((1, NSA_KV_HEADS, tm // NSA_QB, LANES, NSA_QB), lambda b, i: (b, 0, i, 0, 0)),
            pl.BlockSpec((1, MOBA_HEADS, 1, LANES, tm), lambda b, i: (b, 0, i, 0, 0)),
        ],
        out_shape=[
            jax.ShapeDtypeStruct((B, S, P_COLS), BF16),
            jax.ShapeDtypeStruct((B, S, R_COLS), F32),
            jax.ShapeDtypeStruct((B, S // tm, 8, MOBA_HEADS * LANES), F32),
            jax.ShapeDtypeStruct((B, NSA_KV_HEADS, S // tm, LANES, tm), BF16),
            jax.ShapeDtypeStruct((B, NSA_KV_HEADS, S // NSA_QB, LANES, NSA_QB), BF16),
            jax.ShapeDtypeStruct((B, MOBA_HEADS, S // tm, LANES, tm), BF16),
        ],
        compiler_params=_cparams(("arbitrary", "arbitrary")),
        name="inproj",
    )(x, gain, sc, sh, cos_t, sin_s, w)


def _prep_w_in(w):
    D = w.shape[0]
    o = 0
    parts = {}
    for name, width in (("nq", NSA_WIDTH), ("nkc", NSA_KV_WIDTH), ("nvc", NSA_KV_WIDTH), ("nks", NSA_KV_WIDTH),
                        ("nvs", NSA_KV_WIDTH), ("nkw", NSA_KV_WIDTH), ("nvw", NSA_KV_WIDTH), ("ngate", NSA_HEADS * 3),
                        ("mq", MOBA_WIDTH), ("mk", MOBA_WIDTH), ("mv", MOBA_WIDTH), ("rf", RWKV_COLS)):
        parts[name] = w[:, o:o + width]
        o += width
    hd = lambda t, h: t[:, h * HEAD_DIM:(h + 1) * HEAD_DIM]
    z = jnp.zeros((D, HEAD_DIM), F32)
    scale = HEAD_DIM ** -0.5
    cols = []
    for h in range(NSA_HEADS):
        cols += [hd(parts["nq"], h) * scale, z]
    for h in range(NSA_KV_HEADS):
        cols += [hd(parts["nks"], h), hd(parts["nvs"], h)]
    for h in range(NSA_KV_HEADS):
        cols += [hd(parts["nkw"], h), hd(parts["nvw"], h)]
    cols += [parts["nkc"], parts["nvc"]]
    for h in range(MOBA_HEADS):
        cols += [hd(parts["mq"], h) * scale, z]
    for h in range(MOBA_HEADS):
        cols += [hd(parts["mk"], h), hd(parts["mv"], h)]
    cols += [parts["rf"]]
    per = NSA_GROUP * 3
    for h in range(NSA_KV_HEADS):
        cols += [parts["ngate"][:, h * per:(h + 1) * per], jnp.zeros((D, LANES - per), F32)]
    return jnp.concatenate(cols, axis=1).astype(BF16)


def _cmp_kernel(kc_ref, vc_ref, pos_ref, w1_ref, w2_ref, cos_ref, sin_ref, o_ref, ot_ref):
    nc = kc_ref.shape[1]

    def hidden(t_ref, ia, ib):
        t = t_ref[0].astype(F32)
        a = _dot((t + pos_ref[ia:ia + 1, :]).astype(BF16), w1_ref[ia])
        b = _dot((t + pos_ref[ib:ib + 1, :]).astype(BF16), w1_ref[ib])
        return jax.nn.gelu(a + pltpu.roll(b, nc - 1, 0)).astype(BF16)

    gk = hidden(kc_ref, 0, 1)
    gv = hidden(vc_ref, 2, 3)
    for h in range(NSA_KV_HEADS):
        y = _dot(gk, w2_ref[2 * h]) + _dot(gv, w2_ref[2 * h + 1])
        y = _rope(y, cos_ref[0], sin_ref[0])
        o_ref[0, h] = y.astype(BF16)
        ot_ref[0, h] = _ones_over_values_t(y)


def _compress(kc16, vc16, pos4, w1, w2, cos_c, sin_c):
    B, nc, K = kc16.shape
    return pl.pallas_call(
        _cmp_kernel,
        grid=(B,),
        in_specs=[
            pl.BlockSpec((1, nc, K), lambda b: (b, 0, 0)),
            pl.BlockSpec((1, nc, K), lambda b: (b, 0, 0)),
            pl.BlockSpec((4, K), lambda b: (0, 0)),
            pl.BlockSpec((4, K, LANES), lambda b: (0, 0, 0)),
            pl.BlockSpec((4, LANES, LANES), lambda b: (0, 0, 0)),
            pl.BlockSpec((1, nc, LANES), lambda b: (b, 0, 0)),
            pl.BlockSpec((1, nc, LANES), lambda b: (b, 0, 0)),
        ],
        out_specs=[pl.BlockSpec((1, NSA_KV_HEADS, nc, LANES), lambda b: (b, 0, 0, 0)),
                   pl.BlockSpec((1, NSA_KV_HEADS, LANES, nc), lambda b: (b, 0, 0, 0))],
        out_shape=[jax.ShapeDtypeStruct((B, NSA_KV_HEADS, nc, LANES), BF16),
                   jax.ShapeDtypeStruct((B, NSA_KV_HEADS, LANES, nc), BF16)],
        compiler_params=_cparams(("arbitrary",)),
        name="nsa_compress",
    )(kc16, vc16, pos4, w1, w2, cos_c, sin_c)


def _prep_compress(phi_w1, phi_w2, phi_pos):
    half = CMP_BLOCK // 2
    eye = jnp.eye(NSA_KV_HEADS, dtype=F32)
    w1, pos = [], []
    for t in range(2):
        for part in range(2):
            w = phi_w1[t, part * half:(part + 1) * half]
            w1.append(jnp.einsum("lde,kK->lkdKe", w, eye).reshape(half * NSA_KV_WIDTH, NSA_KV_WIDTH))
            p = phi_pos[t, part * half:(part + 1) * half]
            pos.append(jnp.broadcast_to(p[:, None, :], (half, NSA_KV_HEADS, HEAD_DIM)).reshape(-1))
    w2 = []
    for h in range(NSA_KV_HEADS):
        for t in range(2):
            m = jnp.zeros((LANES, LANES), F32)
            m = m.at[h * HEAD_DIM:(h + 1) * HEAD_DIM, t * HEAD_DIM:(t + 1) * HEAD_DIM].set(phi_w2[t])
            w2.append(m)
    return jnp.stack(w1).astype(BF16), jnp.stack(w2).astype(BF16), jnp.stack(pos)


def _attend_tiles(n_loop, scores, mask_last, values_t, s_a, s_b, p_scr, alpha_scr, m_scr, acc_scr):
    m_scr[...] = jnp.full(m_scr.shape, -jnp.inf, F32)
    alpha_scr[...] = jnp.ones(alpha_scr.shape, F32)
    acc_scr[...] = jnp.zeros(acc_scr.shape, F32)
    p_scr[...] = jnp.zeros(p_scr.shape, BF16)

    def apply_weights(j):
        acc_scr[...] = alpha_scr[0:1, :] * acc_scr[...] + _dot(values_t(j), p_scr[...])

    def softmax_tile(s_t):
        m_prev = m_scr[0:1, :]
        m_new = jnp.maximum(m_prev, jnp.max(s_t, axis=0, keepdims=True))
        alpha_scr[0:1, :] = jnp.exp(m_prev - m_new)
        p_scr[...] = jnp.exp(s_t - m_new).astype(BF16)
        m_scr[0:1, :] = m_new

    def step(j, cur, nxt):
        apply_weights(jnp.maximum(j - 1, 0))
        softmax_tile(cur[...])
        nxt[...] = scores(j + 1)

    s_a[...] = scores(0)

    def quad(i, carry):
        for u in range(ATTEND_UNROLL):
            step(ATTEND_UNROLL * i + u, (s_a, s_b)[u % 2], (s_b, s_a)[u % 2])
        return carry

    lax.fori_loop(0, n_loop // ATTEND_UNROLL, quad, 0)
    done = n_loop - n_loop % ATTEND_UNROLL

    @pl.when(n_loop % ATTEND_UNROLL >= 2)
    def _():
        step(done, s_a, s_b)
        step(done + 1, s_b, s_a)

    @pl.when(n_loop % 2 == 1)
    def _():
        step(n_loop - 1, s_a, s_a)

    apply_weights(jnp.maximum(n_loop - 1, 0))
    softmax_tile(mask_last(s_a[...]))
    apply_weights(n_loop)
    return acc_scr[...]


def _attend_scratch(tk, nq):
    return [pltpu.VMEM((tk, nq), F32), pltpu.VMEM((tk, nq), F32), pltpu.VMEM((tk, nq), BF16),
            pltpu.VMEM((8, nq), F32), pltpu.VMEM((8, nq), F32), pltpu.VMEM((LANES, nq), F32)]


def _nsa_kernel(q0_ref, q1_ref, q2_ref, kvc_ref, kvct_ref, kvs_ref, vst_ref, kvw_ref, vwt_ref, gate_ref, mt_ref,
                et_ref, rel_ref, crel_ref, o_ref, s_a, s_b, p_scr, alpha_scr, m_scr, acc_scr, *, n_pick):
    ci = pl.program_id(2)
    qb = NSA_QB
    nq = NSA_GROUP * qb
    q = jnp.concatenate([q0_ref[0], q1_ref[0], q2_ref[0]], axis=0)

    kc = kvc_ref[0, 0]
    sm = jnp.where(crel_ref[...] <= ci * qb, _dot_nt(kc, q), NEG)
    mx = jnp.max(sm, axis=0, keepdims=True)
    e = jnp.exp(sm - mx)
    den = jnp.maximum(jnp.sum(e, axis=0, keepdims=True), 1e-30)
    p_c = e * jnp.where(mx > 0.5 * NEG, 1.0 / den, 0.0)
    o_c = _dot(kvct_ref[0, 0], p_c.astype(BF16))
    o_w = _nsa_window(ci, q, kvw_ref, vwt_ref, rel_ref)

    hi, mid, lo = _split3(p_c[:, 0:qb] + p_c[:, qb:2 * qb] + p_c[:, 2 * qb:3 * qb])
    mt = mt_ref[...]
    imp_t = _dot(mt, hi) + _dot(mt, mid) + _dot(mt, lo)
    blk = lax.broadcasted_iota(jnp.int32, imp_t.shape, 0)
    cur = (ci * qb + lax.broadcasted_iota(jnp.int32, imp_t.shape, 1)) // SLC_BLOCK
    forced = (blk == 0) | (blk == cur) | (blk == cur - 1)
    free = (blk <= cur) & jnp.logical_not(forced)
    sc = jnp.where(free, imp_t, -1.0)
    for _ in range(n_pick):
        best = jnp.max(sc, axis=0, keepdims=True)
        idx = jnp.min(jnp.where(sc == best, blk, imp_t.shape[0]), axis=0, keepdims=True)
        sc = jnp.where(blk == idx, -2.0, sc)
    bias = jnp.where(free, jnp.where(sc < -1.5, 0.0, NEG), jnp.where(forced, 0.0, NEG)).T.astype(BF16)
    w_nt = jnp.concatenate([q, jnp.concatenate([bias] * NSA_GROUP, axis=0)], axis=1)

    tk = NSA_TK
    last = (ci * qb) // tk

    def scores(j):
        kv = kvs_ref[0, pl.ds(pl.multiple_of(j * tk, tk), tk), :]
        return _dot_nt(jnp.concatenate([kv, et_ref[j]], axis=1), w_nt)

    def causal_edge(s_t):
        return jnp.where(rel_ref[0:tk, :] <= ci * qb - last * tk, s_t, NEG)

    acc = _attend_tiles(last, scores, causal_edge, lambda j: vst_ref[0, 0, j],
                        s_a, s_b, p_scr, alpha_scr, m_scr, acc_scr)
    o_s = acc[HEAD_DIM:] * (1.0 / acc[0:1])

    gt = jax.nn.sigmoid(gate_ref[0]).T
    outs = []
    for g in range(NSA_GROUP):
        ls = slice(g * qb, (g + 1) * qb)
        outs.append(gt[3 * g:3 * g + 1] * o_c[HEAD_DIM:, ls] + gt[3 * g + 1:3 * g + 2] * o_s[:, ls]
                    + gt[3 * g + 2:3 * g + 3] * o_w[:, ls])
    outs.append(jnp.zeros((NSA_OUT_LANES - NSA_GROUP * HEAD_DIM, qb), F32))
    o_ref[0] = jnp.concatenate(outs, axis=0).T


def _nsa_window(ci, q, kvw_ref, vwt_ref, rel_ref):
    qb = NSA_QB
    n_wt = NSA_WIN_SPAN // qb
    first = jnp.maximum(ci - WINDOW // qb, 0)
    kvw = kvw_ref[0, pl.ds(pl.multiple_of(first * qb, qb), NSA_WIN_SPAN), :]
    s_t = _dot_nt(kvw, q)
    dist = (ci - first) * qb - rel_ref[...]
    sm = jnp.where(lax.bitcast_convert_type(dist, jnp.uint32) < WINDOW, s_t, NEG)
    e_w = jnp.exp(sm - jnp.max(sm, axis=0, keepdims=True))
    vw_t = jnp.concatenate([vwt_ref[0, 0, first + t] for t in range(n_wt)], axis=1)
    acc_w = _dot(vw_t, e_w.astype(BF16))
    return acc_w[HEAD_DIM:] * (1.0 / acc_w[0:1])


def _nsa(p, r, kvcmp, kvcmp_t, vs_t, vw_t, cmp_to_slc_t, e3, rel, crel):
    B, S, _ = p.shape
    nc = kvcmp.shape[2]
    qb = NSA_QB
    n_top = min(SLC_TOPN, S // SLC_BLOCK)
    qspec = lambda g: pl.BlockSpec((1, qb, LANES), lambda b, h, i: (b, i, G_NSA_Q + NSA_GROUP * h + g))
    nq = NSA_GROUP * qb
    return pl.pallas_call(
        functools.partial(_nsa_kernel, n_pick=max(n_top - 3, 0)),
        grid=(B, NSA_KV_HEADS, S // qb),
        in_specs=[
            qspec(0), qspec(1), qspec(2),
            pl.BlockSpec((1, 1, nc, LANES), lambda b, h, i: (b, h, 0, 0)),
            pl.BlockSpec((1, 1, LANES, nc), lambda b, h, i: (b, h, 0, 0)),
            pl.BlockSpec((1, S, LANES), lambda b, h, i: (b, 0, G_SLC + h)),
            pl.BlockSpec((1, 1) + vs_t.shape[2:], lambda b, h, i: (b, h, 0, 0, 0)),
            pl.BlockSpec((1, S, LANES), lambda b, h, i: (b, 0, G_WIN + h)),
            pl.BlockSpec((1, 1) + vw_t.shape[2:], lambda b, h, i: (b, h, 0, 0, 0)),
            pl.BlockSpec((1, qb, LANES), lambda b, h, i: (b, i, R_GATE_BLOCK + h)),
            pl.BlockSpec(cmp_to_slc_t.shape, lambda b, h, i: (0, 0)),
            pl.BlockSpec(e3.shape, lambda b, h, i: (0, 0, 0)),
            pl.BlockSpec(rel.shape, lambda b, h, i: (0, 0)),
            pl.BlockSpec(crel.shape, lambda b, h, i: (0, 0)),
        ],
        out_specs=pl.BlockSpec((1, qb, NSA_OUT_LANES), lambda b, h, i: (b, i, h)),
        out_shape=jax.ShapeDtypeStruct((B, S, NSA_KV_HEADS * NSA_OUT_LANES), F32),
        scratch_shapes=_attend_scratch(NSA_TK, nq),
        compiler_params=_cparams(("arbitrary", "arbitrary", "arbitrary")),
        name="nsa_attention",
    )(p, p, p, kvcmp, kvcmp_t, p, vs_t, p, vw_t, r, cmp_to_slc_t, e3, rel, crel)


def _nsa_constants(S):
    n_cmp = (S - CMP_BLOCK) // CMP_STRIDE + 1
    nc = S // CMP_STRIDE
    n_slc = S // SLC_BLOCK
    c_start = jnp.arange(nc) * CMP_STRIDE
    s_start = jnp.arange(LANES) * SLC_BLOCK
    overlap = (jnp.minimum(c_start[None, :] + CMP_BLOCK, s_start[:, None] + SLC_BLOCK)
               - jnp.maximum(c_start[None, :], s_start[:, None]))
    m_t = jnp.clip(overlap, 0, None).astype(F32) / CMP_BLOCK
    m_t = jnp.where((jnp.arange(nc)[None, :] < n_cmp) & (jnp.arange(LANES)[:, None] < n_slc), m_t, 0.0)
    key_blk = (jnp.arange(S) // SLC_BLOCK).reshape(S // NSA_TK, NSA_TK, 1)
    e3 = (key_blk == jnp.arange(LANES)[None, None, :]).astype(BF16)
    lane_q = jnp.arange(NSA_GROUP * NSA_QB) % NSA_QB
    rel = jnp.arange(NSA_WIN_SPAN)[:, None] - lane_q[None, :]
    crel = (jnp.arange(nc) * CMP_STRIDE + CMP_BLOCK - 1)[:, None] - lane_q[None, :]
    return m_t.astype(BF16), e3, rel.astype(jnp.int32), crel.astype(jnp.int32)


def _moba_kernel(q_ref, kv_ref, vt_ref, km_ref, o_ref, sel_scr, s_a, s_b, p_scr, alpha_scr, m_scr, acc_scr):
    ci = pl.program_id(2)
    qb = MOBA_BLOCK
    q = q_ref[0]
    km = km_ref[0]
    km_hi, km_lo = _split2(km)
    gate_t = _dot_nt(km_hi, q) + _dot_nt(km_lo, q)
    blk = lax.broadcasted_iota(jnp.int32, gate_t.shape, 0)
    valid = blk < ci
    sc = jnp.where(valid, gate_t, -jnp.inf)
    picked = jnp.zeros(gate_t.shape, F32)
    for _ in range(MOBA_TOPK):
        best = jnp.max(sc, axis=0, keepdims=True)
        idx = jnp.min(jnp.where(sc == best, blk, gate_t.shape[0]), axis=0, keepdims=True)
        pick = blk == idx
        picked = jnp.where(pick, 1.0, picked)
        sc = jnp.where(pick, -jnp.inf, sc)
    sel = jnp.where(valid, picked, jnp.where(blk == ci, 1.0, 0.0))
    sel_scr[...] = jnp.where(sel > 0.5, 0.0, NEG)

    def scores(j):
        kvj = kv_ref[0, pl.ds(pl.multiple_of(j * qb, qb), qb), :]
        return _dot_nt(kvj, q) + sel_scr[pl.ds(j, 1), :]

    def causal_edge(s_t):
        causal = lax.broadcasted_iota(jnp.int32, s_t.shape, 0) <= lax.broadcasted_iota(jnp.int32, s_t.shape, 1)
        return jnp.where(causal, s_t, NEG)

    acc = _attend_tiles(ci, scores, causal_edge, lambda j: vt_ref[0, 0, j],
                        s_a, s_b, p_scr, alpha_scr, m_scr, acc_scr)
    o_ref[0] = (acc * (1.0 / acc[0:1])).T


def _moba(p, mv_t, kmean):
    B, S, _ = p.shape
    qb = MOBA_BLOCK
    return pl.pallas_call(
        _moba_kernel,
        grid=(B, MOBA_HEADS, S // qb),
        in_specs=[
            pl.BlockSpec((1, qb, LANES), lambda b, h, i: (b, i, G_MOBA_Q + h)),
            pl.BlockSpec((1, S, LANES), lambda b, h, i: (b, 0, G_MOBA_KV + h)),
            pl.BlockSpec((1, 1) + mv_t.shape[2:], lambda b, h, i: (b, h, 0, 0, 0)),
            pl.BlockSpec((1, MOBA_NBP, LANES), lambda b, h, i: (b, 0, h)),
        ],
        out_specs=pl.BlockSpec((1, qb, LANES), lambda b, h, i: (b, i, h)),
        out_shape=jax.ShapeDtypeStruct((B, S, MOBA_HEADS * LANES), F32),
        scratch_shapes=[pltpu.VMEM((MOBA_NBP, qb), F32)] + _attend_scratch(qb, qb),
        compiler_params=_cparams(("arbitrary", "arbitrary", "arbitrary")),
        name="moba_attention",
    )(p, p, mv_t, kmean)


def _rwkv_kernel(f_ref, mu_ref, wup_ref, aup_ref, gup_ref, vec_ref, bd_ref, tri_ref, o_ref, carry_scr, st_scr):
    i = pl.program_id(1)
    tb = f_ref.shape[1]
    C = RWKV_CHUNK
    W = RWKV_WIDTH

    @pl.when(i == 0)
    def _():
        carry_scr[...] = jnp.zeros(carry_scr.shape, F32)
        st_scr[...] = jnp.zeros(st_scr.shape, F32)

    feat = f_ref[0]
    rowi = lax.broadcasted_iota(jnp.int32, feat.shape, 0)
    prev = jnp.where(rowi == 0, carry_scr[0:1, :], pltpu.roll(feat, 1, 0))
    carry_scr[0:1, :] = feat[tb - 1:tb, :]
    xs = feat + (prev - feat) * mu_ref[...]
    r = xs[:, 0:W]
    k = xs[:, W:2 * W]
    v = xs[:, 2 * W:3 * W]
    wa = xs[:, 3 * W:3 * W + DECAY_LORA + AAA_LORA]
    gd = xs[:, 3 * W + DECAY_LORA + AAA_LORA:]
    w0, a0, k_k, k_a, r_k, ln_w, ln_b = (vec_ref[n:n + 1, :] for n in range(7))
    bd = bd_ref[...]

    def hsum(t):
        hi, lo = _split2(t)
        return _dot(hi, bd) + _dot(lo, bd)

    logw = -DECAY_SCALE * jax.nn.sigmoid(w0 + _mm3(jnp.tanh(wa), wup_ref[...]))
    a = jax.nn.sigmoid(a0 + _mm1(wa, aup_ref[...]))
    gate = _mm1(jax.nn.sigmoid(gd), gup_ref[...])
    kk = k * k_k
    kk = kk / jnp.maximum(jnp.sqrt(hsum(kk * kk)), 1e-12)
    k = k * (1.0 + (a - 1.0) * k_a)
    bonus = hsum(r * k * r_k) * v
    kka = kk * a

    ri = lax.broadcasted_iota(jnp.int32, (tb, tb), 0)
    cj = lax.broadcasted_iota(jnp.int32, (tb, tb), 1)
    same = (ri // C) == (cj // C)
    strict = same & (cj < ri)
    incl = same & (cj <= ri)
    eye = jnp.where(ri == cj, 1.0, 0.0)
    eye_c = eye[0:C, 0:C]
    hi, mid, lo = _split3(logw)
    tri = tri_ref[0]
    blk = tri_ref[1]
    cum = _dot(tri, hi) + _dot(tri, mid) + _dot(tri, lo)
    tot = _dot(blk, hi) + _dot(blk, mid) + _dot(blk, lo)
    e_neg = jnp.exp(-cum)
    e_end = jnp.exp(tot - cum)
    d_end = jnp.exp(tot)
    a_t = -kk * jnp.exp(cum - logw)
    r_t = r * jnp.exp(cum)
    b_t = kka * e_neg
    k_t = k * e_neg
    b_e = kka * e_end
    k_e = k * e_end
    heads = range(RWKV_HEADS)
    hs = [slice(h * HEAD_DIM, (h + 1) * HEAD_DIM) for h in heads]
    bf = lambda t: t.astype(BF16)
    ah = [bf(a_t[:, s]) for s in hs]
    rh = [r_t[:, s] for s in hs]
    vh = [bf(v[:, s]) for s in hs]
    ar = [jnp.concatenate([ah[h], bf(rh[h])], axis=0) for h in heads]
    xb = [_dot_nt(ar[h], bf(b_t[:, hs[h]])) for h in heads]
    xk = [_dot_nt(ar[h], bf(k_t[:, hs[h]])) for h in heads]
    n = [bf(jnp.where(strict, xb[h][0:tb], 0.0)) for h in heads]
    m_ak = [bf(jnp.where(strict, xk[h][0:tb], 0.0)) for h in heads]
    m_rb = [bf(jnp.where(incl, xb[h][tb:], 0.0)) for h in heads]
    m_rk = [bf(jnp.where(incl, xk[h][tb:], 0.0)) for h in heads]
    tinv = [eye + n[h].astype(F32) for h in heads]
    npow = n
    step = 1
    while 2 * step < C:
        npow = [bf(_dot(npow[h], npow[h])) for h in heads]
        tinv = [tinv[h] + _dot(bf(tinv[h]), npow[h]) for h in heads]
        step *= 2
    tinv = [bf(t) for t in tinv]
    g = [_dot(tinv[h], ah[h]) for h in heads]
    u0 = [_dot(tinv[h], bf(_dot(m_ak[h], vh[h]))) for h in heads]
    gb = [bf(t) for t in g]
    ub = [bf(t) for t in u0]
    r_y = [rh[h] + _dot(m_rb[h], gb[h]) for h in heads]
    y0 = [_dot(m_rb[h], ub[h]) + _dot(m_rk[h], vh[h]) for h in heads]
    beh = [bf(b_e[:, s]) for s in hs]
    keh = [bf(k_e[:, s]) for s in hs]
    st = [st_scr[h] for h in heads]
    yh = [[] for _ in heads]
    for c in range(tb // C):
        cs = slice(c * C, (c + 1) * C)
        p_st = [eye_c * d_end[c * C:c * C + 1, hs[h]] + _dot_tn(beh[h][cs], gb[h][cs]) for h in heads]
        q_st = [_dot_tn(beh[h][cs], ub[h][cs]) + _dot_tn(keh[h][cs], vh[h][cs]) for h in heads]
        for h in heads:
            yh[h].append(_mm3(r_y[h][cs], st[h]) + y0[h][cs])
        st = [_mm3(p_st[h], st[h]) + q_st[h] for h in heads]
    ys = []
    for h in heads:
        st_scr[h] = st[h]
        y = jnp.concatenate(yh[h], axis=0)
        mean = jnp.mean(y, axis=-1, keepdims=True)
        yc = y - mean
        var = jnp.mean(yc * yc, axis=-1, keepdims=True)
        ys.append(yc * lax.rsqrt(var + GN_EPS))
    yn = jnp.concatenate(ys, axis=1)
    o_ref[0] = (yn * ln_w + ln_b + bonus) * gate


def _rwkv(r, mu, wup, aup, gup, vecs, bd, tri):
    B, S, _ = r.shape
    tb = RWKV_TB
    full = lambda a: pl.BlockSpec(a.shape, lambda b, i: (0,) * a.ndim)
    return pl.pallas_call(
        _rwkv_kernel,
        grid=(B, S // tb),
        in_specs=[pl.BlockSpec((1, tb, RWKV_COLS), lambda b, i: (b, i, 0)),
                  full(mu), full(wup), full(aup), full(gup), full(vecs), full(bd), full(tri)],
        out_specs=pl.BlockSpec((1, tb, RWKV_WIDTH), lambda b, i: (b, i, 0)),
        out_shape=jax.ShapeDtypeStruct((B, S, RWKV_WIDTH), F32),
        scratch_shapes=[pltpu.VMEM((8, RWKV_COLS), F32), pltpu.VMEM((RWKV_HEADS, HEAD_DIM, HEAD_DIM), F32)],
        compiler_params=_cparams(("arbitrary", "arbitrary")),
        name="rwkv7",
    )(r, mu, wup, aup, gup, vecs, bd, tri)


def _rwkv_constants():
    head = jnp.arange(RWKV_WIDTH) // HEAD_DIM
    bd = (head[:, None] == head[None, :]).astype(BF16)
    t = jnp.arange(RWKV_TB)
    same = (t[None, :] // RWKV_CHUNK) == (t[:, None] // RWKV_CHUNK)
    tri = jnp.stack([same & (t[None, :] <= t[:, None]), same]).astype(BF16)
    return bd, tri


def _outproj_kernel(mn_ref, mm_ref, rw_ref, x_ref, g1_ref, gn_ref, gm_ref, wn_ref, wm_ref, wr_ref, o_ref):
    def head_norm(o, gain_ref, width):
        o = jnp.where(gain_ref[1:2, :] > 0.5, o, 0.0)
        ms = jnp.sum(o * o, axis=-1, keepdims=True) * (1.0 / width)
        return (o * lax.rsqrt(ms + NORM_EPS) * gain_ref[0:1, :]).astype(BF16)

    z = _dot(head_norm(mn_ref[0], gn_ref, NSA_WIDTH), wn_ref[...])
    z = z + _dot(head_norm(mm_ref[0], gm_ref, MOBA_WIDTH), wm_ref[...])
    z = z + _dot(rw_ref[0].astype(BF16), wr_ref[...])
    o_ref[0] = x_ref[0] + g1_ref[0] * z


def _outproj(mixn, mixm, orw, x, g1, gn, gm, wn, wm, wr):
    B, S, D = x.shape
    tm = 512
    full = lambda a: pl.BlockSpec(a.shape, lambda b, i: (0,) * a.ndim)
    tok = lambda a: pl.BlockSpec((1, tm, a.shape[2]), lambda b, i: (b, i, 0))
    return pl.pallas_call(
        _outproj_kernel,
        grid=(B, S // tm),
        in_specs=[tok(mixn), tok(mixm), tok(orw), tok(x), pl.BlockSpec((1, 1, D), lambda b, i: (b, 0, 0)),
                  full(gn), full(gm), full(wn), full(wm), full(wr)],
        out_specs=tok(x),
        out_shape=jax.ShapeDtypeStruct((B, S, D), F32),
        compiler_params=_cparams(("arbitrary", "arbitrary")),
        name="outproj",
    )(mixn, mixm, orw, x, g1, gn, gm, wn, wm, wr)


def _pad_moba(t):
    t = t.reshape((MOBA_HEADS, HEAD_DIM) + t.shape[1:])
    return jnp.concatenate([jnp.zeros_like(t), t], axis=1).reshape((MOBA_HEADS * LANES,) + t.shape[2:])


def _pad_nsa(t):
    w = NSA_GROUP * HEAD_DIM
    t = t.reshape((NSA_KV_HEADS, w) + t.shape[1:])
    pad = jnp.zeros((NSA_KV_HEADS, NSA_OUT_LANES - w) + t.shape[2:], t.dtype)
    return jnp.concatenate([t, pad], axis=1).reshape((NSA_KV_HEADS * NSA_OUT_LANES,) + t.shape[2:])


def _gain_and_mask(gain, pad):
    return jnp.stack([pad(gain), pad(jnp.ones_like(gain))])


def _route(logit_t, bias_col):
    aff = jax.nn.sigmoid(logit_t)
    biased = aff + bias_col
    row = lambda t, e: t[e:e + 1, :]
    gp = EXPERTS_PER_GROUP
    scores = []
    for g in range(N_EXPERT_GROUPS):
        a_, b_, c_, d_ = (row(biased, g * gp + j) for j in range(gp))
        hi1, lo1, hi2, lo2 = jnp.maximum(a_, b_), jnp.minimum(a_, b_), jnp.maximum(c_, d_), jnp.minimum(c_, d_)
        scores.append(jnp.maximum(hi1, hi2) + jnp.maximum(jnp.minimum(hi1, hi2), jnp.maximum(lo1, lo2)))
    best = jnp.zeros_like(scores[0], dtype=jnp.int32)
    best_s = scores[0]
    for g in range(1, N_EXPERT_GROUPS):
        better = scores[g] > best_s
        best = jnp.where(better, g, best)
        best_s = jnp.where(better, scores[g], best_s)

    def in_group(t, j):
        out = row(t, j)
        for g in range(1, N_EXPERT_GROUPS):
            out = jnp.where(best == g, row(t, g * gp + j), out)
        return out

    vals = [in_group(biased, j) for j in range(gp)]
    affs = [in_group(aff, j) for j in range(gp)]

    def first_argmax(vs):
        top = functools.reduce(jnp.maximum, vs)
        idx = jnp.full(top.shape, gp, jnp.int32)
        for j in reversed(range(gp)):
            idx = jnp.where(vs[j] == top, j, idx)
        return idx

    i1 = first_argmax(vals)
    i2 = first_argmax([jnp.where(i1 == j, -jnp.inf, vals[j]) for j in range(gp)])
    pick = lambda idx: functools.reduce(lambda acc, j: jnp.where(idx == j, affs[j], acc), range(gp), jnp.zeros_like(affs[0]))
    w1, w2 = pick(i1), pick(i2)
    tot = w1 + w2
    e_iota = lax.broadcasted_iota(jnp.int32, logit_t.shape, 0)
    e1 = best * gp + i1
    e2 = best * gp + i2
    return jnp.where(e_iota == e1, w1 / tot, 0.0) + jnp.where(e_iota == e2, w2 / tot, 0.0)


def _moe_kernel(x_ref, gain_ref, sc_ref, sh_ref, g2_ref, wrt_ref, rb_ref, wg_ref, wu_ref, wd_ref, fin_ref,
                o_ref, hb_scr, cb_scr, acc_scr, *, final):
    e = pl.program_id(1)
    tm = x_ref.shape[1]

    @pl.when(e == 0)
    def _():
        x = x_ref[0]
        ms = jnp.mean(x * x, axis=-1, keepdims=True)
        h = x * lax.rsqrt(ms + NORM_EPS) * gain_ref[...]
        h = h * (1.0 + sc_ref[0]) + sh_ref[0]
        hb_scr[...] = h.astype(BF16)
        comb = _route(_mm3(wrt_ref[...], h, _dot_nt), rb_ref[...])
        comb_t = jnp.concatenate([comb, jnp.zeros((LANES - N_EXPERTS, tm), F32)], axis=0).T
        for ee in range(N_EXPERTS):
            cb_scr[ee] = jnp.broadcast_to(comb_t[:, ee:ee + 1], (tm, LANES))
        acc_scr[...] = jnp.zeros(acc_scr.shape, F32)

    hb = hb_scr[...]
    hes = []
    for j in range(MOE_EXPERTS_PER_STEP):
        hg = _dot(hb, wg_ref[j])
        hu = _dot(hb, wu_ref[j])
        cbe = cb_scr[e * MOE_EXPERTS_PER_STEP + j]
        he = hg * jax.nn.sigmoid(hg) * hu * jnp.concatenate([cbe] * (EXPERT_FF // LANES), axis=1)
        hes.append(he.astype(BF16))
    acc_scr[...] += _dot(jnp.concatenate(hes, axis=1), wd_ref[...])

    @pl.when(e == N_EXPERTS // MOE_EXPERTS_PER_STEP - 1)
    def _():
        out = x_ref[0] + g2_ref[0] * acc_scr[...]
        if final:
            ms = jnp.mean(out * out, axis=-1, keepdims=True)
            out = out * lax.rsqrt(ms + NORM_EPS) * fin_ref[...]
        o_ref[0] = out


def _moe(x, gain, sc, sh, g2, wrt, rb, wg, wu, wd, fin, final):
    B, S, D = x.shape
    tm = 512
    tpb = S // tm
    eps = MOE_EXPERTS_PER_STEP
    wd = wd.reshape(N_EXPERTS * EXPERT_FF, D)
    tok = pl.BlockSpec((1, tm, D), lambda i, e: (i // tpb, i % tpb, 0))
    per_b = pl.BlockSpec((1, 1, D), lambda i, e: (i // tpb, 0, 0))
    full = lambda a: pl.BlockSpec(a.shape, lambda i, e: (0,) * a.ndim)
    return pl.pallas_call(
        functools.partial(_moe_kernel, final=final),
        grid=(B * tpb, N_EXPERTS // eps),
        in_specs=[tok, full(gain), per_b, per_b, per_b, full(wrt), full(rb),
                  pl.BlockSpec((eps, D, EXPERT_FF), lambda i, e: (e, 0, 0)),
                  pl.BlockSpec((eps, D, EXPERT_FF), lambda i, e: (e, 0, 0)),
                  pl.BlockSpec((eps * EXPERT_FF, D), lambda i, e: (e, 0)),
                  full(fin)],
        out_specs=tok,
        out_shape=jax.ShapeDtypeStruct((B, S, D), F32),
        scratch_shapes=[pltpu.VMEM((tm, D), BF16), pltpu.VMEM((N_EXPERTS, tm, LANES), F32), pltpu.VMEM((tm, D), F32)],
        compiler_params=_cparams(("arbitrary", "arbitrary")),
        name="moe",
    )(x, gain, sc, sh, g2, wrt, rb, wg, wu, wd, fin)


def kernel(x, c, positions, w_mod, b_mod, norm_mix, norm_ffn, w_in, nsa_phi_w1, nsa_phi_w2, nsa_phi_pos, rwkv_mu, rwkv_w_up, rwkv_w0, rwkv_a_up, rwkv_a0, rwkv_g_up, rwkv_k_k, rwkv_k_a, rwkv_r_k, rwkv_ln_w, rwkv_ln_b, norm_nsa_out, norm_moba_out, w_out, w_router, router_bias, moe_w_gate, moe_w_up, moe_w_down, norm_final):
    B, S, D = x.shape
    depth = w_in.shape[0]
    assert S % NSA_TK == 0 and S % MOBA_BLOCK == 0 and S % RWKV_TB == 0 and S >= WINDOW + NSA_QB
    assert S // SLC_BLOCK <= LANES and S // MOBA_BLOCK <= MOBA_NBP

    inv = ROPE_THETA ** (-jnp.arange(0, HEAD_DIM, 2, dtype=F32) / HEAD_DIM)
    ang = positions.astype(F32)[..., None] * inv
    cos, sin = jnp.cos(ang), jnp.sin(ang)
    one, zero = jnp.ones((B, S, HEAD_DIM), F32), jnp.zeros((B, S, HEAD_DIM), F32)
    cos_t = jnp.concatenate([cos, cos, one], axis=-1)
    sin_s = jnp.concatenate([-sin, sin, zero], axis=-1)
    nc = S // CMP_STRIDE
    n_cmp = (S - CMP_BLOCK) // CMP_STRIDE + 1
    pad_c = lambda t, fill: jnp.concatenate([t[:, CMP_BLOCK - 1::CMP_STRIDE][:, :n_cmp],
                                             jnp.full((B, nc - n_cmp, LANES), fill, F32)], axis=1)
    cos_c, sin_c = pad_c(cos_t, 1.0), pad_c(sin_s, 0.0)

    nsa_consts = _nsa_constants(S)
    bd, tri = _rwkv_constants()
    mod = _modulation(c, w_mod, b_mod)
    wrt = w_router.T
    rb = router_bias.reshape(N_EXPERTS, 1)
    fin = norm_final.reshape(1, D)

    for l in range(depth):
        sh1, sc1, g1, sh2, sc2, g2 = (mod[l, :B, i * D:(i + 1) * D].reshape(B, 1, D) for i in range(6))
        p, r, km, vs_t, vw_t, mv_t = _inproj(x, norm_mix[l].reshape(1, D), sc1, sh1, cos_t, sin_s, _prep_w_in(w_in[l]))

        w1, w2, pos4 = _prep_compress(nsa_phi_w1[l], nsa_phi_w2[l], nsa_phi_pos[l])
        chunks = lambda g: p[:, :, g * LANES:(g + 1) * LANES].reshape(B, nc, CMP_STRIDE * LANES)
        kvcmp, kvcmp_t = _compress(chunks(G_KC), chunks(G_VC), pos4, w1, w2, cos_c, sin_c)
        mixn = _nsa(p, r, kvcmp, kvcmp_t, vs_t, vw_t, *nsa_consts)

        kmean = km[:, :, 0, :]
        kmean = jnp.concatenate([kmean, jnp.zeros((B, MOBA_NBP - kmean.shape[1], kmean.shape[2]), F32)], axis=1)
        mixm = _moba(p, mv_t, kmean)

        zl = jnp.zeros((DECAY_LORA, RWKV_WIDTH), F32)
        wup = jnp.concatenate([rwkv_w_up[l], zl], axis=0)
        aup = jnp.concatenate([zl, rwkv_a_up[l]], axis=0)
        vecs = jnp.stack([rwkv_w0[l], rwkv_a0[l], rwkv_k_k[l], rwkv_k_a[l], rwkv_r_k[l].reshape(-1),
                          rwkv_ln_w[l], rwkv_ln_b[l], jnp.zeros((RWKV_WIDTH,), F32)])
        orw = _rwkv(r, rwkv_mu[l].reshape(1, -1), wup, aup, rwkv_g_up[l], vecs, bd, tri)

        wo = w_out[l]
        x = _outproj(mixn, mixm, orw, x, g1,
                     _gain_and_mask(norm_nsa_out[l], _pad_nsa),
                     _gain_and_mask(norm_moba_out[l], _pad_moba),
                     _pad_nsa(wo[:NSA_WIDTH]).astype(BF16),
                     _pad_moba(wo[NSA_WIDTH:NSA_WIDTH + MOBA_WIDTH]).astype(BF16),
                     wo[NSA_WIDTH + MOBA_WIDTH:].astype(BF16))
        x = _moe(x, norm_ffn[l].reshape(1, D), sc2, sh2, g2, wrt, rb,
                 moe_w_gate[l].astype(BF16), moe_w_up[l].astype(BF16), moe_w_down[l].astype(BF16),
                 fin, final=(l == depth - 1))
    return x
```

```python
import functools

import jax
import jax.numpy as jnp
import numpy as np
from jax import lax
from jax.experimental import pallas as pl
from jax.experimental.pallas import tpu as pltpu

F32 = jnp.float32
BF16 = jnp.bfloat16

HEAD_DIM = 64
LANES = 128
ROPE_THETA = 10000.0
NORM_EPS = 1e-6
NEG = -1e30

NSA_HEADS = 6
NSA_KV_HEADS = 2
NSA_GROUP = NSA_HEADS // NSA_KV_HEADS
NSA_WIDTH = NSA_HEADS * HEAD_DIM
NSA_KV_WIDTH = NSA_KV_HEADS * HEAD_DIM
CMP_BLOCK = 32
CMP_STRIDE = 16
SLC_BLOCK = 64
SLC_TOPN = 16
WINDOW = 512
NSA_QB = 256
NSA_TK = 256
NSA_OUT_LANES = 256
NSA_WIN_SPAN = WINDOW + NSA_QB
ATTEND_UNROLL = 4

MOBA_HEADS = 4
MOBA_WIDTH = MOBA_HEADS * HEAD_DIM
MOBA_BLOCK = 256
MOBA_TOPK = 3
MOBA_NBP = 32

RWKV_HEADS = 6
RWKV_WIDTH = RWKV_HEADS * HEAD_DIM
DECAY_LORA = 64
AAA_LORA = 64
GATE_LORA = 128
RWKV_COLS = 3 * RWKV_WIDTH + DECAY_LORA + AAA_LORA + GATE_LORA
DECAY_SCALE = 0.606531
GN_EPS = 64e-5
RWKV_CHUNK = 64
RWKV_TB = 256

N_EXPERTS = 16
N_EXPERT_GROUPS = 4
EXPERTS_PER_GROUP = N_EXPERTS // N_EXPERT_GROUPS
EXPERT_FF = 256
MOE_EXPERTS_PER_STEP = 4

G_NSA_Q = 0
G_SLC = 6
G_WIN = 8
G_KC = 10
G_VC = 11
G_MOBA_Q = 12
G_MOBA_KV = 16
N_GROUPS = 20
ROPED_GROUPS = tuple(range(0, 10)) + tuple(range(12, 20))
P_COLS = N_GROUPS * LANES
R_GATE_BLOCK = RWKV_COLS // LANES
R_COLS = RWKV_COLS + NSA_KV_HEADS * LANES

VMEM_LIMIT = 56 * 1024 * 1024


def _cparams(sem):
    return pltpu.CompilerParams(dimension_semantics=sem, vmem_limit_bytes=VMEM_LIMIT)


def _dot(a, b):
    return jnp.dot(a, b, preferred_element_type=F32)


def _dot_nt(a, b):
    return lax.dot_general(a, b, (((1,), (1,)), ((), ())), preferred_element_type=F32)


def _dot_tn(a, b):
    return lax.dot_general(a, b, (((0,), (0,)), ((), ())), preferred_element_type=F32)


def _split2(x):
    hi = x.astype(BF16)
    lo = (x - hi.astype(F32)).astype(BF16)
    return hi, lo


def _split3(x):
    hi = x.astype(BF16)
    r1 = x - hi.astype(F32)
    mid = r1.astype(BF16)
    lo = (r1 - mid.astype(F32)).astype(BF16)
    return hi, mid, lo


def _mm3(a, b, dot=_dot):
    ah, al = _split2(a)
    bh, bl = _split2(b)
    return dot(ah, bh) + dot(ah, bl) + dot(al, bh)


def _mm1(a, b, dot=_dot):
    return dot(a.astype(BF16), b.astype(BF16))


def _mm3_exact_rhs(a, b_bf16):
    hi, mid, lo = _split3(a)
    return _dot(hi, b_bf16) + _dot(mid, b_bf16) + _dot(lo, b_bf16)


def _rope(y, cos_t, sin_s):
    lane = lax.broadcasted_iota(jnp.int32, y.shape, 1)
    rot = jnp.where(lane < HEAD_DIM // 2, pltpu.roll(y, LANES - HEAD_DIM // 2, 1), pltpu.roll(y, HEAD_DIM // 2, 1))
    return y * cos_t + rot * sin_s


def _mod_kernel(c_ref, w_ref, b_ref, o_ref):
    c = c_ref[...]
    ca = c * jax.nn.sigmoid(c)
    o_ref[0] = _dot(ca.astype(BF16), w_ref[0].astype(BF16)) + b_ref[0]


def _modulation(c, w_mod, b_mod):
    B, D = c.shape
    L, _, N = w_mod.shape
    tn = 512
    c8 = jnp.zeros((8, D), F32).at[:B].set(c)
    return pl.pallas_call(
        _mod_kernel,
        grid=(L, N // tn),
        in_specs=[
            pl.BlockSpec((8, D), lambda l, j: (0, 0)),
            pl.BlockSpec((1, D, tn), lambda l, j: (l, 0, j)),
            pl.BlockSpec((1, 1, tn), lambda l, j: (l, 0, j)),
        ],
        out_specs=pl.BlockSpec((1, 8, tn), lambda l, j: (l, 0, j)),
        out_shape=jax.ShapeDtypeStruct((L, 8, N), F32),
        compiler_params=_cparams(("arbitrary", "arbitrary")),
        name="modulation",
    )(c8, w_mod, b_mod.reshape(L, 1, N))


def _ones_over_values_t(y):
    yt = y.T
    row = lax.broadcasted_iota(jnp.int32, yt.shape, 0)
    return jnp.where(row < HEAD_DIM, 1.0, yt).astype(BF16)


def _inproj_kernel(x_ref, gain_ref, sc_ref, sh_ref, cos_ref, sin_ref, w_ref, p_ref, r_ref, km_ref,
                   vs_ref, vw_ref, mv_ref):
    x = x_ref[0]
    ms = jnp.mean(x * x, axis=-1, keepdims=True)
    h = x * lax.rsqrt(ms + NORM_EPS) * gain_ref[...]
    h = h * (1.0 + sc_ref[0]) + sh_ref[0]
    hb = h.astype(BF16)
    cos_t = cos_ref[0]
    sin_s = sin_ref[0]
    per = 4
    for g0 in range(0, N_GROUPS, per):
        y4 = _dot(hb, w_ref[:, g0 * LANES:(g0 + per) * LANES])
        for j in range(per):
            g = g0 + j
            y = y4[:, j * LANES:(j + 1) * LANES]
            if g in ROPED_GROUPS:
                y = _rope(y, cos_t, sin_s)
            if g >= G_MOBA_KV:
                km = jnp.mean(y, axis=0, keepdims=True)
                km_ref[0, 0, :, (g - G_MOBA_KV) * LANES:(g - G_MOBA_KV + 1) * LANES] = jnp.broadcast_to(km, (8, LANES))
                mv_ref[0, g - G_MOBA_KV, 0] = _ones_over_values_t(y)
            elif G_SLC <= g < G_WIN:
                vs_ref[0, g - G_SLC, 0] = _ones_over_values_t(y)
            elif G_WIN <= g < G_KC:
                yt = _ones_over_values_t(y)
                for t in range(y.shape[0] // NSA_QB):
                    vw_ref[0, g - G_WIN, t] = yt[:, t * NSA_QB:(t + 1) * NSA_QB]
            p_ref[0, :, g * LANES:(g + 1) * LANES] = y.astype(BF16)
    r_ref[0] = _dot(hb, w_ref[:, P_COLS:])


def _inproj(x, gain, sc, sh, cos_t, sin_s, w):
    B, S, D = x.shape
    tm = MOBA_BLOCK
    return pl.pallas_call(
        _inproj_kernel,
        grid=(B, S // tm),
        in_specs=[
            pl.BlockSpec((1, tm, D), lambda b, i: (b, i, 0)),
            pl.BlockSpec((1, D), lambda b, i: (0, 0)),
            pl.BlockSpec((1, 1, D), lambda b, i: (b, 0, 0)),
            pl.BlockSpec((1, 1, D), lambda b, i: (b, 0, 0)),
            pl.BlockSpec((1, tm, LANES), lambda b, i: (b, i, 0)),
            pl.BlockSpec((1, tm, LANES), lambda b, i: (b, i, 0)),
            pl.BlockSpec((D, P_COLS + R_COLS), lambda b, i: (0, 0)),
        ],
        out_specs=[
            pl.BlockSpec((1, tm, P_COLS), lambda b, i: (b, i, 0)),
            pl.BlockSpec((1, tm, R_COLS), lambda b, i: (b, i, 0)),
            pl.BlockSpec((1, 1, 8, MOBA_HEADS * LANES), lambda b, i: (b, i, 0, 0)),
            pl.BlockSpec((1, NSA_KV_HEADS, 1, LANES, tm), lambda b, i: (b, 0, i, 0, 0)),
            pl.BlockSpec((1, NSA_KV_HEADS, tm // NSA_QB, LANES, NSA_QB), lambda b, i: (b, 0, i, 0, 0)),
            pl.BlockSpec((1, MOBA_HEADS, 1, LANES, tm), lambda b, i: (b, 0, i, 0, 0)),
        ],
        out_shape=[
            jax.ShapeDtypeStruct((B, S, P_COLS), BF16),
            jax.ShapeDtypeStruct((B, S, R_COLS), F32),
            jax.ShapeDtypeStruct((B, S // tm, 8, MOBA_HEADS * LANES), F32),
            jax.ShapeDtypeStruct((B, NSA_KV_HEADS, S // tm, LANES, tm), BF16),
            jax.ShapeDtypeStruct((B, NSA_KV_HEADS, S // NSA_QB, LANES, NSA_QB), BF16),
            jax.ShapeDtypeStruct((B, MOBA_HEADS, S // tm, LANES, tm), BF16),
        ],
        compiler_params=_cparams(("arbitrary", "arbitrary")),
        name="inproj",
    )(x, gain, sc, sh, cos_t, sin_s, w)


def _prep_w_in(w):
    D = w.shape[0]
    o = 0
    parts = {}
    for name, width in (("nq", NSA_WIDTH), ("nkc", NSA_KV_WIDTH), ("nvc", NSA_KV_WIDTH), ("nks", NSA_KV_WIDTH),
                        ("nvs", NSA_KV_WIDTH), ("nkw", NSA_KV_WIDTH), ("nvw", NSA_KV_WIDTH), ("ngate", NSA_HEADS * 3),
                        ("mq", MOBA_WIDTH), ("mk", MOBA_WIDTH), ("mv", MOBA_WIDTH), ("rf", RWKV_COLS)):
        parts[name] = w[:, o:o + width]
        o += width
    hd = lambda t, h: t[:, h * HEAD_DIM:(h + 1) * HEAD_DIM]
    z = jnp.zeros((D, HEAD_DIM), F32)
    scale = HEAD_DIM ** -0.5
    cols = []
    for h in range(NSA_HEADS):
        cols += [hd(parts["nq"], h) * scale, z]
    for h in range(NSA_KV_HEADS):
        cols += [hd(parts["nks"], h), hd(parts["nvs"], h)]
    for h in range(NSA_KV_HEADS):
        cols += [hd(parts["nkw"], h), hd(parts["nvw"], h)]
    cols += [parts["nkc"], parts["nvc"]]
    for h in range(MOBA_HEADS):
        cols += [hd(parts["mq"], h) * scale, z]
    for h in range(MOBA_HEADS):
        cols += [hd(parts["mk"], h), hd(parts["mv"], h)]
    cols += [parts["rf"]]
    per = NSA_GROUP * 3
    for h in range(NSA_KV_HEADS):
        cols += [parts["ngate"][:, h * per:(h + 1) * per], jnp.zeros((D, LANES - per), F32)]
    return jnp.concatenate(cols, axis=1).astype(BF16)


def _cmp_kernel(kc_ref, vc_ref, pos_ref, w1_ref, w2_ref, cos_ref, sin_ref, o_ref, ot_ref):
    nc = kc_ref.shape[1]

    def hidden(t_ref, ia, ib):
        t = t_ref[0].astype(F32)
        a = _dot((t + pos_ref[ia:ia + 1, :]).astype(BF16), w1_ref[ia])
        b = _dot((t + pos_ref[ib:ib + 1, :]).astype(BF16), w1_ref[ib])
        return jax.nn.gelu(a + pltpu.roll(b, nc - 1, 0)).astype(BF16)

    gk = hidden(kc_ref, 0, 1)
    gv = hidden(vc_ref, 2, 3)
    for h in range(NSA_KV_HEADS):
        y = _dot(gk, w2_ref[2 * h]) + _dot(gv, w2_ref[2 * h + 1])
        y = _rope(y, cos_ref[0], sin_ref[0])
        o_ref[0, h] = y.astype(BF16)
        ot_ref[0, h] = _ones_over_values_t(y)


def _compress(kc16, vc16, pos4, w1, w2, cos_c, sin_c):
    B, nc, K = kc16.shape
    return pl.pallas_call(
        _cmp_kernel,
        grid=(B,),
        in_specs=[
            pl.BlockSpec((1, nc, K), lambda b: (b, 0, 0)),
            pl.BlockSpec((1, nc, K), lambda b: (b, 0, 0)),
            pl.BlockSpec((4, K), lambda b: (0, 0)),
            pl.BlockSpec((4, K, LANES), lambda b: (0, 0, 0)),
            pl.BlockSpec((4, LANES, LANES), lambda b: (0, 0, 0)),
            pl.BlockSpec((1, nc, LANES), lambda b: (b, 0, 0)),
            pl.BlockSpec((1, nc, LANES), lambda b: (b, 0, 0)),
        ],
        out_specs=[pl.BlockSpec((1, NSA_KV_HEADS, nc, LANES), lambda b: (b, 0, 0, 0)),
                   pl.BlockSpec((1, NSA_KV_HEADS, LANES, nc), lambda b: (b, 0, 0, 0))],
        out_shape=[jax.ShapeDtypeStruct((B, NSA_KV_HEADS, nc, LANES), BF16),
                   jax.ShapeDtypeStruct((B, NSA_KV_HEADS, LANES, nc), BF16)],
        compiler_params=_cparams(("arbitrary",)),
        name="nsa_compress",
    )(kc16, vc16, pos4, w1, w2, cos_c, sin_c)


def _prep_compress(phi_w1, phi_w2, phi_pos):
    half = CMP_BLOCK // 2
    eye = jnp.eye(NSA_KV_HEADS, dtype=F32)
    w1, pos = [], []
    for t in range(2):
        for part in range(2):
            w = phi_w1[t, part * half:(part + 1) * half]
            w1.append(jnp.einsum("lde,kK->lkdKe", w, eye).reshape(half * NSA_KV_WIDTH, NSA_KV_WIDTH))
            p = phi_pos[t, part * half:(part + 1) * half]
            pos.append(jnp.broadcast_to(p[:, None, :], (half, NSA_KV_HEADS, HEAD_DIM)).reshape(-1))
    w2 = []
    for h in range(NSA_KV_HEADS):
        for t in range(2):
            m = jnp.zeros((LANES, LANES), F32)
            m = m.at[h * HEAD_DIM:(h + 1) * HEAD_DIM, t * HEAD_DIM:(t + 1) * HEAD_DIM].set(phi_w2[t])
            w2.append(m)
    return jnp.stack(w1).astype(BF16), jnp.stack(w2).astype(BF16), jnp.stack(pos)


def _attend_tiles(n_loop, scores, mask_last, values_t, s_a, s_b, p_scr, alpha_scr, m_scr, acc_scr):
    m_scr[...] = jnp.full(m_scr.shape, -jnp.inf, F32)
    alpha_scr[...] = jnp.ones(alpha_scr.shape, F32)
    acc_scr[...] = jnp.zeros(acc_scr.shape, F32)
    p_scr[...] = jnp.zeros(p_scr.shape, BF16)

    def apply_weights(j):
        acc_scr[...] = alpha_scr[0:1, :] * acc_scr[...] + _dot(values_t(j), p_scr[...])

    def softmax_tile(s_t):
        m_prev = m_scr[0:1, :]
        m_new = jnp.maximum(m_prev, jnp.max(s_t, axis=0, keepdims=True))
        alpha_scr[0:1, :] = jnp.exp(m_prev - m_new)
        p_scr[...] = jnp.exp(s_t - m_new).astype(BF16)
        m_scr[0:1, :] = m_new

    def step(j, cur, nxt):
        apply_weights(jnp.maximum(j - 1, 0))
        softmax_tile(cur[...])
        nxt[...] = scores(j + 1)

    s_a[...] = scores(0)

    def quad(i, carry):
        for u in range(ATTEND_UNROLL):
            step(ATTEND_UNROLL * i + u, (s_a, s_b)[u % 2], (s_b, s_a)[u % 2])
        return carry

    lax.fori_loop(0, n_loop // ATTEND_UNROLL, quad, 0)
    done = n_loop - n_loop % ATTEND_UNROLL

    @pl.when(n_loop % ATTEND_UNROLL >= 2)
    def _():
        step(done, s_a, s_b)
        step(done + 1, s_b, s_a)

    @pl.when(n_loop % 2 == 1)
    def _():
        step(n_loop - 1, s_a, s_a)

    apply_weights(jnp.maximum(n_loop - 1, 0))
    softmax_tile(mask_last(s_a[...]))
    apply_weights(n_loop)
    return acc_scr[...]


def _attend_scratch(tk, nq):
    return [pltpu.VMEM((tk, nq), F32), pltpu.VMEM((tk, nq), F32), pltpu.VMEM((tk, nq), BF16),
            pltpu.VMEM((8, nq), F32), pltpu.VMEM((8, nq), F32), pltpu.VMEM((LANES, nq), F32)]


def _nsa_kernel(q0_ref, q1_ref, q2_ref, kvc_ref, kvct_ref, kvs_ref, vst_ref, kvw_ref, vwt_ref, gate_ref, mt_ref,
                et_ref, rel_ref, crel_ref, o_ref, s_a, s_b, p_scr, alpha_scr, m_scr, acc_scr, *, n_pick):
    ci = pl.program_id(2)
    qb = NSA_QB
    nq = NSA_GROUP * qb
    q = jnp.concatenate([q0_ref[0], q1_ref[0], q2_ref[0]], axis=0)

    kc = kvc_ref[0, 0]
    sm = jnp.where(crel_ref[...] <= ci * qb, _dot_nt(kc, q), NEG)
    mx = jnp.max(sm, axis=0, keepdims=True)
    e = jnp.exp(sm - mx)
    den = jnp.maximum(jnp.sum(e, axis=0, keepdims=True), 1e-30)
    p_c = e * jnp.where(mx > 0.5 * NEG, 1.0 / den, 0.0)
    o_c = _dot(kvct_ref[0, 0], p_c.astype(BF16))

    hi, mid, lo = _split3(p_c[:, 0:qb] + p_c[:, qb:2 * qb] + p_c[:, 2 * qb:3 * qb])
    mt = mt_ref[...]
    imp_t = _dot(mt, hi) + _dot(mt, mid) + _dot(mt, lo)
    blk = lax.broadcasted_iota(jnp.int32, imp_t.shape, 0)
    cur = (ci * qb + lax.broadcasted_iota(jnp.int32, imp_t.shape, 1)) // SLC_BLOCK
    forced = (blk == 0) | (blk == cur) | (blk == cur - 1)
    free = (blk <= cur) & jnp.logical_not(forced)
    sc = jnp.where(free, imp_t, -1.0)
    for _ in range(n_pick):
        best = jnp.max(sc, axis=0, keepdims=True)
        idx = jnp.min(jnp.where(sc == best, blk, imp_t.shape[0]), axis=0, keepdims=True)
        sc = jnp.where(blk == idx, -2.0, sc)
    bias = jnp.where(free, jnp.where(sc < -1.5, 0.0, NEG), jnp.where(forced, 0.0, NEG)).T.astype(BF16)
    w_nt = jnp.concatenate([q, jnp.concatenate([bias] * NSA_GROUP, axis=0)], axis=1)
    o_w = _nsa_window(ci, q, kvw_ref, vwt_ref, rel_ref)

    tk = NSA_TK
    last = (ci * qb) // tk

    def scores(j):
        kv = kvs_ref[0, pl.ds(pl.multiple_of(j * tk, tk), tk), :]
        return _dot_nt(jnp.concatenate([kv, et_ref[j]], axis=1), w_nt)

    def causal_edge(s_t):
        return jnp.where(rel_ref[0:tk, :] <= ci * qb - last * tk, s_t, NEG)

    acc = _attend_tiles(last, scores, causal_edge, lambda j: vst_ref[0, 0, j],
                        s_a, s_b, p_scr, alpha_scr, m_scr, acc_scr)
    o_s = acc[HEAD_DIM:] * (1.0 / acc[0:1])

    gt = jax.nn.sigmoid(gate_ref[0]).T
    outs = []
    for g in range(NSA_GROUP):
        ls = slice(g * qb, (g + 1) * qb)
        outs.append(gt[3 * g:3 * g + 1] * o_c[HEAD_DIM:, ls] + gt[3 * g + 1:3 * g + 2] * o_s[:, ls]
                    + gt[3 * g + 2:3 * g + 3] * o_w[:, ls])
    outs.append(jnp.zeros((NSA_OUT_LANES - NSA_GROUP * HEAD_DIM, qb), F32))
    o_ref[0] = jnp.concatenate(outs, axis=0).T


def _nsa_window(ci, q, kvw_ref, vwt_ref, rel_ref):
    qb = NSA_QB
    n_wt = NSA_WIN_SPAN // qb
    first = jnp.maximum(ci - WINDOW // qb, 0)
    kvw = kvw_ref[0, pl.ds(pl.multiple_of(first * qb, qb), NSA_WIN_SPAN), :]
    s_t = _dot_nt(kvw, q)
    dist = (ci - first) * qb - rel_ref[...]
    sm = jnp.where(lax.bitcast_convert_type(dist, jnp.uint32) < WINDOW, s_t, NEG)
    e_w = jnp.exp(sm - jnp.max(sm, axis=0, keepdims=True))
    vw_t = jnp.concatenate([vwt_ref[0, 0, first + t] for t in range(n_wt)], axis=1)
    acc_w = _dot(vw_t, e_w.astype(BF16))
    return acc_w[HEAD_DIM:] * (1.0 / acc_w[0:1])


def _nsa(p, r, kvcmp, kvcmp_t, vs_t, vw_t, cmp_to_slc_t, e3, rel, crel):
    B, S, _ = p.shape
    nc = kvcmp.shape[2]
    qb = NSA_QB
    n_top = min(SLC_TOPN, S // SLC_BLOCK)
    qspec = lambda g: pl.BlockSpec((1, qb, LANES), lambda b, h, i: (b, i, G_NSA_Q + NSA_GROUP * h + g))
    nq = NSA_GROUP * qb
    return pl.pallas_call(
        functools.partial(_nsa_kernel, n_pick=max(n_top - 3, 0)),
        grid=(B, NSA_KV_HEADS, S // qb),
        in_specs=[
            qspec(0), qspec(1), qspec(2),
            pl.BlockSpec((1, 1, nc, LANES), lambda b, h, i: (b, h, 0, 0)),
            pl.BlockSpec((1, 1, LANES, nc), lambda b, h, i: (b, h, 0, 0)),
            pl.BlockSpec((1, S, LANES), lambda b, h, i: (b, 0, G_SLC + h)),
            pl.BlockSpec((1, 1) + vs_t.shape[2:], lambda b, h, i: (b, h, 0, 0, 0)),
            pl.BlockSpec((1, S, LANES), lambda b, h, i: (b, 0, G_WIN + h)),
            pl.BlockSpec((1, 1) + vw_t.shape[2:], lambda b, h, i: (b, h, 0, 0, 0)),
            pl.BlockSpec((1, qb, LANES), lambda b, h, i: (b, i, R_GATE_BLOCK + h)),
            pl.BlockSpec(cmp_to_slc_t.shape, lambda b, h, i: (0, 0)),
            pl.BlockSpec(e3.shape, lambda b, h, i: (0, 0, 0)),
            pl.BlockSpec(rel.shape, lambda b, h, i: (0, 0)),
            pl.BlockSpec(crel.shape, lambda b, h, i: (0, 0)),
        ],
        out_specs=pl.BlockSpec((1, qb, NSA_OUT_LANES), lambda b, h, i: (b, i, h)),
        out_shape=jax.ShapeDtypeStruct((B, S, NSA_KV_HEADS * NSA_OUT_LANES), F32),
        scratch_shapes=_attend_scratch(NSA_TK, nq),
        compiler_params=_cparams(("arbitrary", "arbitrary", "arbitrary")),
        name="nsa_attention",
    )(p, p, p, kvcmp, kvcmp_t, p, vs_t, p, vw_t, r, cmp_to_slc_t, e3, rel, crel)


def _nsa_constants(S):
    n_cmp = (S - CMP_BLOCK) // CMP_STRIDE + 1
    nc = S // CMP_STRIDE
    n_slc = S // SLC_BLOCK
    c_start = np.arange(nc) * CMP_STRIDE
    s_start = np.arange(LANES) * SLC_BLOCK
    overlap = (np.minimum(c_start[None, :] + CMP_BLOCK, s_start[:, None] + SLC_BLOCK)
               - np.maximum(c_start[None, :], s_start[:, None]))
    m_t = np.clip(overlap, 0, None).astype(np.float32) / CMP_BLOCK
    m_t = np.where((np.arange(nc)[None, :] < n_cmp) & (np.arange(LANES)[:, None] < n_slc), m_t, 0.0)
    key_blk = (np.arange(S) // SLC_BLOCK).reshape(S // NSA_TK, NSA_TK, 1)
    e3 = (key_blk == np.arange(LANES)[None, None, :]).astype(np.float32)
    lane_q = np.arange(NSA_GROUP * NSA_QB) % NSA_QB
    rel = np.arange(NSA_WIN_SPAN)[:, None] - lane_q[None, :]
    crel = (np.arange(nc) * CMP_STRIDE + CMP_BLOCK - 1)[:, None] - lane_q[None, :]
    return jnp.asarray(m_t, BF16), jnp.asarray(e3, BF16), jnp.asarray(rel, jnp.int32), jnp.asarray(crel, jnp.int32)


def _moba_kernel(q_ref, kv_ref, vt_ref, km_ref, o_ref, sel_scr, s_a, s_b, p_scr, alpha_scr, m_scr, acc_scr):
    ci = pl.program_id(2)
    qb = MOBA_BLOCK
    q = q_ref[0]
    km = km_ref[0]
    km_hi, km_lo = _split2(km)
    gate_t = _dot_nt(km_hi, q) + _dot_nt(km_lo, q)
    blk = lax.broadcasted_iota(jnp.int32, gate_t.shape, 0)
    valid = blk < ci
    sc = jnp.where(valid, gate_t, -jnp.inf)
    picked = jnp.zeros(gate_t.shape, F32)
    for _ in range(MOBA_TOPK):
        best = jnp.max(sc, axis=0, keepdims=True)
        idx = jnp.min(jnp.where(sc == best, blk, gate_t.shape[0]), axis=0, keepdims=True)
        pick = blk == idx
        picked = jnp.where(pick, 1.0, picked)
        sc = jnp.where(pick, -jnp.inf, sc)
    sel = jnp.where(valid, picked, jnp.where(blk == ci, 1.0, 0.0))
    sel_scr[...] = jnp.where(sel > 0.5, 0.0, NEG)

    def scores(j):
        kvj = kv_ref[0, pl.ds(pl.multiple_of(j * qb, qb), qb), :]
        return _dot_nt(kvj, q) + sel_scr[pl.ds(j, 1), :]

    def causal_edge(s_t):
        causal = lax.broadcasted_iota(jnp.int32, s_t.shape, 0) <= lax.broadcasted_iota(jnp.int32, s_t.shape, 1)
        return jnp.where(causal, s_t, NEG)

    acc = _attend_tiles(ci, scores, causal_edge, lambda j: vt_ref[0, 0, j],
                        s_a, s_b, p_scr, alpha_scr, m_scr, acc_scr)
    o_ref[0] = (acc * (1.0 / acc[0:1])).T


def _moba(p, mv_t, kmean):
    B, S, _ = p.shape
    qb = MOBA_BLOCK
    return pl.pallas_call(
        _moba_kernel,
        grid=(B, MOBA_HEADS, S // qb),
        in_specs=[
            pl.BlockSpec((1, qb, LANES), lambda b, h, i: (b, i, G_MOBA_Q + h)),
            pl.BlockSpec((1, S, LANES), lambda b, h, i: (b, 0, G_MOBA_KV + h)),
            pl.BlockSpec((1, 1) + mv_t.shape[2:], lambda b, h, i: (b, h, 0, 0, 0)),
            pl.BlockSpec((1, MOBA_NBP, LANES), lambda b, h, i: (b, 0, h)),
        ],
        out_specs=pl.BlockSpec((1, qb, LANES), lambda b, h, i: (b, i, h)),
        out_shape=jax.ShapeDtypeStruct((B, S, MOBA_HEADS * LANES), F32),
        scratch_shapes=[pltpu.VMEM((MOBA_NBP, qb), F32)] + _attend_scratch(qb, qb),
        compiler_params=_cparams(("arbitrary", "arbitrary", "arbitrary")),
        name="moba_attention",
    )(p, p, mv_t, kmean)


def _rwkv_kernel(f_ref, mu_ref, wup_ref, aup_ref, gup_ref, vec_ref, bd_ref, tri_ref, o_ref, carry_scr, st_scr):
    i = pl.program_id(1)
    tb = f_ref.shape[1]
    C = RWKV_CHUNK
    W = RWKV_WIDTH

    @pl.when(i == 0)
    def _():
        carry_scr[...] = jnp.zeros(carry_scr.shape, F32)
        st_scr[...] = jnp.zeros(st_scr.shape, F32)

    feat = f_ref[0]
    rowi = lax.broadcasted_iota(jnp.int32, feat.shape, 0)
    prev = jnp.where(rowi == 0, carry_scr[0:1, :], pltpu.roll(feat, 1, 0))
    carry_scr[0:1, :] = feat[tb - 1:tb, :]
    xs = feat + (prev - feat) * mu_ref[...]
    r = xs[:, 0:W]
    k = xs[:, W:2 * W]
    v = xs[:, 2 * W:3 * W]
    wa = xs[:, 3 * W:3 * W + DECAY_LORA + AAA_LORA]
    gd = xs[:, 3 * W + DECAY_LORA + AAA_LORA:]
    w0, a0, k_k, k_a, r_k, ln_w, ln_b = (vec_ref[n:n + 1, :] for n in range(7))
    bd = bd_ref[...]

    def hsum(t):
        hi, lo = _split2(t)
        return _dot(hi, bd) + _dot(lo, bd)

    logw = -DECAY_SCALE * jax.nn.sigmoid(w0 + _mm3(jnp.tanh(wa), wup_ref[...]))
    a = jax.nn.sigmoid(a0 + _mm1(wa, aup_ref[...]))
    gate = _mm1(jax.nn.sigmoid(gd), gup_ref[...])
    kk = k * k_k
    kk = kk / jnp.maximum(jnp.sqrt(hsum(kk * kk)), 1e-12)
    k = k * (1.0 + (a - 1.0) * k_a)
    bonus = hsum(r * k * r_k) * v
    kka = kk * a

    ri = lax.broadcasted_iota(jnp.int32, (tb, tb), 0)
    cj = lax.broadcasted_iota(jnp.int32, (tb, tb), 1)
    same = (ri // C) == (cj // C)
    strict = same & (cj < ri)
    incl = same & (cj <= ri)
    eye = jnp.where(ri == cj, 1.0, 0.0)
    eye_c = eye[0:C, 0:C]
    hi, mid, lo = _split3(logw)
    tri = tri_ref[0]
    blk = tri_ref[1]
    cum = _dot(tri, hi) + _dot(tri, mid) + _dot(tri, lo)
    tot = _dot(blk, hi) + _dot(blk, mid) + _dot(blk, lo)
    e_neg = jnp.exp(-cum)
    e_end = jnp.exp(tot - cum)
    d_end = jnp.exp(tot)
    a_t = -kk * jnp.exp(cum - logw)
    r_t = r * jnp.exp(cum)
    b_t = kka * e_neg
    k_t = k * e_neg
    b_e = kka * e_end
    k_e = k * e_end
    heads = range(RWKV_HEADS)
    hs = [slice(h * HEAD_DIM, (h + 1) * HEAD_DIM) for h in heads]
    bf = lambda t: t.astype(BF16)
    ah = [bf(a_t[:, s]) for s in hs]
    rh = [r_t[:, s] for s in hs]
    vh = [bf(v[:, s]) for s in hs]
    ar = [jnp.concatenate([ah[h], bf(rh[h])], axis=0) for h in heads]
    xb = [_dot_nt(ar[h], bf(b_t[:, hs[h]])) for h in heads]
    xk = [_dot_nt(ar[h], bf(k_t[:, hs[h]])) for h in heads]
    n = [bf(jnp.where(strict, xb[h][0:tb], 0.0)) for h in heads]
    m_ak = [bf(jnp.where(strict, xk[h][0:tb], 0.0)) for h in heads]
    m_rb = [bf(jnp.where(incl, xb[h][tb:], 0.0)) for h in heads]
    m_rk = [bf(jnp.where(incl, xk[h][tb:], 0.0)) for h in heads]
    tinv = [eye + n[h].astype(F32) for h in heads]
    npow = n
    step = 1
    while 2 * step < C:
        npow = [bf(_dot(npow[h], npow[h])) for h in heads]
        tinv = [tinv[h] + _dot(bf(tinv[h]), npow[h]) for h in heads]
        step *= 2
    tinv = [bf(t) for t in tinv]
    g = [_dot(tinv[h], ah[h]) for h in heads]
    u0 = [_dot(tinv[h], bf(_dot(m_ak[h], vh[h]))) for h in heads]
    gb = [bf(t) for t in g]
    ub = [bf(t) for t in u0]
    r_y = [rh[h] + _dot(m_rb[h], gb[h]) for h in heads]
    y0 = [_dot(m_rb[h], ub[h]) + _dot(m_rk[h], vh[h]) for h in heads]
    beh = [bf(b_e[:, s]) for s in hs]
    keh = [bf(k_e[:, s]) for s in hs]
    st = [st_scr[h] for h in heads]
    yh = [[] for _ in heads]
    for c in range(tb // C):
        cs = slice(c * C, (c + 1) * C)
        p_st = [eye_c * d_end[c * C:c * C + 1, hs[h]] + _dot_tn(beh[h][cs], gb[h][cs]) for h in heads]
        q_st = [_dot_tn(beh[h][cs], ub[h][cs]) + _dot_tn(keh[h][cs], vh[h][cs]) for h in heads]
        for h in heads:
            yh[h].append(_mm3(r_y[h][cs], st[h]) + y0[h][cs])
        st = [_mm3(p_st[h], st[h]) + q_st[h] for h in heads]
    ys = []
    for h in heads:
        st_scr[h] = st[h]
        y = jnp.concatenate(yh[h], axis=0)
        mean = jnp.mean(y, axis=-1, keepdims=True)
        yc = y - mean
        var = jnp.mean(yc * yc, axis=-1, keepdims=True)
        ys.append(yc * lax.rsqrt(var + GN_EPS))
    yn = jnp.concatenate(ys, axis=1)
    o_ref[0] = (yn * ln_w + ln_b + bonus) * gate


def _rwkv(r, mu, wup, aup, gup, vecs, bd, tri):
    B, S, _ = r.shape
    tb = RWKV_TB
    full = lambda a: pl.BlockSpec(a.shape, lambda b, i: (0,) * a.ndim)
    return pl.pallas_call(
        _rwkv_kernel,
        grid=(B, S // tb),
        in_specs=[pl.BlockSpec((1, tb, RWKV_COLS), lambda b, i: (b, i, 0)),
                  full(mu), full(wup), full(aup), full(gup), full(vecs), full(bd), full(tri)],
        out_specs=pl.BlockSpec((1, tb, RWKV_WIDTH), lambda b, i: (b, i, 0)),
        out_shape=jax.ShapeDtypeStruct((B, S, RWKV_WIDTH), F32),
        scratch_shapes=[pltpu.VMEM((8, RWKV_COLS), F32), pltpu.VMEM((RWKV_HEADS, HEAD_DIM, HEAD_DIM), F32)],
        compiler_params=_cparams(("arbitrary", "arbitrary")),
        name="rwkv7",
    )(r, mu, wup, aup, gup, vecs, bd, tri)


def _rwkv_constants():
    head = np.arange(RWKV_WIDTH) // HEAD_DIM
    bd = (head[:, None] == head[None, :]).astype(np.float32)
    t = np.arange(RWKV_TB)
    same = (t[None, :] // RWKV_CHUNK) == (t[:, None] // RWKV_CHUNK)
    tri = np.stack([same & (t[None, :] <= t[:, None]), same]).astype(np.float32)
    return jnp.asarray(bd, BF16), jnp.asarray(tri, BF16)


def _outproj_kernel(mn_ref, mm_ref, rw_ref, x_ref, g1_ref, gn_ref, gm_ref, wn_ref, wm_ref, wr_ref, o_ref):
    def head_norm(o, gain_ref, width):
        o = jnp.where(gain_ref[1:2, :] > 0.5, o, 0.0)
        ms = jnp.sum(o * o, axis=-1, keepdims=True) * (1.0 / width)
        return (o * lax.rsqrt(ms + NORM_EPS) * gain_ref[0:1, :]).astype(BF16)

    z = _dot(head_norm(mn_ref[0], gn_ref, NSA_WIDTH), wn_ref[...])
    z = z + _dot(head_norm(mm_ref[0], gm_ref, MOBA_WIDTH), wm_ref[...])
    z = z + _dot(rw_ref[0].astype(BF16), wr_ref[...])
    o_ref[0] = x_ref[0] + g1_ref[0] * z


def _outproj(mixn, mixm, orw, x, g1, gn, gm, wn, wm, wr):
    B, S, D = x.shape
    tm = 512
    full = lambda a: pl.BlockSpec(a.shape, lambda b, i: (0,) * a.ndim)
    tok = lambda a: pl.BlockSpec((1, tm, a.shape[2]), lambda b, i: (b, i, 0))
    return pl.pallas_call(
        _outproj_kernel,
        grid=(B, S // tm),
        in_specs=[tok(mixn), tok(mixm), tok(orw), tok(x), pl.BlockSpec((1, 1, D), lambda b, i: (b, 0, 0)),
                  full(gn), full(gm), full(wn), full(wm), full(wr)],
        out_specs=tok(x),
        out_shape=jax.ShapeDtypeStruct((B, S, D), F32),
        compiler_params=_cparams(("arbitrary", "arbitrary")),
        name="outproj",
    )(mixn, mixm, orw, x, g1, gn, gm, wn, wm, wr)


def _pad_moba(t):
    t = t.reshape((MOBA_HEADS, HEAD_DIM) + t.shape[1:])
    return jnp.concatenate([jnp.zeros_like(t), t], axis=1).reshape((MOBA_HEADS * LANES,) + t.shape[2:])


def _pad_nsa(t):
    w = NSA_GROUP * HEAD_DIM
    t = t.reshape((NSA_KV_HEADS, w) + t.shape[1:])
    pad = jnp.zeros((NSA_KV_HEADS, NSA_OUT_LANES - w) + t.shape[2:], t.dtype)
    return jnp.concatenate([t, pad], axis=1).reshape((NSA_KV_HEADS * NSA_OUT_LANES,) + t.shape[2:])


def _gain_and_mask(gain, pad):
    return jnp.stack([pad(gain), pad(jnp.ones_like(gain))])


def _route(logit_t, bias_col):
    aff = jax.nn.sigmoid(logit_t)
    biased = aff + bias_col
    row = lambda t, e: t[e:e + 1, :]
    gp = EXPERTS_PER_GROUP
    scores = []
    for g in range(N_EXPERT_GROUPS):
        a_, b_, c_, d_ = (row(biased, g * gp + j) for j in range(gp))
        hi1, lo1, hi2, lo2 = jnp.maximum(a_, b_), jnp.minimum(a_, b_), jnp.maximum(c_, d_), jnp.minimum(c_, d_)
        scores.append(jnp.maximum(hi1, hi2) + jnp.maximum(jnp.minimum(hi1, hi2), jnp.maximum(lo1, lo2)))
    best = jnp.zeros_like(scores[0], dtype=jnp.int32)
    best_s = scores[0]
    for g in range(1, N_EXPERT_GROUPS):
        better = scores[g] > best_s
        best = jnp.where(better, g, best)
        best_s = jnp.where(better, scores[g], best_s)

    def in_group(t, j):
        out = row(t, j)
        for g in range(1, N_EXPERT_GROUPS):
            out = jnp.where(best == g, row(t, g * gp + j), out)
        return out

    vals = [in_group(biased, j) for j in range(gp)]
    affs = [in_group(aff, j) for j in range(gp)]

    def first_argmax(vs):
        top = functools.reduce(jnp.maximum, vs)
        idx = jnp.full(top.shape, gp, jnp.int32)
        for j in reversed(range(gp)):
            idx = jnp.where(vs[j] == top, j, idx)
        return idx

    i1 = first_argmax(vals)
    i2 = first_argmax([jnp.where(i1 == j, -jnp.inf, vals[j]) for j in range(gp)])
    pick = lambda idx: functools.reduce(lambda acc, j: jnp.where(idx == j, affs[j], acc), range(gp), jnp.zeros_like(affs[0]))
    w1, w2 = pick(i1), pick(i2)
    tot = w1 + w2
    e_iota = lax.broadcasted_iota(jnp.int32, logit_t.shape, 0)
    e1 = best * gp + i1
    e2 = best * gp + i2
    return jnp.where(e_iota == e1, w1 / tot, 0.0) + jnp.where(e_iota == e2, w2 / tot, 0.0)


def _moe_kernel(x_ref, gain_ref, sc_ref, sh_ref, g2_ref, wrt_ref, rb_ref, wg_ref, wu_ref, wd_ref, fin_ref,
                o_ref, hb_scr, cb_scr, acc_scr, *, final):
    e = pl.program_id(1)
    tm = x_ref.shape[1]

    @pl.when(e == 0)
    def _():
        x = x_ref[0]
        ms = jnp.mean(x * x, axis=-1, keepdims=True)
        h = x * lax.rsqrt(ms + NORM_EPS) * gain_ref[...]
        h = h * (1.0 + sc_ref[0]) + sh_ref[0]
        hb_scr[...] = h.astype(BF16)
        comb = _route(_mm3(wrt_ref[...], h, _dot_nt), rb_ref[...])
        cb_scr[...] = jnp.concatenate([comb, jnp.zeros((LANES - N_EXPERTS, tm), F32)], axis=0).T
        acc_scr[...] = jnp.zeros(acc_scr.shape, F32)

    hb = hb_scr[...]
    comb_t = cb_scr[...]
    hes = []
    for j in range(MOE_EXPERTS_PER_STEP):
        hg = _dot(hb, wg_ref[j])
        hu = _dot(hb, wu_ref[j])
        cbe = jnp.broadcast_to(comb_t[:, j:j + 1], (tm, LANES))
        he = hg * jax.nn.sigmoid(hg) * hu * jnp.concatenate([cbe] * (EXPERT_FF // LANES), axis=1)
        hes.append(he.astype(BF16))
    cb_scr[...] = pltpu.roll(comb_t, LANES - MOE_EXPERTS_PER_STEP, 1)
    acc_scr[...] += _dot(jnp.concatenate(hes, axis=1), wd_ref[...])

    @pl.when(e == N_EXPERTS // MOE_EXPERTS_PER_STEP - 1)
    def _():
        out = x_ref[0] + g2_ref[0] * acc_scr[...]
        if final:
            ms = jnp.mean(out * out, axis=-1, keepdims=True)
            out = out * lax.rsqrt(ms + NORM_EPS) * fin_ref[...]
        o_ref[0] = out


def _moe(x, gain, sc, sh, g2, wrt, rb, wg, wu, wd, fin, final):
    B, S, D = x.shape
    tm = 512
    tpb = S // tm
    eps = MOE_EXPERTS_PER_STEP
    wd = wd.reshape(N_EXPERTS * EXPERT_FF, D)
    tok = pl.BlockSpec((1, tm, D), lambda i, e: (i // tpb, i % tpb, 0))
    per_b = pl.BlockSpec((1, 1, D), lambda i, e: (i // tpb, 0, 0))
    full = lambda a: pl.BlockSpec(a.shape, lambda i, e: (0,) * a.ndim)
    return pl.pallas_call(
        functools.partial(_moe_kernel, final=final),
        grid=(B * tpb, N_EXPERTS // eps),
        in_specs=[tok, full(gain), per_b, per_b, per_b, full(wrt), full(rb),
                  pl.BlockSpec((eps, D, EXPERT_FF), lambda i, e: (e, 0, 0)),
                  pl.BlockSpec((eps, D, EXPERT_FF), lambda i, e: (e, 0, 0)),
                  pl.BlockSpec((eps * EXPERT_FF, D), lambda i, e: (e, 0)),
                  full(fin)],
        out_specs=tok,
        out_shape=jax.ShapeDtypeStruct((B, S, D), F32),
        scratch_shapes=[pltpu.VMEM((tm, D), BF16), pltpu.VMEM((tm, LANES), F32), pltpu.VMEM((tm, D), F32)],
        compiler_params=_cparams(("arbitrary", "arbitrary")),
        name="moe",
    )(x, gain, sc, sh, g2, wrt, rb, wg, wu, wd, fin)


def kernel(x, c, positions, w_mod, b_mod, norm_mix, norm_ffn, w_in, nsa_phi_w1, nsa_phi_w2, nsa_phi_pos, rwkv_mu, rwkv_w_up, rwkv_w0, rwkv_a_up, rwkv_a0, rwkv_g_up, rwkv_k_k, rwkv_k_a, rwkv_r_k, rwkv_ln_w, rwkv_ln_b, norm_nsa_out, norm_moba_out, w_out, w_router, router_bias, moe_w_gate, moe_w_up, moe_w_down, norm_final):
    B, S, D = x.shape
    depth = w_in.shape[0]
    assert S % NSA_TK == 0 and S % MOBA_BLOCK == 0 and S % RWKV_TB == 0 and S >= WINDOW + NSA_QB
    assert S // SLC_BLOCK <= LANES and S // MOBA_BLOCK <= MOBA_NBP

    inv = ROPE_THETA ** (-jnp.arange(0, HEAD_DIM, 2, dtype=F32) / HEAD_DIM)
    ang = positions.astype(F32)[..., None] * inv
    cos, sin = jnp.cos(ang), jnp.sin(ang)
    one, zero = jnp.ones((B, S, HEAD_DIM), F32), jnp.zeros((B, S, HEAD_DIM), F32)
    cos_t = jnp.concatenate([cos, cos, one], axis=-1)
    sin_s = jnp.concatenate([-sin, sin, zero], axis=-1)
    nc = S // CMP_STRIDE
    n_cmp = (S - CMP_BLOCK) // CMP_STRIDE + 1
    pad_c = lambda t, fill: jnp.concatenate([t[:, CMP_BLOCK - 1::CMP_STRIDE][:, :n_cmp],
                                             jnp.full((B, nc - n_cmp, LANES), fill, F32)], axis=1)
    cos_c, sin_c = pad_c(cos_t, 1.0), pad_c(sin_s, 0.0)

    nsa_consts = _nsa_constants(S)
    bd, tri = _rwkv_constants()
    mod = _modulation(c, w_mod, b_mod)
    wrt = w_router.T
    rb = router_bias.reshape(N_EXPERTS, 1)
    fin = norm_final.reshape(1, D)

    for l in range(depth):
        sh1, sc1, g1, sh2, sc2, g2 = (mod[l, :B, i * D:(i + 1) * D].reshape(B, 1, D) for i in range(6))
        p, r, km, vs_t, vw_t, mv_t = _inproj(x, norm_mix[l].reshape(1, D), sc1, sh1, cos_t, sin_s, _prep_w_in(w_in[l]))

        w1, w2, pos4 = _prep_compress(nsa_phi_w1[l], nsa_phi_w2[l], nsa_phi_pos[l])
        chunks = lambda g: p[:, :, g * LANES:(g + 1) * LANES].reshape(B, nc, CMP_STRIDE * LANES)
        kvcmp, kvcmp_t = _compress(chunks(G_KC), chunks(G_VC), pos4, w1, w2, cos_c, sin_c)
        mixn = _nsa(p, r, kvcmp, kvcmp_t, vs_t, vw_t, *nsa_consts)

        kmean = km[:, :, 0, :]
        kmean = jnp.concatenate([kmean, jnp.zeros((B, MOBA_NBP - kmean.shape[1], kmean.shape[2]), F32)], axis=1)
        mixm = _moba(p, mv_t, kmean)

        zl = jnp.zeros((DECAY_LORA, RWKV_WIDTH), F32)
        wup = jnp.concatenate([rwkv_w_up[l], zl], axis=0)
        aup = jnp.concatenate([zl, rwkv_a_up[l]], axis=0)
        vecs = jnp.stack([rwkv_w0[l], rwkv_a0[l], rwkv_k_k[l], rwkv_k_a[l], rwkv_r_k[l].reshape(-1),
                          rwkv_ln_w[l], rwkv_ln_b[l], jnp.zeros((RWKV_WIDTH,), F32)])
        orw = _rwkv(r, rwkv_mu[l].reshape(1, -1), wup, aup, rwkv_g_up[l], vecs, bd, tri)

        wo = w_out[l]
        x = _outproj(mixn, mixm, orw, x, g1,
                     _gain_and_mask(norm_nsa_out[l], _pad_nsa),
                     _gain_and_mask(norm_moba_out[l], _pad_moba),
                     _pad_nsa(wo[:NSA_WIDTH]).astype(BF16),
                     _pad_moba(wo[NSA_WIDTH:NSA_WIDTH + MOBA_WIDTH]).astype(BF16),
                     wo[NSA_WIDTH + MOBA_WIDTH:].astype(BF16))
        x = _moe(x, norm_ffn[l].reshape(1, D), sc2, sh2, g2, wrt, rb,
                 moe_w_gate[l].astype(BF16), moe_w_up[l].astype(BF16), moe_w_down[l].astype(BF16),
                 fin, final=(l == depth - 1))
    return x
```

```python
import functools

import jax
import jax.numpy as jnp
import numpy as np
from jax import lax
from jax.experimental import pallas as pl
from jax.experimental.pallas import tpu as pltpu

F32 = jnp.float32
BF16 = jnp.bfloat16

HEAD_DIM = 64
LANES = 128
ROPE_THETA = 10000.0
NORM_EPS = 1e-6
NEG = -1e30
LOG2_E = 1.4426950408889634

NSA_HEADS = 6
NSA_KV_HEADS = 2
NSA_GROUP = NSA_HEADS // NSA_KV_HEADS
NSA_WIDTH = NSA_HEADS * HEAD_DIM
NSA_KV_WIDTH = NSA_KV_HEADS * HEAD_DIM
CMP_BLOCK = 32
CMP_STRIDE = 16
SLC_BLOCK = 64
SLC_TOPN = 16
WINDOW = 512
NSA_QB = 256
NSA_TK = 256
NSA_OUT_LANES = 256
NSA_WIN_SPAN = WINDOW + NSA_QB
ATTEND_UNROLL = 4

MOBA_HEADS = 4
MOBA_WIDTH = MOBA_HEADS * HEAD_DIM
MOBA_BLOCK = 256
MOBA_TOPK = 3
MOBA_NBP = 32

RWKV_HEADS = 6
RWKV_WIDTH = RWKV_HEADS * HEAD_DIM
DECAY_LORA = 64
AAA_LORA = 64
GATE_LORA = 128
RWKV_COLS = 3 * RWKV_WIDTH + DECAY_LORA + AAA_LORA + GATE_LORA
DECAY_SCALE = 0.606531
GN_EPS = 64e-5
RWKV_CHUNK = 64
RWKV_TB = 256

N_EXPERTS = 16
N_EXPERT_GROUPS = 4
EXPERTS_PER_GROUP = N_EXPERTS // N_EXPERT_GROUPS
EXPERT_FF = 256
MOE_EXPERTS_PER_STEP = 4

G_NSA_Q = 0
G_SLC = 6
G_WIN = 8
G_KC = 10
G_VC = 11
G_MOBA_Q = 12
G_MOBA_KV = 16
N_GROUPS = 20
ROPED_GROUPS = tuple(range(0, 10)) + tuple(range(12, 20))
P_COLS = N_GROUPS * LANES
R_GATE_BLOCK = RWKV_COLS // LANES
R_COLS = RWKV_COLS + NSA_KV_HEADS * LANES

VMEM_LIMIT = 56 * 1024 * 1024


def _cparams(sem):
    return pltpu.CompilerParams(dimension_semantics=sem, vmem_limit_bytes=VMEM_LIMIT)


def _dot(a, b):
    return jnp.dot(a, b, preferred_element_type=F32)


def _dot_nt(a, b):
    return lax.dot_general(a, b, (((1,), (1,)), ((), ())), preferred_element_type=F32)


def _dot_tn(a, b):
    return lax.dot_general(a, b, (((0,), (0,)), ((), ())), preferred_element_type=F32)


def _split2(x):
    hi = x.astype(BF16)
    lo = (x - hi.astype(F32)).astype(BF16)
    return hi, lo


def _split3(x):
    hi = x.astype(BF16)
    r1 = x - hi.astype(F32)
    mid = r1.astype(BF16)
    lo = (r1 - mid.astype(F32)).astype(BF16)
    return hi, mid, lo


def _mm3(a, b, dot=_dot):
    ah, al = _split2(a)
    bh, bl = _split2(b)
    return dot(ah, bh) + dot(ah, bl) + dot(al, bh)


def _mm1(a, b, dot=_dot):
    return dot(a.astype(BF16), b.astype(BF16))


def _mm3_exact_rhs(a, b_bf16):
    hi, mid, lo = _split3(a)
    return _dot(hi, b_bf16) + _dot(mid, b_bf16) + _dot(lo, b_bf16)


def _rope(y, cos_t, sin_s):
    lane = lax.broadcasted_iota(jnp.int32, y.shape, 1)
    rot = jnp.where(lane < HEAD_DIM // 2, pltpu.roll(y, LANES - HEAD_DIM // 2, 1), pltpu.roll(y, HEAD_DIM // 2, 1))
    return y * cos_t + rot * sin_s


def _mod_kernel(c_ref, w_ref, b_ref, o_ref):
    c = c_ref[...]
    ca = c * jax.nn.sigmoid(c)
    o_ref[0] = _dot(ca.astype(BF16), w_ref[0].astype(BF16)) + b_ref[0]


def _modulation(c, w_mod, b_mod):
    B, D = c.shape
    L, _, N = w_mod.shape
    tn = 512
    c8 = jnp.zeros((8, D), F32).at[:B].set(c)
    return pl.pallas_call(
        _mod_kernel,
        grid=(L, N // tn),
        in_specs=[
            pl.BlockSpec((8, D), lambda l, j: (0, 0)),
            pl.BlockSpec((1, D, tn), lambda l, j: (l, 0, j)),
            pl.BlockSpec((1, 1, tn), lambda l, j: (l, 0, j)),
        ],
        out_specs=pl.BlockSpec((1, 8, tn), lambda l, j: (l, 0, j)),
        out_shape=jax.ShapeDtypeStruct((L, 8, N), F32),
        compiler_params=_cparams(("arbitrary", "arbitrary")),
        name="modulation",
    )(c8, w_mod, b_mod.reshape(L, 1, N))


def _ones_over_values_t(y):
    yt = y.T
    row = lax.broadcasted_iota(jnp.int32, yt.shape, 0)
    return jnp.where(row < HEAD_DIM, 1.0, yt).astype(BF16)


def _inproj_kernel(x_ref, gain_ref, sc_ref, sh_ref, cos_ref, sin_ref, w_ref, p_ref, r_ref, km_ref,
                   vs_ref, vw_ref, mv_ref):
    x = x_ref[0]
    ms = jnp.mean(x * x, axis=-1, keepdims=True)
    h = x * lax.rsqrt(ms + NORM_EPS) * gain_ref[...]
    h = h * (1.0 + sc_ref[0]) + sh_ref[0]
    hb = h.astype(BF16)
    cos_t = cos_ref[0]
    sin_s = sin_ref[0]
    per = 4
    for g0 in range(0, N_GROUPS, per):
        y4 = _dot(hb, w_ref[:, g0 * LANES:(g0 + per) * LANES])
        for j in range(per):
            g = g0 + j
            y = y4[:, j * LANES:(j + 1) * LANES]
            if g in ROPED_GROUPS:
                y = _rope(y, cos_t, sin_s)
            if g >= G_MOBA_KV:
                km = jnp.mean(y, axis=0, keepdims=True)
                km_ref[0, 0, :, (g - G_MOBA_KV) * LANES:(g - G_MOBA_KV + 1) * LANES] = jnp.broadcast_to(km, (8, LANES))
                mv_ref[0, g - G_MOBA_KV, 0] = _ones_over_values_t(y)
            elif G_SLC <= g < G_WIN:
                vs_ref[0, g - G_SLC, 0] = _ones_over_values_t(y)
            elif G_WIN <= g < G_KC:
                yt = _ones_over_values_t(y)
                for t in range(y.shape[0] // NSA_QB):
                    vw_ref[0, g - G_WIN, t] = yt[:, t * NSA_QB:(t + 1) * NSA_QB]
            p_ref[0, :, g * LANES:(g + 1) * LANES] = y.astype(BF16)
    r_ref[0] = _dot(hb, w_ref[:, P_COLS:])


def _inproj(x, gain, sc, sh, cos_t, sin_s, w):
    B, S, D = x.shape
    tm = MOBA_BLOCK
    return pl.pallas_call(
        _inproj_kernel,
        grid=(B, S // tm),
        in_specs=[
            pl.BlockSpec((1, tm, D), lambda b, i: (b, i, 0)),
            pl.BlockSpec((1, D), lambda b, i: (0, 0)),
            pl.BlockSpec((1, 1, D), lambda b, i: (b, 0, 0)),
            pl.BlockSpec((1, 1, D), lambda b, i: (b, 0, 0)),
            pl.BlockSpec((1, tm, LANES), lambda b, i: (b, i, 0)),
            pl.BlockSpec((1, tm, LANES), lambda b, i: (b, i, 0)),
            pl.BlockSpec((D, P_COLS + R_COLS), lambda b, i: (0, 0)),
        ],
        out_specs=[
            pl.BlockSpec((1, tm, P_COLS), lambda b, i: (b, i, 0)),
            pl.BlockSpec((1, tm, R_COLS), lambda b, i: (b, i, 0)),
            pl.BlockSpec((1, 1, 8, MOBA_HEADS * LANES), lambda b, i: (b, i, 0, 0)),
            pl.BlockSpec((1, NSA_KV_HEADS, 1, LANES, tm), lambda b, i: (b, 0, i, 0, 0)),
            pl.BlockSpec((1, NSA_KV_HEADS, tm // NSA_QB, LANES, NSA_QB), lambda b, i: (b, 0, i, 0, 0)),
            pl.BlockSpec((1, MOBA_HEADS, 1, LANES, tm), lambda b, i: (b, 0, i, 0, 0)),
        ],
        out_shape=[
            jax.ShapeDtypeStruct((B, S, P_COLS), BF16),
            jax.ShapeDtypeStruct((B, S, R_COLS), F32),
            jax.ShapeDtypeStruct((B, S // tm, 8, MOBA_HEADS * LANES), F32),
            jax.ShapeDtypeStruct((B, NSA_KV_HEADS, S // tm, LANES, tm), BF16),
            jax.ShapeDtypeStruct((B, NSA_KV_HEADS, S // NSA_QB, LANES, NSA_QB), BF16),
            jax.ShapeDtypeStruct((B, MOBA_HEADS, S // tm, LANES, tm), BF16),
        ],
        compiler_params=_cparams(("arbitrary", "arbitrary")),
        name="inproj",
    )(x, gain, sc, sh, cos_t, sin_s, w)


def _prep_w_in(w):
    D = w.shape[0]
    o = 0
    parts = {}
    for name, width in (("nq", NSA_WIDTH), ("nkc", NSA_KV_WIDTH), ("nvc", NSA_KV_WIDTH), ("nks", NSA_KV_WIDTH),
                        ("nvs", NSA_KV_WIDTH), ("nkw", NSA_KV_WIDTH), ("nvw", NSA_KV_WIDTH), ("ngate", NSA_HEADS * 3),
                        ("mq", MOBA_WIDTH), ("mk", MOBA_WIDTH), ("mv", MOBA_WIDTH), ("rf", RWKV_COLS)):
        parts[name] = w[:, o:o + width]
        o += width
    hd = lambda t, h: t[:, h * HEAD_DIM:(h + 1) * HEAD_DIM]
    z = jnp.zeros((D, HEAD_DIM), F32)
    scale = HEAD_DIM ** -0.5 * LOG2_E
    cols = []
    for h in range(NSA_HEADS):
        cols += [hd(parts["nq"], h) * scale, z]
    for h in range(NSA_KV_HEADS):
        cols += [hd(parts["nks"], h), hd(parts["nvs"], h)]
    for h in range(NSA_KV_HEADS):
        cols += [hd(parts["nkw"], h), hd(parts["nvw"], h)]
    cols += [parts["nkc"], parts["nvc"]]
    for h in range(MOBA_HEADS):
        cols += [hd(parts["mq"], h) * scale, z]
    for h in range(MOBA_HEADS):
        cols += [hd(parts["mk"], h), hd(parts["mv"], h)]
    cols += [parts["rf"]]
    per = NSA_GROUP * 3
    for h in range(NSA_KV_HEADS):
        cols += [parts["ngate"][:, h * per:(h + 1) * per], jnp.zeros((D, LANES - per), F32)]
    return jnp.concatenate(cols, axis=1).astype(BF16)


def _cmp_kernel(kc_ref, vc_ref, pos_ref, w1_ref, w2_ref, cos_ref, sin_ref, o_ref, ot_ref):
    nc = kc_ref.shape[1]

    def hidden(t_ref, ia, ib):
        t = t_ref[0].astype(F32)
        a = _dot((t + pos_ref[ia:ia + 1, :]).astype(BF16), w1_ref[ia])
        b = _dot((t + pos_ref[ib:ib + 1, :]).astype(BF16), w1_ref[ib])
        return jax.nn.gelu(a + pltpu.roll(b, nc - 1, 0)).astype(BF16)

    gk = hidden(kc_ref, 0, 1)
    gv = hidden(vc_ref, 2, 3)
    for h in range(NSA_KV_HEADS):
        y = _dot(gk, w2_ref[2 * h]) + _dot(gv, w2_ref[2 * h + 1])
        y = _rope(y, cos_ref[0], sin_ref[0])
        o_ref[0, h] = y.astype(BF16)
        ot_ref[0, h] = _ones_over_values_t(y)


def _compress(kc16, vc16, pos4, w1, w2, cos_c, sin_c):
    B, nc, K = kc16.shape
    return pl.pallas_call(
        _cmp_kernel,
        grid=(B,),
        in_specs=[
            pl.BlockSpec((1, nc, K), lambda b: (b, 0, 0)),
            pl.BlockSpec((1, nc, K), lambda b: (b, 0, 0)),
            pl.BlockSpec((4, K), lambda b: (0, 0)),
            pl.BlockSpec((4, K, LANES), lambda b: (0, 0, 0)),
            pl.BlockSpec((4, LANES, LANES), lambda b: (0, 0, 0)),
            pl.BlockSpec((1, nc, LANES), lambda b: (b, 0, 0)),
            pl.BlockSpec((1, nc, LANES), lambda b: (b, 0, 0)),
        ],
        out_specs=[pl.BlockSpec((1, NSA_KV_HEADS, nc, LANES), lambda b: (b, 0, 0, 0)),
                   pl.BlockSpec((1, NSA_KV_HEADS, LANES, nc), lambda b: (b, 0, 0, 0))],
        out_shape=[jax.ShapeDtypeStruct((B, NSA_KV_HEADS, nc, LANES), BF16),
                   jax.ShapeDtypeStruct((B, NSA_KV_HEADS, LANES, nc), BF16)],
        compiler_params=_cparams(("arbitrary",)),
        name="nsa_compress",
    )(kc16, vc16, pos4, w1, w2, cos_c, sin_c)


def _prep_compress(phi_w1, phi_w2, phi_pos):
    half = CMP_BLOCK // 2
    eye = jnp.eye(NSA_KV_HEADS, dtype=F32)
    w1, pos = [], []
    for t in range(2):
        for part in range(2):
            w = phi_w1[t, part * half:(part + 1) * half]
            w1.append(jnp.einsum("lde,kK->lkdKe", w, eye).reshape(half * NSA_KV_WIDTH, NSA_KV_WIDTH))
            p = phi_pos[t, part * half:(part + 1) * half]
            pos.append(jnp.broadcast_to(p[:, None, :], (half, NSA_KV_HEADS, HEAD_DIM)).reshape(-1))
    w2 = []
    for h in range(NSA_KV_HEADS):
        for t in range(2):
            m = jnp.zeros((LANES, LANES), F32)
            m = m.at[h * HEAD_DIM:(h + 1) * HEAD_DIM, t * HEAD_DIM:(t + 1) * HEAD_DIM].set(phi_w2[t])
            w2.append(m)
    return jnp.stack(w1).astype(BF16), jnp.stack(w2).astype(BF16), jnp.stack(pos)


def _attend_tiles(n_loop, scores, mask_last, values_t, scratch, prefetch_all):
    s_a, s_b, s_c, s_d, p_scr, alpha_scr, m_scr, acc_scr = scratch
    m_scr[...] = jnp.full(m_scr.shape, -jnp.inf, F32)
    alpha_scr[...] = jnp.ones(alpha_scr.shape, F32)
    acc_scr[...] = jnp.zeros(acc_scr.shape, F32)
    p_scr[...] = jnp.zeros(p_scr.shape, BF16)

    def apply_weights(j):
        acc_scr[...] = alpha_scr[0:1, :] * acc_scr[...] + _dot(values_t(j), p_scr[...])

    def softmax_tile(s_t):
        m_prev = m_scr[0:1, :]
        m_new = jnp.maximum(m_prev, jnp.max(s_t, axis=0, keepdims=True))
        alpha_scr[0:1, :] = jnp.exp2(m_prev - m_new)
        p_scr[...] = jnp.exp2(s_t - m_new).astype(BF16)
        m_scr[0:1, :] = m_new

    def step(j, cur, nxt):
        if nxt is not cur:
            nxt[...] = scores(j + 1)
        apply_weights(jnp.maximum(j - 1, 0))
        softmax_tile(cur[...])
        if nxt is cur:
            nxt[...] = scores(j + 1)

    s_a[...] = scores(0)

    def quad(i, carry):
        j0 = ATTEND_UNROLL * i
        if not prefetch_all:
            for u in range(ATTEND_UNROLL):
                step(j0 + u, (s_a, s_b)[u % 2], (s_b, s_a)[u % 2])
            return carry
        s_b[...] = scores(j0 + 1)
        s_c[...] = scores(j0 + 2)
        s_d[...] = scores(j0 + 3)
        apply_weights(jnp.maximum(j0 - 1, 0))
        softmax_tile(s_a[...])
        s_a[...] = scores(j0 + 4)
        for u, buf in enumerate((s_b, s_c, s_d)):
            apply_weights(j0 + u)
            softmax_tile(buf[...])
        return carry

    lax.fori_loop(0, n_loop // ATTEND_UNROLL, quad, 0)
    done = n_loop - n_loop % ATTEND_UNROLL

    @pl.when(n_loop % ATTEND_UNROLL >= 2)
    def _():
        step(done, s_a, s_b)
        step(done + 1, s_b, s_a)

    @pl.when(n_loop % 2 == 1)
    def _():
        step(n_loop - 1, s_a, s_a)

    apply_weights(jnp.maximum(n_loop - 1, 0))
    softmax_tile(mask_last(s_a[...]))
    apply_weights(n_loop)
    return acc_scr[...]


def _attend_scratch(tk, nq):
    return [pltpu.VMEM((tk, nq), F32) for _ in range(ATTEND_UNROLL)] + [
        pltpu.VMEM((tk, nq), BF16), pltpu.VMEM((8, nq), F32), pltpu.VMEM((8, nq), F32), pltpu.VMEM((LANES, nq), F32)]


def _nsa_kernel(q0_ref, q1_ref, q2_ref, kvc_ref, kvct_ref, kvs_ref, vst_ref, kvw_ref, vwt_ref, gate_ref, mt_ref,
                et_ref, rel_ref, crel_ref, o_ref, *attend_scratch, n_pick):
    ci = pl.program_id(2)
    qb = NSA_QB
    nq = NSA_GROUP * qb
    q = jnp.concatenate([q0_ref[0], q1_ref[0], q2_ref[0]], axis=0)

    kc = kvc_ref[0, 0]
    sm = jnp.where(crel_ref[...] <= ci * qb, _dot_nt(kc, q), NEG)
    mx = jnp.max(sm, axis=0, keepdims=True)
    e = jnp.exp2(sm - mx)
    den = jnp.maximum(jnp.sum(e, axis=0, keepdims=True), 1e-30)
    p_c = e * jnp.where(mx > 0.5 * NEG, 1.0 / den, 0.0)
    o_c = _dot(kvct_ref[0, 0], p_c.astype(BF16))

    hi, mid, lo = _split3(p_c[:, 0:qb] + p_c[:, qb:2 * qb] + p_c[:, 2 * qb:3 * qb])
    mt = mt_ref[...]
    imp_t = _dot(mt, hi) + _dot(mt, mid) + _dot(mt, lo)
    blk = lax.broadcasted_iota(jnp.int32, imp_t.shape, 0)
    cur = (ci * qb + lax.broadcasted_iota(jnp.int32, imp_t.shape, 1)) // SLC_BLOCK
    forced = (blk == 0) | (blk == cur) | (blk == cur - 1)
    free = (blk <= cur) & jnp.logical_not(forced)
    sc = jnp.where(free, imp_t, -1.0)
    for _ in range(n_pick):
        best = jnp.max(sc, axis=0, keepdims=True)
        idx = jnp.min(jnp.where(sc == best, blk, imp_t.shape[0]), axis=0, keepdims=True)
        sc = jnp.where(blk == idx, -2.0, sc)
    bias = jnp.where(free, jnp.where(sc < -1.5, 0.0, NEG), jnp.where(forced, 0.0, NEG)).T.astype(BF16)
    w_nt = jnp.concatenate([q, jnp.concatenate([bias] * NSA_GROUP, axis=0)], axis=1)
    o_w = _nsa_window(ci, q, kvw_ref, vwt_ref, rel_ref)

    tk = NSA_TK
    last = (ci * qb) // tk

    def scores(j):
        kv = kvs_ref[0, pl.ds(pl.multiple_of(j * tk, tk), tk), :]
        return _dot_nt(jnp.concatenate([kv, et_ref[j]], axis=1), w_nt)

    def causal_edge(s_t):
        return jnp.where(rel_ref[0:tk, :] <= ci * qb - last * tk, s_t, NEG)

    acc = _attend_tiles(last, scores, causal_edge, lambda j: vst_ref[0, 0, j], attend_scratch, prefetch_all=False)
    o_s = acc[HEAD_DIM:] * (1.0 / acc[0:1])

    gt = jax.nn.sigmoid(gate_ref[0]).T
    outs = []
    for g in range(NSA_GROUP):
        ls = slice(g * qb, (g + 1) * qb)
        outs.append(gt[3 * g:3 * g + 1] * o_c[HEAD_DIM:, ls] + gt[3 * g + 1:3 * g + 2] * o_s[:, ls]
                    + gt[3 * g + 2:3 * g + 3] * o_w[:, ls])
    outs.append(jnp.zeros((NSA_OUT_LANES - NSA_GROUP * HEAD_DIM, qb), F32))
    o_ref[0] = jnp.concatenate(outs, axis=0).T


def _nsa_window(ci, q, kvw_ref, vwt_ref, rel_ref):
    qb = NSA_QB
    n_wt = NSA_WIN_SPAN // qb
    first = jnp.maximum(ci - WINDOW // qb, 0)
    kvw = kvw_ref[0, pl.ds(pl.multiple_of(first * qb, qb), NSA_WIN_SPAN), :]
    s_t = _dot_nt(kvw, q)
    dist = (ci - first) * qb - rel_ref[...]
    sm = jnp.where(lax.bitcast_convert_type(dist, jnp.uint32) < WINDOW, s_t, NEG)
    e_w = jnp.exp2(sm - jnp.max(sm, axis=0, keepdims=True))
    vw_t = jnp.concatenate([vwt_ref[0, 0, first + t] for t in range(n_wt)], axis=1)
    acc_w = _dot(vw_t, e_w.astype(BF16))
    return acc_w[HEAD_DIM:] * (1.0 / acc_w[0:1])


def _nsa(p, r, kvcmp, kvcmp_t, vs_t, vw_t, cmp_to_slc_t, e3, rel, crel):
    B, S, _ = p.shape
    nc = kvcmp.shape[2]
    qb = NSA_QB
    n_top = min(SLC_TOPN, S // SLC_BLOCK)
    qspec = lambda g: pl.BlockSpec((1, qb, LANES), lambda b, h, i: (b, i, G_NSA_Q + NSA_GROUP * h + g))
    nq = NSA_GROUP * qb
    return pl.pallas_call(
        functools.partial(_nsa_kernel, n_pick=max(n_top - 3, 0)),
        grid=(B, NSA_KV_HEADS, S // qb),
        in_specs=[
            qspec(0), qspec(1), qspec(2),
            pl.BlockSpec((1, 1, nc, LANES), lambda b, h, i: (b, h, 0, 0)),
            pl.BlockSpec((1, 1, LANES, nc), lambda b, h, i: (b, h, 0, 0)),
            pl.BlockSpec((1, S, LANES), lambda b, h, i: (b, 0, G_SLC + h)),
            pl.BlockSpec((1, 1) + vs_t.shape[2:], lambda b, h, i: (b, h, 0, 0, 0)),
            pl.BlockSpec((1, S, LANES), lambda b, h, i: (b, 0, G_WIN + h)),
            pl.BlockSpec((1, 1) + vw_t.shape[2:], lambda b, h, i: (b, h, 0, 0, 0)),
            pl.BlockSpec((1, qb, LANES), lambda b, h, i: (b, i, R_GATE_BLOCK + h)),
            pl.BlockSpec(cmp_to_slc_t.shape, lambda b, h, i: (0, 0)),
            pl.BlockSpec(e3.shape, lambda b, h, i: (0, 0, 0)),
            pl.BlockSpec(rel.shape, lambda b, h, i: (0, 0)),
            pl.BlockSpec(crel.shape, lambda b, h, i: (0, 0)),
        ],
        out_specs=pl.BlockSpec((1, qb, NSA_OUT_LANES), lambda b, h, i: (b, i, h)),
        out_shape=jax.ShapeDtypeStruct((B, S, NSA_KV_HEADS * NSA_OUT_LANES), F32),
        scratch_shapes=_attend_scratch(NSA_TK, nq),
        compiler_params=_cparams(("arbitrary", "arbitrary", "arbitrary")),
        name="nsa_attention",
    )(p, p, p, kvcmp, kvcmp_t, p, vs_t, p, vw_t, r, cmp_to_slc_t, e3, rel, crel)


def _nsa_constants(S):
    n_cmp = (S - CMP_BLOCK) // CMP_STRIDE + 1
    nc = S // CMP_STRIDE
    n_slc = S // SLC_BLOCK
    c_start = np.arange(nc) * CMP_STRIDE
    s_start = np.arange(LANES) * SLC_BLOCK
    overlap = (np.minimum(c_start[None, :] + CMP_BLOCK, s_start[:, None] + SLC_BLOCK)
               - np.maximum(c_start[None, :], s_start[:, None]))
    m_t = np.clip(overlap, 0, None).astype(np.float32) / CMP_BLOCK
    m_t = np.where((np.arange(nc)[None, :] < n_cmp) & (np.arange(LANES)[:, None] < n_slc), m_t, 0.0)
    key_blk = (np.arange(S) // SLC_BLOCK).reshape(S // NSA_TK, NSA_TK, 1)
    e3 = (key_blk == np.arange(LANES)[None, None, :]).astype(np.float32)
    lane_q = np.arange(NSA_GROUP * NSA_QB) % NSA_QB
    rel = np.arange(NSA_WIN_SPAN)[:, None] - lane_q[None, :]
    crel = (np.arange(nc) * CMP_STRIDE + CMP_BLOCK - 1)[:, None] - lane_q[None, :]
    return jnp.asarray(m_t, BF16), jnp.asarray(e3, BF16), jnp.asarray(rel, jnp.int32), jnp.asarray(crel, jnp.int32)


def _moba_kernel(q_ref, kv_ref, vt_ref, km_ref, o_ref, sel_scr, *attend_scratch):
    ci = pl.program_id(2)
    qb = MOBA_BLOCK
    q = q_ref[0]
    km = km_ref[0]
    km_hi, km_lo = _split2(km)
    gate_t = _dot_nt(km_hi, q) + _dot_nt(km_lo, q)
    blk = lax.broadcasted_iota(jnp.int32, gate_t.shape, 0)
    valid = blk < ci
    sc = jnp.where(valid, gate_t, -jnp.inf)
    picked = jnp.zeros(gate_t.shape, F32)
    for _ in range(MOBA_TOPK):
        best = jnp.max(sc, axis=0, keepdims=True)
        idx = jnp.min(jnp.where(sc == best, blk, gate_t.shape[0]), axis=0, keepdims=True)
        pick = blk == idx
        picked = jnp.where(pick, 1.0, picked)
        sc = jnp.where(pick, -jnp.inf, sc)
    sel = jnp.where(valid, picked, jnp.where(blk == ci, 1.0, 0.0))
    sel_scr[...] = jnp.where(sel > 0.5, 0.0, NEG)

    def scores(j):
        kvj = kv_ref[0, pl.ds(pl.multiple_of(j * qb, qb), qb), :]
        return _dot_nt(kvj, q) + sel_scr[pl.ds(j, 1), :]

    def causal_edge(s_t):
        causal = lax.broadcasted_iota(jnp.int32, s_t.shape, 0) <= lax.broadcasted_iota(jnp.int32, s_t.shape, 1)
        return jnp.where(causal, s_t, NEG)

    acc = _attend_tiles(ci, scores, causal_edge, lambda j: vt_ref[0, 0, j], attend_scratch, prefetch_all=True)
    o_ref[0] = (acc * (1.0 / acc[0:1])).T


def _moba(p, mv_t, kmean):
    B, S, _ = p.shape
    qb = MOBA_BLOCK
    return pl.pallas_call(
        _moba_kernel,
        grid=(B, MOBA_HEADS, S // qb),
        in_specs=[
            pl.BlockSpec((1, qb, LANES), lambda b, h, i: (b, i, G_MOBA_Q + h)),
            pl.BlockSpec((1, S, LANES), lambda b, h, i: (b, 0, G_MOBA_KV + h)),
            pl.BlockSpec((1, 1) + mv_t.shape[2:], lambda b, h, i: (b, h, 0, 0, 0)),
            pl.BlockSpec((1, MOBA_NBP, LANES), lambda b, h, i: (b, 0, h)),
        ],
        out_specs=pl.BlockSpec((1, qb, LANES), lambda b, h, i: (b, i, h)),
        out_shape=jax.ShapeDtypeStruct((B, S, MOBA_HEADS * LANES), F32),
        scratch_shapes=[pltpu.VMEM((MOBA_NBP, qb), F32)] + _attend_scratch(qb, qb),
        compiler_params=_cparams(("arbitrary", "arbitrary", "arbitrary")),
        name="moba_attention",
    )(p, p, mv_t, kmean)


def _rwkv_kernel(f_ref, mu_ref, wup_ref, aup_ref, gup_ref, vec_ref, bd_ref, tri_ref, o_ref, carry_scr, st_scr):
    i = pl.program_id(1)
    tb = f_ref.shape[1]
    C = RWKV_CHUNK
    W = RWKV_WIDTH

    @pl.when(i == 0)
    def _():
        carry_scr[...] = jnp.zeros(carry_scr.shape, F32)
        st_scr[...] = jnp.zeros(st_scr.shape, F32)

    feat = f_ref[0]
    rowi = lax.broadcasted_iota(jnp.int32, feat.shape, 0)
    prev = jnp.where(rowi == 0, carry_scr[0:1, :], pltpu.roll(feat, 1, 0))
    carry_scr[0:1, :] = feat[tb - 1:tb, :]
    xs = feat + (prev - feat) * mu_ref[...]
    r = xs[:, 0:W]
    k = xs[:, W:2 * W]
    v = xs[:, 2 * W:3 * W]
    wa = xs[:, 3 * W:3 * W + DECAY_LORA + AAA_LORA]
    gd = xs[:, 3 * W + DECAY_LORA + AAA_LORA:]
    w0, a0, k_k, k_a, r_k, ln_w, ln_b = (vec_ref[n:n + 1, :] for n in range(7))
    bd = bd_ref[...]

    def hsum(t):
        hi, lo = _split2(t)
        return _dot(hi, bd) + _dot(lo, bd)

    logw = -DECAY_SCALE * jax.nn.sigmoid(w0 + _mm3(jnp.tanh(wa), wup_ref[...]))
    a = jax.nn.sigmoid(a0 + _mm1(wa, aup_ref[...]))
    gate = _mm1(jax.nn.sigmoid(gd), gup_ref[...])
    kk = k * k_k
    kk = kk / jnp.maximum(jnp.sqrt(hsum(kk * kk)), 1e-12)
    k = k * (1.0 + (a - 1.0) * k_a)
    bonus = hsum(r * k * r_k) * v
    kka = kk * a

    ri = lax.broadcasted_iota(jnp.int32, (tb, tb), 0)
    cj = lax.broadcasted_iota(jnp.int32, (tb, tb), 1)
    same = (ri // C) == (cj // C)
    strict = same & (cj < ri)
    incl = same & (cj <= ri)
    eye = jnp.where(ri == cj, 1.0, 0.0)
    eye_c = eye[0:C, 0:C]
    hi, mid, lo = _split3(logw)
    tri = tri_ref[0]
    blk = tri_ref[1]
    cum = _dot(tri, hi) + _dot(tri, mid) + _dot(tri, lo)
    tot = _dot(blk, hi) + _dot(blk, mid) + _dot(blk, lo)
    e_neg = jnp.exp(-cum)
    e_end = jnp.exp(tot - cum)
    d_end = jnp.exp(tot)
    a_t = -kk * jnp.exp(cum - logw)
    r_t = r * jnp.exp(cum)
    b_t = kka * e_neg
    k_t = k * e_neg
    b_e = kka * e_end
    k_e = k * e_end
    heads = range(RWKV_HEADS)
    hs = [slice(h * HEAD_DIM, (h + 1) * HEAD_DIM) for h in heads]
    bf = lambda t: t.astype(BF16)
    ah = [bf(a_t[:, s]) for s in hs]
    rh = [r_t[:, s] for s in hs]
    vh = [bf(v[:, s]) for s in hs]
    ar = [jnp.concatenate([ah[h], bf(rh[h])], axis=0) for h in heads]
    xb = [_dot_nt(ar[h], bf(b_t[:, hs[h]])) for h in heads]
    xk = [_dot_nt(ar[h], bf(k_t[:, hs[h]])) for h in heads]
    n = [bf(jnp.where(strict, xb[h][0:tb], 0.0)) for h in heads]
    m_ak = [bf(jnp.where(strict, xk[h][0:tb], 0.0)) for h in heads]
    m_rb = [bf(jnp.where(incl, xb[h][tb:], 0.0)) for h in heads]
    m_rk = [bf(jnp.where(incl, xk[h][tb:], 0.0)) for h in heads]
    tinv = [eye + n[h].astype(F32) for h in heads]
    npow = n
    step = 1
    while 2 * step < C:
        npow = [bf(_dot(npow[h], npow[h])) for h in heads]
        tinv = [tinv[h] + _dot(bf(tinv[h]), npow[h]) for h in heads]
        step *= 2
    tinv = [bf(t) for t in tinv]
    g = [_dot(tinv[h], ah[h]) for h in heads]
    u0 = [_dot(tinv[h], bf(_dot(m_ak[h], vh[h]))) for h in heads]
    gb = [bf(t) for t in g]
    ub = [bf(t) for t in u0]
    r_y = [rh[h] + _dot(m_rb[h], gb[h]) for h in heads]
    y0 = [_dot(m_rb[h], ub[h]) + _dot(m_rk[h], vh[h]) for h in heads]
    beh = [bf(b_e[:, s]) for s in hs]
    keh = [bf(k_e[:, s]) for s in hs]
    st = [st_scr[h] for h in heads]
    yh = [[] for _ in heads]
    for c in range(tb // C):
        cs = slice(c * C, (c + 1) * C)
        p_st = [eye_c * d_end[c * C:c * C + 1, hs[h]] + _dot_tn(beh[h][cs], gb[h][cs]) for h in heads]
        q_st = [_dot_tn(beh[h][cs], ub[h][cs]) + _dot_tn(keh[h][cs], vh[h][cs]) for h in heads]
        for h in heads:
            yh[h].append(_mm3(r_y[h][cs], st[h]) + y0[h][cs])
        st = [_mm3(p_st[h], st[h]) + q_st[h] for h in heads]
    ys = []
    for h in heads:
        st_scr[h] = st[h]
        y = jnp.concatenate(yh[h], axis=0)
        mean = jnp.mean(y, axis=-1, keepdims=True)
        yc = y - mean
        var = jnp.mean(yc * yc, axis=-1, keepdims=True)
        ys.append(yc * lax.rsqrt(var + GN_EPS))
    yn = jnp.concatenate(ys, axis=1)
    o_ref[0] = (yn * ln_w + ln_b + bonus) * gate


def _rwkv(r, mu, wup, aup, gup, vecs, bd, tri):
    B, S, _ = r.shape
    tb = RWKV_TB
    full = lambda a: pl.BlockSpec(a.shape, lambda b, i: (0,) * a.ndim)
    return pl.pallas_call(
        _rwkv_kernel,
        grid=(B, S // tb),
        in_specs=[pl.BlockSpec((1, tb, RWKV_COLS), lambda b, i: (b, i, 0)),
                  full(mu), full(wup), full(aup), full(gup), full(vecs), full(bd), full(tri)],
        out_specs=pl.BlockSpec((1, tb, RWKV_WIDTH), lambda b, i: (b, i, 0)),
        out_shape=jax.ShapeDtypeStruct((B, S, RWKV_WIDTH), F32),
        scratch_shapes=[pltpu.VMEM((8, RWKV_COLS), F32), pltpu.VMEM((RWKV_HEADS, HEAD_DIM, HEAD_DIM), F32)],
        compiler_params=_cparams(("arbitrary", "arbitrary")),
        name="rwkv7",
    )(r, mu, wup, aup, gup, vecs, bd, tri)


def _rwkv_constants():
    head = np.arange(RWKV_WIDTH) // HEAD_DIM
    bd = (head[:, None] == head[None, :]).astype(np.float32)
    t = np.arange(RWKV_TB)
    same = (t[None, :] // RWKV_CHUNK) == (t[:, None] // RWKV_CHUNK)
    tri = np.stack([same & (t[None, :] <= t[:, None]), same]).astype(np.float32)
    return jnp.asarray(bd, BF16), jnp.asarray(tri, BF16)


def _outproj_kernel(mn_ref, mm_ref, rw_ref, x_ref, g1_ref, gn_ref, gm_ref, wn_ref, wm_ref, wr_ref, o_ref):
    def head_norm(o, gain_ref, width):
        o = jnp.where(gain_ref[1:2, :] > 0.5, o, 0.0)
        ms = jnp.sum(o * o, axis=-1, keepdims=True) * (1.0 / width)
        return (o * lax.rsqrt(ms + NORM_EPS) * gain_ref[0:1, :]).astype(BF16)

    z = _dot(head_norm(mn_ref[0], gn_ref, NSA_WIDTH), wn_ref[...])
    z = z + _dot(head_norm(mm_ref[0], gm_ref, MOBA_WIDTH), wm_ref[...])
    z = z + _dot(rw_ref[0].astype(BF16), wr_ref[...])
    o_ref[0] = x_ref[0] + g1_ref[0] * z


def _outproj(mixn, mixm, orw, x, g1, gn, gm, wn, wm, wr):
    B, S, D = x.shape
    tm = 512
    full = lambda a: pl.BlockSpec(a.shape, lambda b, i: (0,) * a.ndim)
    tok = lambda a: pl.BlockSpec((1, tm, a.shape[2]), lambda b, i: (b, i, 0))
    return pl.pallas_call(
        _outproj_kernel,
        grid=(B, S // tm),
        in_specs=[tok(mixn), tok(mixm), tok(orw), tok(x), pl.BlockSpec((1, 1, D), lambda b, i: (b, 0, 0)),
                  full(gn), full(gm), full(wn), full(wm), full(wr)],
        out_specs=tok(x),
        out_shape=jax.ShapeDtypeStruct((B, S, D), F32),
        compiler_params=_cparams(("arbitrary", "arbitrary")),
        name="outproj",
    )(mixn, mixm, orw, x, g1, gn, gm, wn, wm, wr)


def _pad_moba(t):
    t = t.reshape((MOBA_HEADS, HEAD_DIM) + t.shape[1:])
    return jnp.concatenate([jnp.zeros_like(t), t], axis=1).reshape((MOBA_HEADS * LANES,) + t.shape[2:])


def _pad_nsa(t):
    w = NSA_GROUP * HEAD_DIM
    t = t.reshape((NSA_KV_HEADS, w) + t.shape[1:])
    pad = jnp.zeros((NSA_KV_HEADS, NSA_OUT_LANES - w) + t.shape[2:], t.dtype)
    return jnp.concatenate([t, pad], axis=1).reshape((NSA_KV_HEADS * NSA_OUT_LANES,) + t.shape[2:])


def _gain_and_mask(gain, pad):
    return jnp.stack([pad(gain), pad(jnp.ones_like(gain))])


def _route(logit_t, bias_col):
    aff = jax.nn.sigmoid(logit_t)
    biased = aff + bias_col
    row = lambda t, e: t[e:e + 1, :]
    gp = EXPERTS_PER_GROUP
    scores = []
    for g in range(N_EXPERT_GROUPS):
        a_, b_, c_, d_ = (row(biased, g * gp + j) for j in range(gp))
        hi1, lo1, hi2, lo2 = jnp.maximum(a_, b_), jnp.minimum(a_, b_), jnp.maximum(c_, d_), jnp.minimum(c_, d_)
        scores.append(jnp.maximum(hi1, hi2) + jnp.maximum(jnp.minimum(hi1, hi2), jnp.maximum(lo1, lo2)))
    best = jnp.zeros_like(scores[0], dtype=jnp.int32)
    best_s = scores[0]
    for g in range(1, N_EXPERT_GROUPS):
        better = scores[g] > best_s
        best = jnp.where(better, g, best)
        best_s = jnp.where(better, scores[g], best_s)

    def in_group(t, j):
        out = row(t, j)
        for g in range(1, N_EXPERT_GROUPS):
            out = jnp.where(best == g, row(t, g * gp + j), out)
        return out

    vals = [in_group(biased, j) for j in range(gp)]
    affs = [in_group(aff, j) for j in range(gp)]

    def first_argmax(vs):
        top = functools.reduce(jnp.maximum, vs)
        idx = jnp.full(top.shape, gp, jnp.int32)
        for j in reversed(range(gp)):
            idx = jnp.where(vs[j] == top, j, idx)
        return idx

    i1 = first_argmax(vals)
    i2 = first_argmax([jnp.where(i1 == j, -jnp.inf, vals[j]) for j in range(gp)])
    pick = lambda idx: functools.reduce(lambda acc, j: jnp.where(idx == j, affs[j], acc), range(gp), jnp.zeros_like(affs[0]))
    w1, w2 = pick(i1), pick(i2)
    tot = w1 + w2
    e_iota = lax.broadcasted_iota(jnp.int32, logit_t.shape, 0)
    e1 = best * gp + i1
    e2 = best * gp + i2
    return jnp.where(e_iota == e1, w1 / tot, 0.0) + jnp.where(e_iota == e2, w2 / tot, 0.0)


def _moe_kernel(x_ref, gain_ref, sc_ref, sh_ref, g2_ref, wrt_ref, rb_ref, wg_ref, wu_ref, wd_ref, fin_ref,
                o_ref, hb_scr, cb_scr, acc_scr, *, final):
    e = pl.program_id(1)
    tm = x_ref.shape[1]

    @pl.when(e == 0)
    def _():
        x = x_ref[0]
        ms = jnp.mean(x * x, axis=-1, keepdims=True)
        h = x * lax.rsqrt(ms + NORM_EPS) * gain_ref[...]
        h = h * (1.0 + sc_ref[0]) + sh_ref[0]
        hb_scr[...] = h.astype(BF16)
        comb = _route(_mm3(wrt_ref[...], h, _dot_nt), rb_ref[...])
        cb_scr[...] = jnp.concatenate([comb, jnp.zeros((LANES - N_EXPERTS, tm), F32)], axis=0).T
        acc_scr[...] = jnp.zeros(acc_scr.shape, F32)

    hb = hb_scr[...]
    comb_t = cb_scr[...]
    hes = []
    for j in range(MOE_EXPERTS_PER_STEP):
        hg = _dot(hb, wg_ref[j])
        hu = _dot(hb, wu_ref[j])
        cbe = jnp.broadcast_to(comb_t[:, j:j + 1], (tm, LANES))
        he = hg * jax.nn.sigmoid(hg) * hu * jnp.concatenate([cbe] * (EXPERT_FF // LANES), axis=1)
        hes.append(he.astype(BF16))
    cb_scr[...] = pltpu.roll(comb_t, LANES - MOE_EXPERTS_PER_STEP, 1)
    acc_scr[...] += _dot(jnp.concatenate(hes, axis=1), wd_ref[...])

    @pl.when(e == N_EXPERTS // MOE_EXPERTS_PER_STEP - 1)
    def _():
        out = x_ref[0] + g2_ref[0] * acc_scr[...]
        if final:
            ms = jnp.mean(out * out, axis=-1, keepdims=True)
            out = out * lax.rsqrt(ms + NORM_EPS) * fin_ref[...]
        o_ref[0] = out


def _moe(x, gain, sc, sh, g2, wrt, rb, wg, wu, wd, fin, final):
    B, S, D = x.shape
    tm = 512
    tpb = S // tm
    eps = MOE_EXPERTS_PER_STEP
    wd = wd.reshape(N_EXPERTS * EXPERT_FF, D)
    tok = pl.BlockSpec((1, tm, D), lambda i, e: (i // tpb, i % tpb, 0))
    per_b = pl.BlockSpec((1, 1, D), lambda i, e: (i // tpb, 0, 0))
    full = lambda a: pl.BlockSpec(a.shape, lambda i, e: (0,) * a.ndim)
    return pl.pallas_call(
        functools.partial(_moe_kernel, final=final),
        grid=(B * tpb, N_EXPERTS // eps),
        in_specs=[tok, full(gain), per_b, per_b, per_b, full(wrt), full(rb),
                  pl.BlockSpec((eps, D, EXPERT_FF), lambda i, e: (e, 0, 0)),
                  pl.BlockSpec((eps, D, EXPERT_FF), lambda i, e: (e, 0, 0)),
                  pl.BlockSpec((eps * EXPERT_FF, D), lambda i, e: (e, 0)),
                  full(fin)],
        out_specs=tok,
        out_shape=jax.ShapeDtypeStruct((B, S, D), F32),
        scratch_shapes=[pltpu.VMEM((tm, D), BF16), pltpu.VMEM((tm, LANES), F32), pltpu.VMEM((tm, D), F32)],
        compiler_params=_cparams(("arbitrary", "arbitrary")),
        name="moe",
    )(x, gain, sc, sh, g2, wrt, rb, wg, wu, wd, fin)


def kernel(x, c, positions, w_mod, b_mod, norm_mix, norm_ffn, w_in, nsa_phi_w1, nsa_phi_w2, nsa_phi_pos, rwkv_mu, rwkv_w_up, rwkv_w0, rwkv_a_up, rwkv_a0, rwkv_g_up, rwkv_k_k, rwkv_k_a, rwkv_r_k, rwkv_ln_w, rwkv_ln_b, norm_nsa_out, norm_moba_out, w_out, w_router, router_bias, moe_w_gate, moe_w_up, moe_w_down, norm_final):
    B, S, D = x.shape
    depth = w_in.shape[0]
    assert S % NSA_TK == 0 and S % MOBA_BLOCK == 0 and S % RWKV_TB == 0 and S >= WINDOW + NSA_QB
    assert S // SLC_BLOCK <= LANES and S // MOBA_BLOCK <= MOBA_NBP

    inv = ROPE_THETA ** (-jnp.arange(0, HEAD_DIM, 2, dtype=F32) / HEAD_DIM)
    ang = positions.astype(F32)[..., None] * inv
    cos, sin = jnp.cos(ang), jnp.sin(ang)
    one, zero = jnp.ones((B, S, HEAD_DIM), F32), jnp.zeros((B, S, HEAD_DIM), F32)
    cos_t = jnp.concatenate([cos, cos, one], axis=-1)
    sin_s = jnp.concatenate([-sin, sin, zero], axis=-1)
    nc = S // CMP_STRIDE
    n_cmp = (S - CMP_BLOCK) // CMP_STRIDE + 1
    pad_c = lambda t, fill: jnp.concatenate([t[:, CMP_BLOCK - 1::CMP_STRIDE][:, :n_cmp],
                                             jnp.full((B, nc - n_cmp, LANES), fill, F32)], axis=1)
    cos_c, sin_c = pad_c(cos_t, 1.0), pad_c(sin_s, 0.0)

    nsa_consts = _nsa_constants(S)
    bd, tri = _rwkv_constants()
    mod = _modulation(c, w_mod, b_mod)
    wrt = w_router.T
    rb = router_bias.reshape(N_EXPERTS, 1)
    fin = norm_final.reshape(1, D)

    for l in range(depth):
        sh1, sc1, g1, sh2, sc2, g2 = (mod[l, :B, i * D:(i + 1) * D].reshape(B, 1, D) for i in range(6))
        p, r, km, vs_t, vw_t, mv_t = _inproj(x, norm_mix[l].reshape(1, D), sc1, sh1, cos_t, sin_s, _prep_w_in(w_in[l]))

        w1, w2, pos4 = _prep_compress(nsa_phi_w1[l], nsa_phi_w2[l], nsa_phi_pos[l])
        chunks = lambda g: p[:, :, g * LANES:(g + 1) * LANES].reshape(B, nc, CMP_STRIDE * LANES)
        kvcmp, kvcmp_t = _compress(chunks(G_KC), chunks(G_VC), pos4, w1, w2, cos_c, sin_c)
        mixn = _nsa(p, r, kvcmp, kvcmp_t, vs_t, vw_t, *nsa_consts)

        kmean = km[:, :, 0, :]
        kmean = jnp.concatenate([kmean, jnp.zeros((B, MOBA_NBP - kmean.shape[1], kmean.shape[2]), F32)], axis=1)
        mixm = _moba(p, mv_t, kmean)

        zl = jnp.zeros((DECAY_LORA, RWKV_WIDTH), F32)
        wup = jnp.concatenate([rwkv_w_up[l], zl], axis=0)
        aup = jnp.concatenate([zl, rwkv_a_up[l]], axis=0)
        vecs = jnp.stack([rwkv_w0[l], rwkv_a0[l], rwkv_k_k[l], rwkv_k_a[l], rwkv_r_k[l].reshape(-1),
                          rwkv_ln_w[l], rwkv_ln_b[l], jnp.zeros((RWKV_WIDTH,), F32)])
        orw = _rwkv(r, rwkv_mu[l].reshape(1, -1), wup, aup, rwkv_g_up[l], vecs, bd, tri)

        wo = w_out[l]
        x = _outproj(mixn, mixm, orw, x, g1,
                     _gain_and_mask(norm_nsa_out[l], _pad_nsa),
                     _gain_and_mask(norm_moba_out[l], _pad_moba),
                     _pad_nsa(wo[:NSA_WIDTH]).astype(BF16),
                     _pad_moba(wo[NSA_WIDTH:NSA_WIDTH + MOBA_WIDTH]).astype(BF16),
                     wo[NSA_WIDTH + MOBA_WIDTH:].astype(BF16))
        x = _moe(x, norm_ffn[l].reshape(1, D), sc2, sh2, g2, wrt, rb,
                 moe_w_gate[l].astype(BF16), moe_w_up[l].astype(BF16), moe_w_down[l].astype(BF16),
                 fin, final=(l == depth - 1))
    return x
```

```python
import functools

import jax
import jax.numpy as jnp
import numpy as np
from jax import lax
from jax.experimental import pallas as pl
from jax.experimental.pallas import tpu as pltpu

F32 = jnp.float32
BF16 = jnp.bfloat16

HEAD_DIM = 64
LANES = 128
ROPE_THETA = 10000.0
NORM_EPS = 1e-6
NEG = -1e30

NSA_HEADS = 6
NSA_KV_HEADS = 2
NSA_GROUP = NSA_HEADS // NSA_KV_HEADS
NSA_WIDTH = NSA_HEADS * HEAD_DIM
NSA_KV_WIDTH = NSA_KV_HEADS * HEAD_DIM
CMP_BLOCK = 32
CMP_STRIDE = 16
SLC_BLOCK = 64
SLC_TOPN = 16
WINDOW = 512
NSA_QB = 256
NSA_TK = 256
NSA_OUT_LANES = 256
NSA_WIN_SPAN = WINDOW + NSA_QB
ATTEND_UNROLL = 4

MOBA_HEADS = 4
MOBA_WIDTH = MOBA_HEADS * HEAD_DIM
MOBA_BLOCK = 256
MOBA_TOPK = 3
MOBA_NBP = 32
MOBA_HEADS_PER_STEP = 2

RWKV_HEADS = 6
RWKV_WIDTH = RWKV_HEADS * HEAD_DIM
DECAY_LORA = 64
AAA_LORA = 64
GATE_LORA = 128
RWKV_COLS = 3 * RWKV_WIDTH + DECAY_LORA + AAA_LORA + GATE_LORA
DECAY_SCALE = 0.606531
GN_EPS = 64e-5
RWKV_CHUNK = 64
RWKV_TB = 256

N_EXPERTS = 16
N_EXPERT_GROUPS = 4
EXPERTS_PER_GROUP = N_EXPERTS // N_EXPERT_GROUPS
EXPERT_FF = 256
MOE_EXPERTS_PER_STEP = 4

G_NSA_Q = 0
G_SLC = 6
G_WIN = 8
G_KC = 10
G_VC = 11
G_MOBA_Q = 12
G_MOBA_KV = 16
N_GROUPS = 20
ROPED_GROUPS = tuple(range(0, 10)) + tuple(range(12, 20))
P_COLS = N_GROUPS * LANES
R_GATE_BLOCK = RWKV_COLS // LANES
R_COLS = RWKV_COLS + NSA_KV_HEADS * LANES

VMEM_LIMIT = 56 * 1024 * 1024


def _cparams(sem):
    return pltpu.CompilerParams(dimension_semantics=sem, vmem_limit_bytes=VMEM_LIMIT)


def _dot(a, b):
    return jnp.dot(a, b, preferred_element_type=F32)


def _dot_nt(a, b):
    return lax.dot_general(a, b, (((1,), (1,)), ((), ())), preferred_element_type=F32)


def _dot_tn(a, b):
    return lax.dot_general(a, b, (((0,), (0,)), ((), ())), preferred_element_type=F32)


def _split2(x):
    hi = x.astype(BF16)
    lo = (x - hi.astype(F32)).astype(BF16)
    return hi, lo


def _split3(x):
    hi = x.astype(BF16)
    r1 = x - hi.astype(F32)
    mid = r1.astype(BF16)
    lo = (r1 - mid.astype(F32)).astype(BF16)
    return hi, mid, lo


def _mm3(a, b, dot=_dot):
    ah, al = _split2(a)
    bh, bl = _split2(b)
    return dot(ah, bh) + dot(ah, bl) + dot(al, bh)


def _mm1(a, b, dot=_dot):
    return dot(a.astype(BF16), b.astype(BF16))


def _mm3_exact_rhs(a, b_bf16):
    hi, mid, lo = _split3(a)
    return _dot(hi, b_bf16) + _dot(mid, b_bf16) + _dot(lo, b_bf16)


def _rope(y, cos_t, sin_s):
    lane = lax.broadcasted_iota(jnp.int32, y.shape, 1)
    rot = jnp.where(lane < HEAD_DIM // 2, pltpu.roll(y, LANES - HEAD_DIM // 2, 1), pltpu.roll(y, HEAD_DIM // 2, 1))
    return y * cos_t + rot * sin_s


def _mod_kernel(c_ref, w_ref, b_ref, o_ref):
    c = c_ref[...]
    ca = c * jax.nn.sigmoid(c)
    o_ref[0] = _dot(ca.astype(BF16), w_ref[0].astype(BF16)) + b_ref[0]


def _modulation(c, w_mod, b_mod):
    B, D = c.shape
    L, _, N = w_mod.shape
    tn = 512
    c8 = jnp.zeros((8, D), F32).at[:B].set(c)
    return pl.pallas_call(
        _mod_kernel,
        grid=(L, N // tn),
        in_specs=[
            pl.BlockSpec((8, D), lambda l, j: (0, 0)),
            pl.BlockSpec((1, D, tn), lambda l, j: (l, 0, j)),
            pl.BlockSpec((1, 1, tn), lambda l, j: (l, 0, j)),
        ],
        out_specs=pl.BlockSpec((1, 8, tn), lambda l, j: (l, 0, j)),
        out_shape=jax.ShapeDtypeStruct((L, 8, N), F32),
        compiler_params=_cparams(("arbitrary", "arbitrary")),
        name="modulation",
    )(c8, w_mod, b_mod.reshape(L, 1, N))


def _ones_over_values_t(y):
    yt = y.T
    row = lax.broadcasted_iota(jnp.int32, yt.shape, 0)
    return jnp.where(row < HEAD_DIM, 1.0, yt).astype(BF16)


def _inproj_kernel(x_ref, gain_ref, sc_ref, sh_ref, cos_ref, sin_ref, w_ref, p_ref, r_ref, km_ref,
                   vs_ref, vw_ref, mv_ref):
    x = x_ref[0]
    ms = jnp.mean(x * x, axis=-1, keepdims=True)
    h = x * lax.rsqrt(ms + NORM_EPS) * gain_ref[...]
    h = h * (1.0 + sc_ref[0]) + sh_ref[0]
    hb = h.astype(BF16)
    cos_t = cos_ref[0]
    sin_s = sin_ref[0]
    per = 4
    for g0 in range(0, N_GROUPS, per):
        y4 = _dot(hb, w_ref[:, g0 * LANES:(g0 + per) * LANES])
        for j in range(per):
            g = g0 + j
            y = y4[:, j * LANES:(j + 1) * LANES]
            if g in ROPED_GROUPS:
                y = _rope(y, cos_t, sin_s)
            if g >= G_MOBA_KV:
                km = jnp.mean(y, axis=0, keepdims=True)
                km_ref[0, 0, :, (g - G_MOBA_KV) * LANES:(g - G_MOBA_KV + 1) * LANES] = jnp.broadcast_to(km, (8, LANES))
                mv_ref[0, g - G_MOBA_KV, 0] = _ones_over_values_t(y)
            elif G_SLC <= g < G_WIN:
                vs_ref[0, g - G_SLC, 0] = _ones_over_values_t(y)
            elif G_WIN <= g < G_KC:
                yt = _ones_over_values_t(y)
                for t in range(y.shape[0] // NSA_QB):
                    vw_ref[0, g - G_WIN, t] = yt[:, t * NSA_QB:(t + 1) * NSA_QB]
            p_ref[0, :, g * LANES:(g + 1) * LANES] = y.astype(BF16)
    r_ref[0] = _dot(hb, w_ref[:, P_COLS:])


def _inproj(x, gain, sc, sh, cos_t, sin_s, w):
    B, S, D = x.shape
    tm = MOBA_BLOCK
    return pl.pallas_call(
        _inproj_kernel,
        grid=(B, S // tm),
        in_specs=[
            pl.BlockSpec((1, tm, D), lambda b, i: (b, i, 0)),
            pl.BlockSpec((1, D), lambda b, i: (0, 0)),
            pl.BlockSpec((1, 1, D), lambda b, i: (b, 0, 0)),
            pl.BlockSpec((1, 1, D), lambda b, i: (b, 0, 0)),
            pl.BlockSpec((1, tm, LANES), lambda b, i: (b, i, 0)),
            pl.BlockSpec((1, tm, LANES), lambda b, i: (b, i, 0)),
            pl.BlockSpec((D, P_COLS + R_COLS), lambda b, i: (0, 0)),
        ],
        out_specs=[
            pl.BlockSpec((1, tm, P_COLS), lambda b, i: (b, i, 0)),
            pl.BlockSpec((1, tm, R_COLS), lambda b, i: (b, i, 0)),
            pl.BlockSpec((1, 1, 8, MOBA_HEADS * LANES), lambda b, i: (b, i, 0, 0)),
            pl.BlockSpec((1, NSA_KV_HEADS, 1, LANES, tm), lambda b, i: (b, 0, i, 0, 0)),
            pl.BlockSpec((1, NSA_KV_HEADS, tm // NSA_QB, LANES, NSA_QB), lambda b, i: (b, 0, i, 0, 0)),
            pl.BlockSpec((1, MOBA_HEADS, 1, LANES, tm), lambda b, i: (b, 0, i, 0, 0)),
        ],
        out_shape=[
            jax.ShapeDtypeStruct((B, S, P_COLS), BF16),
            jax.ShapeDtypeStruct((B, S, R_COLS), F32),
            jax.ShapeDtypeStruct((B, S // tm, 8, MOBA_HEADS * LANES), F32),
            jax.ShapeDtypeStruct((B, NSA_KV_HEADS, S // tm, LANES, tm), BF16),
            jax.ShapeDtypeStruct((B, NSA_KV_HEADS, S // NSA_QB, LANES, NSA_QB), BF16),
            jax.ShapeDtypeStruct((B, MOBA_HEADS, S // tm, LANES, tm), BF16),
        ],
        compiler_params=_cparams(("arbitrary", "arbitrary")),
        name="inproj",
    )(x, gain, sc, sh, cos_t, sin_s, w)


def _prep_w_in(w):
    D = w.shape[0]
    o = 0
    parts = {}
    for name, width in (("nq", NSA_WIDTH), ("nkc", NSA_KV_WIDTH), ("nvc", NSA_KV_WIDTH), ("nks", NSA_KV_WIDTH),
                        ("nvs", NSA_KV_WIDTH), ("nkw", NSA_KV_WIDTH), ("nvw", NSA_KV_WIDTH), ("ngate", NSA_HEADS * 3),
                        ("mq", MOBA_WIDTH), ("mk", MOBA_WIDTH), ("mv", MOBA_WIDTH), ("rf", RWKV_COLS)):
        parts[name] = w[:, o:o + width]
        o += width
    hd = lambda t, h: t[:, h * HEAD_DIM:(h + 1) * HEAD_DIM]
    z = jnp.zeros((D, HEAD_DIM), F32)
    scale = HEAD_DIM ** -0.5
    cols = []
    for h in range(NSA_HEADS):
        cols += [hd(parts["nq"], h) * scale, z]
    for h in range(NSA_KV_HEADS):
        cols += [hd(parts["nks"], h), hd(parts["nvs"], h)]
    for h in range(NSA_KV_HEADS):
        cols += [hd(parts["nkw"], h), hd(parts["nvw"], h)]
    cols += [parts["nkc"], parts["nvc"]]
    for h in range(MOBA_HEADS):
        cols += [hd(parts["mq"], h) * scale, z]
    for h in range(MOBA_HEADS):
        cols += [hd(parts["mk"], h), hd(parts["mv"], h)]
    cols += [parts["rf"]]
    per = NSA_GROUP * 3
    for h in range(NSA_KV_HEADS):
        cols += [parts["ngate"][:, h * per:(h + 1) * per], jnp.zeros((D, LANES - per), F32)]
    return jnp.concatenate(cols, axis=1).astype(BF16)


def _cmp_kernel(kc_ref, vc_ref, pos_ref, w1_ref, w2_ref, cos_ref, sin_ref, o_ref, ot_ref):
    nc = kc_ref.shape[1]

    def hidden(t_ref, ia, ib):
        t = t_ref[0].astype(F32)
        a = _dot((t + pos_ref[ia:ia + 1, :]).astype(BF16), w1_ref[ia])
        b = _dot((t + pos_ref[ib:ib + 1, :]).astype(BF16), w1_ref[ib])
        return jax.nn.gelu(a + pltpu.roll(b, nc - 1, 0)).astype(BF16)

    gk = hidden(kc_ref, 0, 1)
    gv = hidden(vc_ref, 2, 3)
    for h in range(NSA_KV_HEADS):
        y = _dot(gk, w2_ref[2 * h]) + _dot(gv, w2_ref[2 * h + 1])
        y = _rope(y, cos_ref[0], sin_ref[0])
        o_ref[0, h] = y.astype(BF16)
        ot_ref[0, h] = _ones_over_values_t(y)


def _compress(kc16, vc16, pos4, w1, w2, cos_c, sin_c):
    B, nc, K = kc16.shape
    return pl.pallas_call(
        _cmp_kernel,
        grid=(B,),
        in_specs=[
            pl.BlockSpec((1, nc, K), lambda b: (b, 0, 0)),
            pl.BlockSpec((1, nc, K), lambda b: (b, 0, 0)),
            pl.BlockSpec((4, K), lambda b: (0, 0)),
            pl.BlockSpec((4, K, LANES), lambda b: (0, 0, 0)),
            pl.BlockSpec((4, LANES, LANES), lambda b: (0, 0, 0)),
            pl.BlockSpec((1, nc, LANES), lambda b: (b, 0, 0)),
            pl.BlockSpec((1, nc, LANES), lambda b: (b, 0, 0)),
        ],
        out_specs=[pl.BlockSpec((1, NSA_KV_HEADS, nc, LANES), lambda b: (b, 0, 0, 0)),
                   pl.BlockSpec((1, NSA_KV_HEADS, LANES, nc), lambda b: (b, 0, 0, 0))],
        out_shape=[jax.ShapeDtypeStruct((B, NSA_KV_HEADS, nc, LANES), BF16),
                   jax.ShapeDtypeStruct((B, NSA_KV_HEADS, LANES, nc), BF16)],
        compiler_params=_cparams(("arbitrary",)),
        name="nsa_compress",
    )(kc16, vc16, pos4, w1, w2, cos_c, sin_c)


def _prep_compress(phi_w1, phi_w2, phi_pos):
    half = CMP_BLOCK // 2
    eye = jnp.eye(NSA_KV_HEADS, dtype=F32)
    w1, pos = [], []
    for t in range(2):
        for part in range(2):
            w = phi_w1[t, part * half:(part + 1) * half]
            w1.append(jnp.einsum("lde,kK->lkdKe", w, eye).reshape(half * NSA_KV_WIDTH, NSA_KV_WIDTH))
            p = phi_pos[t, part * half:(part + 1) * half]
            pos.append(jnp.broadcast_to(p[:, None, :], (half, NSA_KV_HEADS, HEAD_DIM)).reshape(-1))
    w2 = []
    for h in range(NSA_KV_HEADS):
        for t in range(2):
            m = jnp.zeros((LANES, LANES), F32)
            m = m.at[h * HEAD_DIM:(h + 1) * HEAD_DIM, t * HEAD_DIM:(t + 1) * HEAD_DIM].set(phi_w2[t])
            w2.append(m)
    return jnp.stack(w1).astype(BF16), jnp.stack(w2).astype(BF16), jnp.stack(pos)


def _attend_tiles(n_loop, streams):
    for _, _, _, (_, _, p_scr, alpha_scr, m_scr, acc_scr) in streams:
        m_scr[...] = jnp.full(m_scr.shape, -jnp.inf, F32)
        alpha_scr[...] = jnp.ones(alpha_scr.shape, F32)
        acc_scr[...] = jnp.zeros(acc_scr.shape, F32)
        p_scr[...] = jnp.zeros(p_scr.shape, BF16)

    def fetch(j, buf):
        for scores, _, _, scratch in streams:
            scratch[buf][...] = scores(j)

    def apply_weights(j):
        for _, _, values_t, (_, _, p_scr, alpha_scr, _, acc_scr) in streams:
            acc_scr[...] = alpha_scr[0:1, :] * acc_scr[...] + _dot(values_t(j), p_scr[...])

    def softmax_tiles(buf, last=False):
        for _, mask_last, _, scratch in streams:
            p_scr, alpha_scr, m_scr = scratch[2], scratch[3], scratch[4]
            s_t = scratch[buf][...]
            if last:
                s_t = mask_last(s_t)
            m_prev = m_scr[0:1, :]
            m_new = jnp.maximum(m_prev, jnp.max(s_t, axis=0, keepdims=True))
            alpha_scr[0:1, :] = jnp.exp(m_prev - m_new)
            p_scr[...] = jnp.exp(s_t - m_new).astype(BF16)
            m_scr[0:1, :] = m_new

    def step(j, cur, nxt):
        apply_weights(jnp.maximum(j - 1, 0))
        softmax_tiles(cur)
        fetch(j + 1, nxt)

    fetch(0, 0)

    def quad(i, carry):
        for u in range(ATTEND_UNROLL):
            step(ATTEND_UNROLL * i + u, u % 2, (u + 1) % 2)
        return carry

    lax.fori_loop(0, n_loop // ATTEND_UNROLL, quad, 0)
    done = n_loop - n_loop % ATTEND_UNROLL

    @pl.when(n_loop % ATTEND_UNROLL >= 2)
    def _():
        step(done, 0, 1)
        step(done + 1, 1, 0)

    @pl.when(n_loop % 2 == 1)
    def _():
        step(n_loop - 1, 0, 0)

    apply_weights(jnp.maximum(n_loop - 1, 0))
    softmax_tiles(0, last=True)
    apply_weights(n_loop)
    return [scratch[5][...] for _, _, _, scratch in streams]


def _attend_scratch(tk, nq):
    return [pltpu.VMEM((tk, nq), F32), pltpu.VMEM((tk, nq), F32), pltpu.VMEM((tk, nq), BF16),
            pltpu.VMEM((8, nq), F32), pltpu.VMEM((8, nq), F32), pltpu.VMEM((LANES, nq), F32)]


def _nsa_kernel(q_ref, kvc_ref, kvct_ref, kvs_ref, vst_ref, kvw_ref, vwt_ref, gate0_ref, gate1_ref, mt_ref,
                et_ref, rel_ref, crel_ref, o_ref, *attend_scratch, n_pick):
    ci = pl.program_id(1)
    qb = NSA_QB
    tk = NSA_TK
    last = (ci * qb) // tk
    heads = range(NSA_KV_HEADS)
    n_scr = len(attend_scratch) // NSA_KV_HEADS
    lanes = lambda g: slice(g * LANES, (g + 1) * LANES)
    q = [jnp.concatenate([q_ref[0, :, lanes(NSA_GROUP * h + g)] for g in range(NSA_GROUP)], axis=0)
         for h in heads]

    o_c, imp_t = [], []
    mt = mt_ref[...]
    for h in heads:
        sm = jnp.where(crel_ref[...] <= ci * qb, _dot_nt(kvc_ref[0, h], q[h]), NEG)
        mx = jnp.max(sm, axis=0, keepdims=True)
        e = jnp.exp(sm - mx)
        den = jnp.maximum(jnp.sum(e, axis=0, keepdims=True), 1e-30)
        p_c = e * jnp.where(mx > 0.5 * NEG, 1.0 / den, 0.0)
        o_c.append(_dot(kvct_ref[0, h], p_c.astype(BF16)))
        hi, mid, lo = _split3(p_c[:, 0:qb] + p_c[:, qb:2 * qb] + p_c[:, 2 * qb:3 * qb])
        imp_t.append(_dot(mt, hi) + _dot(mt, mid) + _dot(mt, lo))

    blk = lax.broadcasted_iota(jnp.int32, imp_t[0].shape, 0)
    cur = (ci * qb + lax.broadcasted_iota(jnp.int32, imp_t[0].shape, 1)) // SLC_BLOCK
    forced = (blk == 0) | (blk == cur) | (blk == cur - 1)
    free = (blk <= cur) & jnp.logical_not(forced)
    sc = [jnp.where(free, imp_t[h], -1.0) for h in heads]
    for _ in range(n_pick):
        best = [jnp.max(sc[h], axis=0, keepdims=True) for h in heads]
        idx = [jnp.min(jnp.where(sc[h] == best[h], blk, blk.shape[0]), axis=0, keepdims=True) for h in heads]
        sc = [jnp.where(blk == idx[h], -2.0, sc[h]) for h in heads]
    w_nt = []
    for h in heads:
        bias = jnp.where(free, jnp.where(sc[h] < -1.5, 0.0, NEG), jnp.where(forced, 0.0, NEG)).T.astype(BF16)
        w_nt.append(jnp.concatenate([q[h], jnp.concatenate([bias] * NSA_GROUP, axis=0)], axis=1))
    o_w = [_nsa_window(ci, q[h], kvw_ref, vwt_ref, rel_ref, h) for h in heads]

    def causal_edge(s_t):
        return jnp.where(rel_ref[0:tk, :] <= ci * qb - last * tk, s_t, NEG)

    def stream(h):
        def scores(j):
            kv = kvs_ref[0, pl.ds(pl.multiple_of(j * tk, tk), tk), lanes(h)]
            return _dot_nt(jnp.concatenate([kv, et_ref[j]], axis=1), w_nt[h])
        return scores, causal_edge, lambda j: vst_ref[0, h, j], attend_scratch[h * n_scr:(h + 1) * n_scr]

    accs = _attend_tiles(last, [stream(h) for h in heads])

    for h, gate_ref in zip(heads, (gate0_ref, gate1_ref)):
        o_s = accs[h][HEAD_DIM:] * (1.0 / accs[h][0:1])
        gt = jax.nn.sigmoid(gate_ref[0]).T
        outs = []
        for g in range(NSA_GROUP):
            ls = slice(g * qb, (g + 1) * qb)
            outs.append(gt[3 * g:3 * g + 1] * o_c[h][HEAD_DIM:, ls] + gt[3 * g + 1:3 * g + 2] * o_s[:, ls]
                        + gt[3 * g + 2:3 * g + 3] * o_w[h][:, ls])
        outs.append(jnp.zeros((NSA_OUT_LANES - NSA_GROUP * HEAD_DIM, qb), F32))
        o_ref[0, :, h * NSA_OUT_LANES:(h + 1) * NSA_OUT_LANES] = jnp.concatenate(outs, axis=0).T


def _nsa_window(ci, q, kvw_ref, vwt_ref, rel_ref, h):
    qb = NSA_QB
    n_wt = NSA_WIN_SPAN // qb
    first = jnp.maximum(ci - WINDOW // qb, 0)
    kvw = kvw_ref[0, pl.ds(pl.multiple_of(first * qb, qb), NSA_WIN_SPAN), h * LANES:(h + 1) * LANES]
    s_t = _dot_nt(kvw, q)
    dist = (ci - first) * qb - rel_ref[...]
    sm = jnp.where(lax.bitcast_convert_type(dist, jnp.uint32) < WINDOW, s_t, NEG)
    e_w = jnp.exp(sm - jnp.max(sm, axis=0, keepdims=True))
    vw_t = jnp.concatenate([vwt_ref[0, h, first + t] for t in range(n_wt)], axis=1)
    acc_w = _dot(vw_t, e_w.astype(BF16))
    return acc_w[HEAD_DIM:] * (1.0 / acc_w[0:1])


def _nsa(p, r, kvcmp, kvcmp_t, vs_t, vw_t, cmp_to_slc_t, e3, rel, crel):
    B, S, _ = p.shape
    qb = NSA_QB
    n_top = min(SLC_TOPN, S // SLC_BLOCK)
    nq = NSA_GROUP * qb
    kvw = NSA_KV_HEADS * LANES
    assert G_NSA_Q == 0 and G_SLC * LANES % kvw == 0 and G_WIN * LANES % kvw == 0
    const = lambda a: pl.BlockSpec(a.shape, lambda b, i: (0,) * a.ndim)
    per_b = lambda a: pl.BlockSpec((1,) + a.shape[1:], lambda b, i: (b,) + (0,) * (a.ndim - 1))
    return pl.pallas_call(
        functools.partial(_nsa_kernel, n_pick=max(n_top - 3, 0)),
        grid=(B, S // qb),
        in_specs=[
            pl.BlockSpec((1, qb, NSA_HEADS * LANES), lambda b, i: (b, i, 0)),
            per_b(kvcmp), per_b(kvcmp_t),
            pl.BlockSpec((1, S, kvw), lambda b, i: (b, 0, G_SLC * LANES // kvw)),
            per_b(vs_t),
            pl.BlockSpec((1, S, kvw), lambda b, i: (b, 0, G_WIN * LANES // kvw)),
            per_b(vw_t),
            pl.BlockSpec((1, qb, LANES), lambda b, i: (b, i, R_GATE_BLOCK)),
            pl.BlockSpec((1, qb, LANES), lambda b, i: (b, i, R_GATE_BLOCK + 1)),
            const(cmp_to_slc_t), const(e3), const(rel), const(crel),
        ],
        out_specs=pl.BlockSpec((1, qb, NSA_KV_HEADS * NSA_OUT_LANES), lambda b, i: (b, i, 0)),
        out_shape=jax.ShapeDtypeStruct((B, S, NSA_KV_HEADS * NSA_OUT_LANES), F32),
        scratch_shapes=sum((_attend_scratch(NSA_TK, nq) for _ in range(NSA_KV_HEADS)), []),
        compiler_params=_cparams(("arbitrary", "arbitrary")),
        name="nsa_attention",
    )(p, kvcmp, kvcmp_t, p, vs_t, p, vw_t, r, r, cmp_to_slc_t, e3, rel, crel)


def _nsa_constants(S):
    n_cmp = (S - CMP_BLOCK) // CMP_STRIDE + 1
    nc = S // CMP_STRIDE
    n_slc = S // SLC_BLOCK
    c_start = np.arange(nc) * CMP_STRIDE
    s_start = np.arange(LANES) * SLC_BLOCK
    overlap = (np.minimum(c_start[None, :] + CMP_BLOCK, s_start[:, None] + SLC_BLOCK)
               - np.maximum(c_start[None, :], s_start[:, None]))
    m_t = np.clip(overlap, 0, None).astype(np.float32) / CMP_BLOCK
    m_t = np.where((np.arange(nc)[None, :] < n_cmp) & (np.arange(LANES)[:, None] < n_slc), m_t, 0.0)
    key_blk = (np.arange(S) // SLC_BLOCK).reshape(S // NSA_TK, NSA_TK, 1)
    e3 = (key_blk == np.arange(LANES)[None, None, :]).astype(np.float32)
    lane_q = np.arange(NSA_GROUP * NSA_QB) % NSA_QB
    rel = np.arange(NSA_WIN_SPAN)[:, None] - lane_q[None, :]
    crel = (np.arange(nc) * CMP_STRIDE + CMP_BLOCK - 1)[:, None] - lane_q[None, :]
    return jnp.asarray(m_t, BF16), jnp.asarray(e3, BF16), jnp.asarray(rel, jnp.int32), jnp.asarray(crel, jnp.int32)


def _moba_kernel(q_ref, kv_ref, vt_ref, km_ref, o_ref, sel_scr, *attend_scratch):
    ci = pl.program_id(2)
    qb = MOBA_BLOCK
    n_scr = len(attend_scratch) // MOBA_HEADS_PER_STEP

    def causal_edge(s_t):
        causal = lax.broadcasted_iota(jnp.int32, s_t.shape, 0) <= lax.broadcasted_iota(jnp.int32, s_t.shape, 1)
        return jnp.where(causal, s_t, NEG)

    def head_stream(h):
        lanes = slice(h * LANES, (h + 1) * LANES)
        q = q_ref[0, :, lanes]
        km_hi, km_lo = _split2(km_ref[0, :, lanes])
        gate_t = _dot_nt(km_hi, q) + _dot_nt(km_lo, q)
        blk = lax.broadcasted_iota(jnp.int32, gate_t.shape, 0)
        valid = blk < ci
        sc = jnp.where(valid, gate_t, -jnp.inf)
        picked = jnp.zeros(gate_t.shape, F32)
        for _ in range(MOBA_TOPK):
            best = jnp.max(sc, axis=0, keepdims=True)
            idx = jnp.min(jnp.where(sc == best, blk, gate_t.shape[0]), axis=0, keepdims=True)
            pick = blk == idx
            picked = jnp.where(pick, 1.0, picked)
            sc = jnp.where(pick, -jnp.inf, sc)
        sel = jnp.where(valid, picked, jnp.where(blk == ci, 1.0, 0.0))
        sel_scr[h] = jnp.where(sel > 0.5, 0.0, NEG)

        def scores(j):
            kvj = kv_ref[0, pl.ds(pl.multiple_of(j * qb, qb), qb), lanes]
            return _dot_nt(kvj, q) + sel_scr[h, pl.ds(j, 1), :]

        return scores, causal_edge, lambda j: vt_ref[0, h, j], attend_scratch[h * n_scr:(h + 1) * n_scr]

    accs = _attend_tiles(ci, [head_stream(h) for h in range(MOBA_HEADS_PER_STEP)])
    for h, acc in enumerate(accs):
        o_ref[0, :, h * LANES:(h + 1) * LANES] = (acc * (1.0 / acc[0:1])).T


def _moba(p, mv_t, kmean):
    B, S, _ = p.shape
    qb = MOBA_BLOCK
    hp = MOBA_HEADS_PER_STEP
    w = hp * LANES
    return pl.pallas_call(
        _moba_kernel,
        grid=(B, MOBA_HEADS // hp, S // qb),
        in_specs=[
            pl.BlockSpec((1, qb, w), lambda b, h, i: (b, i, G_MOBA_Q // hp + h)),
            pl.BlockSpec((1, S, w), lambda b, h, i: (b, 0, G_MOBA_KV // hp + h)),
            pl.BlockSpec((1, hp) + mv_t.shape[2:], lambda b, h, i: (b, h, 0, 0, 0)),
            pl.BlockSpec((1, MOBA_NBP, w), lambda b, h, i: (b, 0, h)),
        ],
        out_specs=pl.BlockSpec((1, qb, w), lambda b, h, i: (b, i, h)),
        out_shape=jax.ShapeDtypeStruct((B, S, MOBA_HEADS * LANES), F32),
        scratch_shapes=[pltpu.VMEM((hp, MOBA_NBP, qb), F32)] + sum((_attend_scratch(qb, qb) for _ in range(hp)), []),
        compiler_params=_cparams(("arbitrary", "arbitrary", "arbitrary")),
        name="moba_attention",
    )(p, p, mv_t, kmean)


def _rwkv_kernel(f_ref, mu_ref, wup_ref, aup_ref, gup_ref, vec_ref, bd_ref, tri_ref, o_ref, carry_scr, st_scr):
    i = pl.program_id(1)
    tb = f_ref.shape[1]
    C = RWKV_CHUNK
    W = RWKV_WIDTH

    @pl.when(i == 0)
    def _():
        carry_scr[...] = jnp.zeros(carry_scr.shape, F32)
        st_scr[...] = jnp.zeros(st_scr.shape, F32)

    feat = f_ref[0]
    rowi = lax.broadcasted_iota(jnp.int32, feat.shape, 0)
    prev = jnp.where(rowi == 0, carry_scr[0:1, :], pltpu.roll(feat, 1, 0))
    carry_scr[0:1, :] = feat[tb - 1:tb, :]
    xs = feat + (prev - feat) * mu_ref[...]
    r = xs[:, 0:W]
    k = xs[:, W:2 * W]
    v = xs[:, 2 * W:3 * W]
    wa = xs[:, 3 * W:3 * W + DECAY_LORA + AAA_LORA]
    gd = xs[:, 3 * W + DECAY_LORA + AAA_LORA:]
    w0, a0, k_k, k_a, r_k, ln_w, ln_b = (vec_ref[n:n + 1, :] for n in range(7))
    bd = bd_ref[...]

    def hsum(t):
        hi, lo = _split2(t)
        return _dot(hi, bd) + _dot(lo, bd)

    logw = -DECAY_SCALE * jax.nn.sigmoid(w0 + _mm3(jnp.tanh(wa), wup_ref[...]))
    a = jax.nn.sigmoid(a0 + _mm1(wa, aup_ref[...]))
    gate = _mm1(jax.nn.sigmoid(gd), gup_ref[...])
    kk = k * k_k
    kk = kk / jnp.maximum(jnp.sqrt(hsum(kk * kk)), 1e-12)
    k = k * (1.0 + (a - 1.0) * k_a)
    bonus = hsum(r * k * r_k) * v
    kka = kk * a

    ri = lax.broadcasted_iota(jnp.int32, (tb, tb), 0)
    cj = lax.broadcasted_iota(jnp.int32, (tb, tb), 1)
    same = (ri // C) == (cj // C)
    strict = same & (cj < ri)
    incl = same & (cj <= ri)
    eye = jnp.where(ri == cj, 1.0, 0.0)
    eye_c = eye[0:C, 0:C]
    hi, mid, lo = _split3(logw)
    tri = tri_ref[0]
    blk = tri_ref[1]
    cum = _dot(tri, hi) + _dot(tri, mid) + _dot(tri, lo)
    tot = _dot(blk, hi) + _dot(blk, mid) + _dot(blk, lo)
    e_neg = jnp.exp(-cum)
    e_end = jnp.exp(tot - cum)
    d_end = jnp.exp(tot)
    a_t = -kk * jnp.exp(cum - logw)
    r_t = r * jnp.exp(cum)
    b_t = kka * e_neg
    k_t = k * e_neg
    b_e = kka * e_end
    k_e = k * e_end
    heads = range(RWKV_HEADS)
    hs = [slice(h * HEAD_DIM, (h + 1) * HEAD_DIM) for h in heads]
    bf = lambda t: t.astype(BF16)
    ah = [bf(a_t[:, s]) for s in hs]
    rh = [r_t[:, s] for s in hs]
    vh = [bf(v[:, s]) for s in hs]
    ar = [jnp.concatenate([ah[h], bf(rh[h])], axis=0) for h in heads]
    xb = [_dot_nt(ar[h], bf(b_t[:, hs[h]])) for h in heads]
    xk = [_dot_nt(ar[h], bf(k_t[:, hs[h]])) for h in heads]
    n = [bf(jnp.where(strict, xb[h][0:tb], 0.0)) for h in heads]
    m_ak = [bf(jnp.where(strict, xk[h][0:tb], 0.0)) for h in heads]
    m_rb = [bf(jnp.where(incl, xb[h][tb:], 0.0)) for h in heads]
    m_rk = [bf(jnp.where(incl, xk[h][tb:], 0.0)) for h in heads]
    tinv = [eye + n[h].astype(F32) for h in heads]
    npow = n
    step = 1
    while 2 * step < C:
        npow = [bf(_dot(npow[h], npow[h])) for h in heads]
        tinv = [tinv[h] + _dot(bf(tinv[h]), npow[h]) for h in heads]
        step *= 2
    tinv = [bf(t) for t in tinv]
    g = [_dot(tinv[h], ah[h]) for h in heads]
    u0 = [_dot(tinv[h], bf(_dot(m_ak[h], vh[h]))) for h in heads]
    gb = [bf(t) for t in g]
    ub = [bf(t) for t in u0]
    r_y = [rh[h] + _dot(m_rb[h], gb[h]) for h in heads]
    y0 = [_dot(m_rb[h], ub[h]) + _dot(m_rk[h], vh[h]) for h in heads]
    beh = [bf(b_e[:, s]) for s in hs]
    keh = [bf(k_e[:, s]) for s in hs]
    st = [st_scr[h] for h in heads]
    yh = [[] for _ in heads]
    for c in range(tb // C):
        cs = slice(c * C, (c + 1) * C)
        p_st = [eye_c * d_end[c * C:c * C + 1, hs[h]] + _dot_tn(beh[h][cs], gb[h][cs]) for h in heads]
        q_st = [_dot_tn(beh[h][cs], ub[h][cs]) + _dot_tn(keh[h][cs], vh[h][cs]) for h in heads]
        for h in heads:
            yh[h].append(_mm3(r_y[h][cs], st[h]) + y0[h][cs])
        st = [_mm3(p_st[h], st[h]) + q_st[h] for h in heads]
    ys = []
    for h in heads:
        st_scr[h] = st[h]
        y = jnp.concatenate(yh[h], axis=0)
        mean = jnp.mean(y, axis=-1, keepdims=True)
        yc = y - mean
        var = jnp.mean(yc * yc, axis=-1, keepdims=True)
        ys.append(yc * lax.rsqrt(var + GN_EPS))
    yn = jnp.concatenate(ys, axis=1)
    o_ref[0] = (yn * ln_w + ln_b + bonus) * gate


def _rwkv(r, mu, wup, aup, gup, vecs, bd, tri):
    B, S, _ = r.shape
    tb = RWKV_TB
    full = lambda a: pl.BlockSpec(a.shape, lambda b, i: (0,) * a.ndim)
    return pl.pallas_call(
        _rwkv_kernel,
        grid=(B, S // tb),
        in_specs=[pl.BlockSpec((1, tb, RWKV_COLS), lambda b, i: (b, i, 0)),
                  full(mu), full(wup), full(aup), full(gup), full(vecs), full(bd), full(tri)],
        out_specs=pl.BlockSpec((1, tb, RWKV_WIDTH), lambda b, i: (b, i, 0)),
        out_shape=jax.ShapeDtypeStruct((B, S, RWKV_WIDTH), F32),
        scratch_shapes=[pltpu.VMEM((8, RWKV_COLS), F32), pltpu.VMEM((RWKV_HEADS, HEAD_DIM, HEAD_DIM), F32)],
        compiler_params=_cparams(("arbitrary", "arbitrary")),
        name="rwkv7",
    )(r, mu, wup, aup, gup, vecs, bd, tri)


def _rwkv_constants():
    head = np.arange(RWKV_WIDTH) // HEAD_DIM
    bd = (head[:, None] == head[None, :]).astype(np.float32)
    t = np.arange(RWKV_TB)
    same = (t[None, :] // RWKV_CHUNK) == (t[:, None] // RWKV_CHUNK)
    tri = np.stack([same & (t[None, :] <= t[:, None]), same]).astype(np.float32)
    return jnp.asarray(bd, BF16), jnp.asarray(tri, BF16)


def _outproj_kernel(mn_ref, mm_ref, rw_ref, x_ref, g1_ref, gn_ref, gm_ref, wn_ref, wm_ref, wr_ref, o_ref):
    def head_norm(o, gain_ref, width):
        o = jnp.where(gain_ref[1:2, :] > 0.5, o, 0.0)
        ms = jnp.sum(o * o, axis=-1, keepdims=True) * (1.0 / width)
        return (o * lax.rsqrt(ms + NORM_EPS) * gain_ref[0:1, :]).astype(BF16)

    z = _dot(head_norm(mn_ref[0], gn_ref, NSA_WIDTH), wn_ref[...])
    z = z + _dot(head_norm(mm_ref[0], gm_ref, MOBA_WIDTH), wm_ref[...])
    z = z + _dot(rw_ref[0].astype(BF16), wr_ref[...])
    o_ref[0] = x_ref[0] + g1_ref[0] * z


def _outproj(mixn, mixm, orw, x, g1, gn, gm, wn, wm, wr):
    B, S, D = x.shape
    tm = 512
    full = lambda a: pl.BlockSpec(a.shape, lambda b, i: (0,) * a.ndim)
    tok = lambda a: pl.BlockSpec((1, tm, a.shape[2]), lambda b, i: (b, i, 0))
    return pl.pallas_call(
        _outproj_kernel,
        grid=(B, S // tm),
        in_specs=[tok(mixn), tok(mixm), tok(orw), tok(x), pl.BlockSpec((1, 1, D), lambda b, i: (b, 0, 0)),
                  full(gn), full(gm), full(wn), full(wm), full(wr)],
        out_specs=tok(x),
        out_shape=jax.ShapeDtypeStruct((B, S, D), F32),
        compiler_params=_cparams(("arbitrary", "arbitrary")),
        name="outproj",
    )(mixn, mixm, orw, x, g1, gn, gm, wn, wm, wr)


def _pad_moba(t):
    t = t.reshape((MOBA_HEADS, HEAD_DIM) + t.shape[1:])
    return jnp.concatenate([jnp.zeros_like(t), t], axis=1).reshape((MOBA_HEADS * LANES,) + t.shape[2:])


def _pad_nsa(t):
    w = NSA_GROUP * HEAD_DIM
    t = t.reshape((NSA_KV_HEADS, w) + t.shape[1:])
    pad = jnp.zeros((NSA_KV_HEADS, NSA_OUT_LANES - w) + t.shape[2:], t.dtype)
    return jnp.concatenate([t, pad], axis=1).reshape((NSA_KV_HEADS * NSA_OUT_LANES,) + t.shape[2:])


def _gain_and_mask(gain, pad):
    return jnp.stack([pad(gain), pad(jnp.ones_like(gain))])


def _route(logit_t, bias_col):
    aff = jax.nn.sigmoid(logit_t)
    biased = aff + bias_col
    row = lambda t, e: t[e:e + 1, :]
    gp = EXPERTS_PER_GROUP
    scores = []
    for g in range(N_EXPERT_GROUPS):
        a_, b_, c_, d_ = (row(biased, g * gp + j) for j in range(gp))
        hi1, lo1, hi2, lo2 = jnp.maximum(a_, b_), jnp.minimum(a_, b_), jnp.maximum(c_, d_), jnp.minimum(c_, d_)
        scores.append(jnp.maximum(hi1, hi2) + jnp.maximum(jnp.minimum(hi1, hi2), jnp.maximum(lo1, lo2)))
    best = jnp.zeros_like(scores[0], dtype=jnp.int32)
    best_s = scores[0]
    for g in range(1, N_EXPERT_GROUPS):
        better = scores[g] > best_s
        best = jnp.where(better, g, best)
        best_s = jnp.where(better, scores[g], best_s)

    def in_group(t, j):
        out = row(t, j)
        for g in range(1, N_EXPERT_GROUPS):
            out = jnp.where(best == g, row(t, g * gp + j), out)
        return out

    vals = [in_group(biased, j) for j in range(gp)]
    affs = [in_group(aff, j) for j in range(gp)]

    def first_argmax(vs):
        top = functools.reduce(jnp.maximum, vs)
        idx = jnp.full(top.shape, gp, jnp.int32)
        for j in reversed(range(gp)):
            idx = jnp.where(vs[j] == top, j, idx)
        return idx

    i1 = first_argmax(vals)
    i2 = first_argmax([jnp.where(i1 == j, -jnp.inf, vals[j]) for j in range(gp)])
    pick = lambda idx: functools.reduce(lambda acc, j: jnp.where(idx == j, affs[j], acc), range(gp), jnp.zeros_like(affs[0]))
    w1, w2 = pick(i1), pick(i2)
    tot = w1 + w2
    e_iota = lax.broadcasted_iota(jnp.int32, logit_t.shape, 0)
    e1 = best * gp + i1
    e2 = best * gp + i2
    return jnp.where(e_iota == e1, w1 / tot, 0.0) + jnp.where(e_iota == e2, w2 / tot, 0.0)


def _moe_kernel(x_ref, gain_ref, sc_ref, sh_ref, g2_ref, wrt_ref, rb_ref, wg_ref, wu_ref, wd_ref, fin_ref,
                o_ref, hb_scr, cb_scr, acc_scr, *, final):
    e = pl.program_id(1)
    tm = x_ref.shape[1]

    @pl.when(e == 0)
    def _():
        x = x_ref[0]
        ms = jnp.mean(x * x, axis=-1, keepdims=True)
        h = x * lax.rsqrt(ms + NORM_EPS) * gain_ref[...]
        h = h * (1.0 + sc_ref[0]) + sh_ref[0]
        hb_scr[...] = h.astype(BF16)
        comb = _route(_mm3(wrt_ref[...], h, _dot_nt), rb_ref[...])
        cb_scr[...] = jnp.concatenate([comb, jnp.zeros((LANES - N_EXPERTS, tm), F32)], axis=0).T
        acc_scr[...] = jnp.zeros(acc_scr.shape, F32)

    hb = hb_scr[...]
    comb_t = cb_scr[...]
    hes = []
    for j in range(MOE_EXPERTS_PER_STEP):
        hg = _dot(hb, wg_ref[j])
        hu = _dot(hb, wu_ref[j])
        cbe = jnp.broadcast_to(comb_t[:, j:j + 1], (tm, LANES))
        he = hg * jax.nn.sigmoid(hg) * hu * jnp.concatenate([cbe] * (EXPERT_FF // LANES), axis=1)
        hes.append(he.astype(BF16))
    cb_scr[...] = pltpu.roll(comb_t, LANES - MOE_EXPERTS_PER_STEP, 1)
    acc_scr[...] += _dot(jnp.concatenate(hes, axis=1), wd_ref[...])

    @pl.when(e == N_EXPERTS // MOE_EXPERTS_PER_STEP - 1)
    def _():
        out = x_ref[0] + g2_ref[0] * acc_scr[...]
        if final:
            ms = jnp.mean(out * out, axis=-1, keepdims=True)
            out = out * lax.rsqrt(ms + NORM_EPS) * fin_ref[...]
        o_ref[0] = out


def _moe(x, gain, sc, sh, g2, wrt, rb, wg, wu, wd, fin, final):
    B, S, D = x.shape
    tm = 512
    tpb = S // tm
    eps = MOE_EXPERTS_PER_STEP
    wd = wd.reshape(N_EXPERTS * EXPERT_FF, D)
    tok = pl.BlockSpec((1, tm, D), lambda i, e: (i // tpb, i % tpb, 0))
    per_b = pl.BlockSpec((1, 1, D), lambda i, e: (i // tpb, 0, 0))
    full = lambda a: pl.BlockSpec(a.shape, lambda i, e: (0,) * a.ndim)
    return pl.pallas_call(
        functools.partial(_moe_kernel, final=final),
        grid=(B * tpb, N_EXPERTS // eps),
        in_specs=[tok, full(gain), per_b, per_b, per_b, full(wrt), full(rb),
                  pl.BlockSpec((eps, D, EXPERT_FF), lambda i, e: (e, 0, 0)),
                  pl.BlockSpec((eps, D, EXPERT_FF), lambda i, e: (e, 0, 0)),
                  pl.BlockSpec((eps * EXPERT_FF, D), lambda i, e: (e, 0)),
                  full(fin)],
        out_specs=tok,
        out_shape=jax.ShapeDtypeStruct((B, S, D), F32),
        scratch_shapes=[pltpu.VMEM((tm, D), BF16), pltpu.VMEM((tm, LANES), F32), pltpu.VMEM((tm, D), F32)],
        compiler_params=_cparams(("arbitrary", "arbitrary")),
        name="moe",
    )(x, gain, sc, sh, g2, wrt, rb, wg, wu, wd, fin)


def kernel(x, c, positions, w_mod, b_mod, norm_mix, norm_ffn, w_in, nsa_phi_w1, nsa_phi_w2, nsa_phi_pos, rwkv_mu, rwkv_w_up, rwkv_w0, rwkv_a_up, rwkv_a0, rwkv_g_up, rwkv_k_k, rwkv_k_a, rwkv_r_k, rwkv_ln_w, rwkv_ln_b, norm_nsa_out, norm_moba_out, w_out, w_router, router_bias, moe_w_gate, moe_w_up, moe_w_down, norm_final):
    B, S, D = x.shape
    depth = w_in.shape[0]
    assert S % NSA_TK == 0 and S % MOBA_BLOCK == 0 and S % RWKV_TB == 0 and S >= WINDOW + NSA_QB
    assert S // SLC_BLOCK <= LANES and S // MOBA_BLOCK <= MOBA_NBP

    inv = ROPE_THETA ** (-jnp.arange(0, HEAD_DIM, 2, dtype=F32) / HEAD_DIM)
    ang = positions.astype(F32)[..., None] * inv
    cos, sin = jnp.cos(ang), jnp.sin(ang)
    one, zero = jnp.ones((B, S, HEAD_DIM), F32), jnp.zeros((B, S, HEAD_DIM), F32)
    cos_t = jnp.concatenate([cos, cos, one], axis=-1)
    sin_s = jnp.concatenate([-sin, sin, zero], axis=-1)
    nc = S // CMP_STRIDE
    n_cmp = (S - CMP_BLOCK) // CMP_STRIDE + 1
    pad_c = lambda t, fill: jnp.concatenate([t[:, CMP_BLOCK - 1::CMP_STRIDE][:, :n_cmp],
                                             jnp.full((B, nc - n_cmp, LANES), fill, F32)], axis=1)
    cos_c, sin_c = pad_c(cos_t, 1.0), pad_c(sin_s, 0.0)

    nsa_consts = _nsa_constants(S)
    bd, tri = _rwkv_constants()
    mod = _modulation(c, w_mod, b_mod)
    wrt = w_router.T
    rb = router_bias.reshape(N_EXPERTS, 1)
    fin = norm_final.reshape(1, D)

    for l in range(depth):
        sh1, sc1, g1, sh2, sc2, g2 = (mod[l, :B, i * D:(i + 1) * D].reshape(B, 1, D) for i in range(6))
        p, r, km, vs_t, vw_t, mv_t = _inproj(x, norm_mix[l].reshape(1, D), sc1, sh1, cos_t, sin_s, _prep_w_in(w_in[l]))

        w1, w2, pos4 = _prep_compress(nsa_phi_w1[l], nsa_phi_w2[l], nsa_phi_pos[l])
        chunks = lambda g: p[:, :, g * LANES:(g + 1) * LANES].reshape(B, nc, CMP_STRIDE * LANES)
        kvcmp, kvcmp_t = _compress(chunks(G_KC), chunks(G_VC), pos4, w1, w2, cos_c, sin_c)
        mixn = _nsa(p, r, kvcmp, kvcmp_t, vs_t, vw_t, *nsa_consts)

        kmean = km[:, :, 0, :]
        kmean = jnp.concatenate([kmean, jnp.zeros((B, MOBA_NBP - kmean.shape[1], kmean.shape[2]), F32)], axis=1)
        mixm = _moba(p, mv_t, kmean)

        zl = jnp.zeros((DECAY_LORA, RWKV_WIDTH), F32)
        wup = jnp.concatenate([rwkv_w_up[l], zl], axis=0)
        aup = jnp.concatenate([zl, rwkv_a_up[l]], axis=0)
        vecs = jnp.stack([rwkv_w0[l], rwkv_a0[l], rwkv_k_k[l], rwkv_k_a[l], rwkv_r_k[l].reshape(-1),
                          rwkv_ln_w[l], rwkv_ln_b[l], jnp.zeros((RWKV_WIDTH,), F32)])
        orw = _rwkv(r, rwkv_mu[l].reshape(1, -1), wup, aup, rwkv_g_up[l], vecs, bd, tri)

        wo = w_out[l]
        x = _outproj(mixn, mixm, orw, x, g1,
                     _gain_and_mask(norm_nsa_out[l], _pad_nsa),
                     _gain_and_mask(norm_moba_out[l], _pad_moba),
                     _pad_nsa(wo[:NSA_WIDTH]).astype(BF16),
                     _pad_moba(wo[NSA_WIDTH:NSA_WIDTH + MOBA_WIDTH]).astype(BF16),
                     wo[NSA_WIDTH + MOBA_WIDTH:].astype(BF16))
        x = _moe(x, norm_ffn[l].reshape(1, D), sc2, sh2, g2, wrt, rb,
                 moe_w_gate[l].astype(BF16), moe_w_up[l].astype(BF16), moe_w_down[l].astype(BF16),
                 fin, final=(l == depth - 1))
    return x
```

```python
import functools

import jax
import jax.numpy as jnp
import numpy as np
from jax import lax
from jax.experimental import pallas as pl
from jax.experimental.pallas import tpu as pltpu

F32 = jnp.float32
BF16 = jnp.bfloat16

HEAD_DIM = 64
LANES = 128
ROPE_THETA = 10000.0
NORM_EPS = 1e-6
NEG = -1e30
LOG2_E = 1.4426950408889634

NSA_HEADS = 6
NSA_KV_HEADS = 2
NSA_GROUP = NSA_HEADS // NSA_KV_HEADS
NSA_WIDTH = NSA_HEADS * HEAD_DIM
NSA_KV_WIDTH = NSA_KV_HEADS * HEAD_DIM
CMP_BLOCK = 32
CMP_STRIDE = 16
SLC_BLOCK = 64
SLC_TOPN = 16
WINDOW = 512
NSA_QB = 256
NSA_TK = 256
NSA_OUT_LANES = 256
NSA_WIN_SPAN = WINDOW + NSA_QB
ATTEND_UNROLL = 4

MOBA_HEADS = 4
MOBA_WIDTH = MOBA_HEADS * HEAD_DIM
MOBA_BLOCK = 256
MOBA_TOPK = 3
MOBA_NBP = 32
MOBA_HEADS_PER_STEP = 4

RWKV_HEADS = 6
RWKV_WIDTH = RWKV_HEADS * HEAD_DIM
DECAY_LORA = 64
AAA_LORA = 64
GATE_LORA = 128
RWKV_COLS = 3 * RWKV_WIDTH + DECAY_LORA + AAA_LORA + GATE_LORA
DECAY_SCALE = 0.606531
GN_EPS = 64e-5
RWKV_CHUNK = 64
RWKV_TB = 256

N_EXPERTS = 16
N_EXPERT_GROUPS = 4
EXPERTS_PER_GROUP = N_EXPERTS // N_EXPERT_GROUPS
EXPERT_FF = 256
MOE_EXPERTS_PER_STEP = 4

G_NSA_Q = 0
G_SLC = 6
G_WIN = 8
G_KC = 10
G_VC = 11
G_MOBA_Q = 12
G_MOBA_KV = 16
N_GROUPS = 20
ROPED_GROUPS = tuple(range(0, 10)) + tuple(range(12, 20))
P_COLS = N_GROUPS * LANES
R_GATE_BLOCK = RWKV_COLS // LANES
R_COLS = RWKV_COLS + NSA_KV_HEADS * LANES

VMEM_LIMIT = 56 * 1024 * 1024


def _cparams(sem):
    return pltpu.CompilerParams(dimension_semantics=sem, vmem_limit_bytes=VMEM_LIMIT)


def _dot(a, b):
    return jnp.dot(a, b, preferred_element_type=F32)


def _dot_nt(a, b):
    return lax.dot_general(a, b, (((1,), (1,)), ((), ())), preferred_element_type=F32)


def _dot_tn(a, b):
    return lax.dot_general(a, b, (((0,), (0,)), ((), ())), preferred_element_type=F32)


def _split2(x):
    hi = x.astype(BF16)
    lo = (x - hi.astype(F32)).astype(BF16)
    return hi, lo


def _split3(x):
    hi = x.astype(BF16)
    r1 = x - hi.astype(F32)
    mid = r1.astype(BF16)
    lo = (r1 - mid.astype(F32)).astype(BF16)
    return hi, mid, lo


def _mm3(a, b, dot=_dot):
    ah, al = _split2(a)
    bh, bl = _split2(b)
    return dot(ah, bh) + dot(ah, bl) + dot(al, bh)


def _mm1(a, b, dot=_dot):
    return dot(a.astype(BF16), b.astype(BF16))


def _mm3_exact_rhs(a, b_bf16):
    hi, mid, lo = _split3(a)
    return _dot(hi, b_bf16) + _dot(mid, b_bf16) + _dot(lo, b_bf16)


def _rope(y, cos_t, sin_s):
    lane = lax.broadcasted_iota(jnp.int32, y.shape, 1)
    rot = jnp.where(lane < HEAD_DIM // 2, pltpu.roll(y, LANES - HEAD_DIM // 2, 1), pltpu.roll(y, HEAD_DIM // 2, 1))
    return y * cos_t + rot * sin_s


def _mod_kernel(c_ref, w_ref, b_ref, o_ref):
    c = c_ref[...]
    ca = c * jax.nn.sigmoid(c)
    o_ref[0] = _dot(ca.astype(BF16), w_ref[0].astype(BF16)) + b_ref[0]


def _modulation(c, w_mod, b_mod):
    B, D = c.shape
    L, _, N = w_mod.shape
    tn = 512
    c8 = jnp.zeros((8, D), F32).at[:B].set(c)
    return pl.pallas_call(
        _mod_kernel,
        grid=(L, N // tn),
        in_specs=[
            pl.BlockSpec((8, D), lambda l, j: (0, 0)),
            pl.BlockSpec((1, D, tn), lambda l, j: (l, 0, j)),
            pl.BlockSpec((1, 1, tn), lambda l, j: (l, 0, j)),
        ],
        out_specs=pl.BlockSpec((1, 8, tn), lambda l, j: (l, 0, j)),
        out_shape=jax.ShapeDtypeStruct((L, 8, N), F32),
        compiler_params=_cparams(("arbitrary", "arbitrary")),
        name="modulation",
    )(c8, w_mod, b_mod.reshape(L, 1, N))


def _ones_over_values_t(y):
    yt = y.T
    row = lax.broadcasted_iota(jnp.int32, yt.shape, 0)
    return jnp.where(row < HEAD_DIM, 1.0, yt).astype(BF16)


def _inproj_kernel(x_ref, gain_ref, sc_ref, sh_ref, cos_ref, sin_ref, w_ref, p_ref, r_ref, km_ref,
                   vs_ref, vw_ref, mv_ref):
    x = x_ref[0]
    ms = jnp.mean(x * x, axis=-1, keepdims=True)
    h = x * lax.rsqrt(ms + NORM_EPS) * gain_ref[...]
    h = h * (1.0 + sc_ref[0]) + sh_ref[0]
    hb = h.astype(BF16)
    cos_t = cos_ref[0]
    sin_s = sin_ref[0]
    per = 4
    for g0 in range(0, N_GROUPS, per):
        y4 = _dot(hb, w_ref[:, g0 * LANES:(g0 + per) * LANES])
        for j in range(per):
            g = g0 + j
            y = y4[:, j * LANES:(j + 1) * LANES]
            if g in ROPED_GROUPS:
                y = _rope(y, cos_t, sin_s)
            if g >= G_MOBA_KV:
                km = jnp.mean(y, axis=0, keepdims=True)
                km_ref[0, 0, :, (g - G_MOBA_KV) * LANES:(g - G_MOBA_KV + 1) * LANES] = jnp.broadcast_to(km, (8, LANES))
                mv_ref[0, g - G_MOBA_KV, 0] = _ones_over_values_t(y)
            elif G_SLC <= g < G_WIN:
                vs_ref[0, g - G_SLC, 0] = _ones_over_values_t(y)
            elif G_WIN <= g < G_KC:
                yt = _ones_over_values_t(y)
                for t in range(y.shape[0] // NSA_QB):
                    vw_ref[0, g - G_WIN, t] = yt[:, t * NSA_QB:(t + 1) * NSA_QB]
            p_ref[0, :, g * LANES:(g + 1) * LANES] = y.astype(BF16)
    r_ref[0] = _dot(hb, w_ref[:, P_COLS:])


def _inproj(x, gain, sc, sh, cos_t, sin_s, w):
    B, S, D = x.shape
    tm = MOBA_BLOCK
    return pl.pallas_call(
        _inproj_kernel,
        grid=(B, S // tm),
        in_specs=[
            pl.BlockSpec((1, tm, D), lambda b, i: (b, i, 0)),
            pl.BlockSpec((1, D), lambda b, i: (0, 0)),
            pl.BlockSpec((1, 1, D), lambda b, i: (b, 0, 0)),
            pl.BlockSpec((1, 1, D), lambda b, i: (b, 0, 0)),
            pl.BlockSpec((1, tm, LANES), lambda b, i: (b, i, 0)),
            pl.BlockSpec((1, tm, LANES), lambda b, i: (b, i, 0)),
            pl.BlockSpec((D, P_COLS + R_COLS), lambda b, i: (0, 0)),
        ],
        out_specs=[
            pl.BlockSpec((1, tm, P_COLS), lambda b, i: (b, i, 0)),
            pl.BlockSpec((1, tm, R_COLS), lambda b, i: (b, i, 0)),
            pl.BlockSpec((1, 1, 8, MOBA_HEADS * LANES), lambda b, i: (b, i, 0, 0)),
            pl.BlockSpec((1, NSA_KV_HEADS, 1, LANES, tm), lambda b, i: (b, 0, i, 0, 0)),
            pl.BlockSpec((1, NSA_KV_HEADS, tm // NSA_QB, LANES, NSA_QB), lambda b, i: (b, 0, i, 0, 0)),
            pl.BlockSpec((1, MOBA_HEADS, 1, LANES, tm), lambda b, i: (b, 0, i, 0, 0)),
        ],
        out_shape=[
            jax.ShapeDtypeStruct((B, S, P_COLS), BF16),
            jax.ShapeDtypeStruct((B, S, R_COLS), F32),
            jax.ShapeDtypeStruct((B, S // tm, 8, MOBA_HEADS * LANES), F32),
            jax.ShapeDtypeStruct((B, NSA_KV_HEADS, S // tm, LANES, tm), BF16),
            jax.ShapeDtypeStruct((B, NSA_KV_HEADS, S // NSA_QB, LANES, NSA_QB), BF16),
            jax.ShapeDtypeStruct((B, MOBA_HEADS, S // tm, LANES, tm), BF16),
        ],
        compiler_params=_cparams(("arbitrary", "arbitrary")),
        name="inproj",
    )(x, gain, sc, sh, cos_t, sin_s, w)


def _prep_w_in(w):
    D = w.shape[0]
    o = 0
    parts = {}
    for name, width in (("nq", NSA_WIDTH), ("nkc", NSA_KV_WIDTH), ("nvc", NSA_KV_WIDTH), ("nks", NSA_KV_WIDTH),
                        ("nvs", NSA_KV_WIDTH), ("nkw", NSA_KV_WIDTH), ("nvw", NSA_KV_WIDTH), ("ngate", NSA_HEADS * 3),
                        ("mq", MOBA_WIDTH), ("mk", MOBA_WIDTH), ("mv", MOBA_WIDTH), ("rf", RWKV_COLS)):
        parts[name] = w[:, o:o + width]
        o += width
    hd = lambda t, h: t[:, h * HEAD_DIM:(h + 1) * HEAD_DIM]
    z = jnp.zeros((D, HEAD_DIM), F32)
    scale = HEAD_DIM ** -0.5 * LOG2_E
    cols = []
    for h in range(NSA_HEADS):
        cols += [hd(parts["nq"], h) * scale, z]
    for h in range(NSA_KV_HEADS):
        cols += [hd(parts["nks"], h), hd(parts["nvs"], h)]
    for h in range(NSA_KV_HEADS):
        cols += [hd(parts["nkw"], h), hd(parts["nvw"], h)]
    cols += [parts["nkc"], parts["nvc"]]
    for h in range(MOBA_HEADS):
        cols += [hd(parts["mq"], h) * scale, z]
    for h in range(MOBA_HEADS):
        cols += [hd(parts["mk"], h), hd(parts["mv"], h)]
    cols += [parts["rf"]]
    per = NSA_GROUP * 3
    for h in range(NSA_KV_HEADS):
        cols += [parts["ngate"][:, h * per:(h + 1) * per], jnp.zeros((D, LANES - per), F32)]
    return jnp.concatenate(cols, axis=1).astype(BF16)


def _cmp_kernel(kc_ref, vc_ref, pos_ref, w1_ref, w2_ref, cos_ref, sin_ref, o_ref, ot_ref):
    nc = kc_ref.shape[1]

    def hidden(t_ref, ia, ib):
        t = t_ref[0].astype(F32)
        a = _dot((t + pos_ref[ia:ia + 1, :]).astype(BF16), w1_ref[ia])
        b = _dot((t + pos_ref[ib:ib + 1, :]).astype(BF16), w1_ref[ib])
        return jax.nn.gelu(a + pltpu.roll(b, nc - 1, 0)).astype(BF16)

    gk = hidden(kc_ref, 0, 1)
    gv = hidden(vc_ref, 2, 3)
    for h in range(NSA_KV_HEADS):
        y = _dot(gk, w2_ref[2 * h]) + _dot(gv, w2_ref[2 * h + 1])
        y = _rope(y, cos_ref[0], sin_ref[0])
        o_ref[0, h] = y.astype(BF16)
        ot_ref[0, h] = _ones_over_values_t(y)


def _compress(kc16, vc16, pos4, w1, w2, cos_c, sin_c):
    B, nc, K = kc16.shape
    return pl.pallas_call(
        _cmp_kernel,
        grid=(B,),
        in_specs=[
            pl.BlockSpec((1, nc, K), lambda b: (b, 0, 0)),
            pl.BlockSpec((1, nc, K), lambda b: (b, 0, 0)),
            pl.BlockSpec((4, K), lambda b: (0, 0)),
            pl.BlockSpec((4, K, LANES), lambda b: (0, 0, 0)),
            pl.BlockSpec((4, LANES, LANES), lambda b: (0, 0, 0)),
            pl.BlockSpec((1, nc, LANES), lambda b: (b, 0, 0)),
            pl.BlockSpec((1, nc, LANES), lambda b: (b, 0, 0)),
        ],
        out_specs=[pl.BlockSpec((1, NSA_KV_HEADS, nc, LANES), lambda b: (b, 0, 0, 0)),
                   pl.BlockSpec((1, NSA_KV_HEADS, LANES, nc), lambda b: (b, 0, 0, 0))],
        out_shape=[jax.ShapeDtypeStruct((B, NSA_KV_HEADS, nc, LANES), BF16),
                   jax.ShapeDtypeStruct((B, NSA_KV_HEADS, LANES, nc), BF16)],
        compiler_params=_cparams(("arbitrary",)),
        name="nsa_compress",
    )(kc16, vc16, pos4, w1, w2, cos_c, sin_c)


def _prep_compress(phi_w1, phi_w2, phi_pos):
    half = CMP_BLOCK // 2
    eye = jnp.eye(NSA_KV_HEADS, dtype=F32)
    w1, pos = [], []
    for t in range(2):
        for part in range(2):
            w = phi_w1[t, part * half:(part + 1) * half]
            w1.append(jnp.einsum("lde,kK->lkdKe", w, eye).reshape(half * NSA_KV_WIDTH, NSA_KV_WIDTH))
            p = phi_pos[t, part * half:(part + 1) * half]
            pos.append(jnp.broadcast_to(p[:, None, :], (half, NSA_KV_HEADS, HEAD_DIM)).reshape(-1))
    w2 = []
    for h in range(NSA_KV_HEADS):
        for t in range(2):
            m = jnp.zeros((LANES, LANES), F32)
            m = m.at[h * HEAD_DIM:(h + 1) * HEAD_DIM, t * HEAD_DIM:(t + 1) * HEAD_DIM].set(phi_w2[t])
            w2.append(m)
    return jnp.stack(w1).astype(BF16), jnp.stack(w2).astype(BF16), jnp.stack(pos)


def _attend_tiles(n_loop, streams):
    for _, _, _, (_, _, p_scr, alpha_scr, m_scr, acc_scr) in streams:
        m_scr[...] = jnp.full(m_scr.shape, -jnp.inf, F32)
        alpha_scr[...] = jnp.ones(alpha_scr.shape, F32)
        acc_scr[...] = jnp.zeros(acc_scr.shape, F32)
        p_scr[...] = jnp.zeros(p_scr.shape, BF16)

    def fetch(j, buf):
        for scores, _, _, scratch in streams:
            scratch[buf][...] = scores(j)

    def apply_weights(j):
        for _, _, values_t, (_, _, p_scr, alpha_scr, _, acc_scr) in streams:
            acc_scr[...] = alpha_scr[0:1, :] * acc_scr[...] + _dot(values_t(j), p_scr[...])

    def softmax_tiles(buf, last=False):
        for _, mask_last, _, scratch in streams:
            p_scr, alpha_scr, m_scr = scratch[2], scratch[3], scratch[4]
            s_t = scratch[buf][...]
            if last:
                s_t = mask_last(s_t)
            m_prev = m_scr[0:1, :]
            m_new = jnp.maximum(m_prev, jnp.max(s_t, axis=0, keepdims=True))
            alpha_scr[0:1, :] = jnp.exp2(m_prev - m_new)
            p_scr[...] = jnp.exp2(s_t - m_new).astype(BF16)
            m_scr[0:1, :] = m_new

    def step(j, cur, nxt):
        apply_weights(jnp.maximum(j - 1, 0))
        softmax_tiles(cur)
        fetch(j + 1, nxt)

    fetch(0, 0)

    def quad(i, carry):
        for u in range(ATTEND_UNROLL):
            step(ATTEND_UNROLL * i + u, u % 2, (u + 1) % 2)
        return carry

    lax.fori_loop(0, n_loop // ATTEND_UNROLL, quad, 0)
    done = n_loop - n_loop % ATTEND_UNROLL

    @pl.when(n_loop % ATTEND_UNROLL >= 2)
    def _():
        step(done, 0, 1)
        step(done + 1, 1, 0)

    @pl.when(n_loop % 2 == 1)
    def _():
        step(n_loop - 1, 0, 0)

    apply_weights(jnp.maximum(n_loop - 1, 0))
    softmax_tiles(0, last=True)
    apply_weights(n_loop)
    return [scratch[5][...] for _, _, _, scratch in streams]


def _attend_scratch(tk, nq):
    return [pltpu.VMEM((tk, nq), F32), pltpu.VMEM((tk, nq), F32), pltpu.VMEM((tk, nq), BF16),
            pltpu.VMEM((8, nq), F32), pltpu.VMEM((8, nq), F32), pltpu.VMEM((LANES, nq), F32)]


def _nsa_kernel(q_ref, kvc_ref, kvct_ref, kvs_ref, vst_ref, kvw_ref, vwt_ref, gate0_ref, gate1_ref, mt_ref,
                et_ref, rel_ref, crel_ref, o_ref, *attend_scratch, n_pick):
    ci = pl.program_id(1)
    qb = NSA_QB
    tk = NSA_TK
    last = (ci * qb) // tk
    heads = range(NSA_KV_HEADS)
    n_scr = len(attend_scratch) // NSA_KV_HEADS
    lanes = lambda g: slice(g * LANES, (g + 1) * LANES)
    q = [jnp.concatenate([q_ref[0, :, lanes(NSA_GROUP * h + g)] for g in range(NSA_GROUP)], axis=0)
         for h in heads]

    o_c, imp_t = [], []
    mt = mt_ref[...]
    for h in heads:
        sm = jnp.where(crel_ref[...] <= ci * qb, _dot_nt(kvc_ref[0, h], q[h]), NEG)
        mx = jnp.max(sm, axis=0, keepdims=True)
        e = jnp.exp2(sm - mx)
        den = jnp.maximum(jnp.sum(e, axis=0, keepdims=True), 1e-30)
        p_c = e * jnp.where(mx > 0.5 * NEG, 1.0 / den, 0.0)
        o_c.append(_dot(kvct_ref[0, h], p_c.astype(BF16)))
        hi, mid, lo = _split3(p_c[:, 0:qb] + p_c[:, qb:2 * qb] + p_c[:, 2 * qb:3 * qb])
        imp_t.append(_dot(mt, hi) + _dot(mt, mid) + _dot(mt, lo))

    blk = lax.broadcasted_iota(jnp.int32, imp_t[0].shape, 0)
    cur = (ci * qb + lax.broadcasted_iota(jnp.int32, imp_t[0].shape, 1)) // SLC_BLOCK
    forced = (blk == 0) | (blk == cur) | (blk == cur - 1)
    free = (blk <= cur) & jnp.logical_not(forced)
    sc = [jnp.where(free, imp_t[h], -1.0) for h in heads]
    for _ in range(n_pick):
        best = [jnp.max(sc[h], axis=0, keepdims=True) for h in heads]
        idx = [jnp.min(jnp.where(sc[h] == best[h], blk, blk.shape[0]), axis=0, keepdims=True) for h in heads]
        sc = [jnp.where(blk == idx[h], -2.0, sc[h]) for h in heads]
    w_nt = []
    for h in heads:
        bias = jnp.where(free, jnp.where(sc[h] < -1.5, 0.0, NEG), jnp.where(forced, 0.0, NEG)).T.astype(BF16)
        w_nt.append(jnp.concatenate([q[h], jnp.concatenate([bias] * NSA_GROUP, axis=0)], axis=1))
    o_w = [_nsa_window(ci, q[h], kvw_ref, vwt_ref, rel_ref, h) for h in heads]

    def causal_edge(s_t):
        return jnp.where(rel_ref[0:tk, :] <= ci * qb - last * tk, s_t, NEG)

    def stream(h):
        def scores(j):
            kv = kvs_ref[0, pl.ds(pl.multiple_of(j * tk, tk), tk), lanes(h)]
            return _dot_nt(jnp.concatenate([kv, et_ref[j]], axis=1), w_nt[h])
        return scores, causal_edge, lambda j: vst_ref[0, h, j], attend_scratch[h * n_scr:(h + 1) * n_scr]

    accs = _attend_tiles(last, [stream(h) for h in heads])

    for h, gate_ref in zip(heads, (gate0_ref, gate1_ref)):
        o_s = accs[h][HEAD_DIM:] * (1.0 / accs[h][0:1])
        gt = jax.nn.sigmoid(gate_ref[0]).T
        outs = []
        for g in range(NSA_GROUP):
            ls = slice(g * qb, (g + 1) * qb)
            outs.append(gt[3 * g:3 * g + 1] * o_c[h][HEAD_DIM:, ls] + gt[3 * g + 1:3 * g + 2] * o_s[:, ls]
                        + gt[3 * g + 2:3 * g + 3] * o_w[h][:, ls])
        outs.append(jnp.zeros((NSA_OUT_LANES - NSA_GROUP * HEAD_DIM, qb), F32))
        o_ref[0, :, h * NSA_OUT_LANES:(h + 1) * NSA_OUT_LANES] = jnp.concatenate(outs, axis=0).T


def _nsa_window(ci, q, kvw_ref, vwt_ref, rel_ref, h):
    qb = NSA_QB
    n_wt = NSA_WIN_SPAN // qb
    first = jnp.maximum(ci - WINDOW // qb, 0)
    kvw = kvw_ref[0, pl.ds(pl.multiple_of(first * qb, qb), NSA_WIN_SPAN), h * LANES:(h + 1) * LANES]
    s_t = _dot_nt(kvw, q)
    dist = (ci - first) * qb - rel_ref[...]
    sm = jnp.where(lax.bitcast_convert_type(dist, jnp.uint32) < WINDOW, s_t, NEG)
    e_w = jnp.exp2(sm - jnp.max(sm, axis=0, keepdims=True))
    vw_t = jnp.concatenate([vwt_ref[0, h, first + t] for t in range(n_wt)], axis=1)
    acc_w = _dot(vw_t, e_w.astype(BF16))
    return acc_w[HEAD_DIM:] * (1.0 / acc_w[0:1])


def _nsa(p, r, kvcmp, kvcmp_t, vs_t, vw_t, cmp_to_slc_t, e3, rel, crel):
    B, S, _ = p.shape
    qb = NSA_QB
    n_top = min(SLC_TOPN, S // SLC_BLOCK)
    nq = NSA_GROUP * qb
    kvw = NSA_KV_HEADS * LANES
    assert G_NSA_Q == 0 and G_SLC * LANES % kvw == 0 and G_WIN * LANES % kvw == 0
    const = lambda a: pl.BlockSpec(a.shape, lambda b, i: (0,) * a.ndim)
    per_b = lambda a: pl.BlockSpec((1,) + a.shape[1:], lambda b, i: (b,) + (0,) * (a.ndim - 1))
    return pl.pallas_call(
        functools.partial(_nsa_kernel, n_pick=max(n_top - 3, 0)),
        grid=(B, S // qb),
        in_specs=[
            pl.BlockSpec((1, qb, NSA_HEADS * LANES), lambda b, i: (b, i, 0)),
            per_b(kvcmp), per_b(kvcmp_t),
            pl.BlockSpec((1, S, kvw), lambda b, i: (b, 0, G_SLC * LANES // kvw)),
            per_b(vs_t),
            pl.BlockSpec((1, S, kvw), lambda b, i: (b, 0, G_WIN * LANES // kvw)),
            per_b(vw_t),
            pl.BlockSpec((1, qb, LANES), lambda b, i: (b, i, R_GATE_BLOCK)),
            pl.BlockSpec((1, qb, LANES), lambda b, i: (b, i, R_GATE_BLOCK + 1)),
            const(cmp_to_slc_t), const(e3), const(rel), const(crel),
        ],
        out_specs=pl.BlockSpec((1, qb, NSA_KV_HEADS * NSA_OUT_LANES), lambda b, i: (b, i, 0)),
        out_shape=jax.ShapeDtypeStruct((B, S, NSA_KV_HEADS * NSA_OUT_LANES), F32),
        scratch_shapes=sum((_attend_scratch(NSA_TK, nq) for _ in range(NSA_KV_HEADS)), []),
        compiler_params=_cparams(("arbitrary", "arbitrary")),
        name="nsa_attention",
    )(p, kvcmp, kvcmp_t, p, vs_t, p, vw_t, r, r, cmp_to_slc_t, e3, rel, crel)


def _nsa_constants(S):
    n_cmp = (S - CMP_BLOCK) // CMP_STRIDE + 1
    nc = S // CMP_STRIDE
    n_slc = S // SLC_BLOCK
    c_start = np.arange(nc) * CMP_STRIDE
    s_start = np.arange(LANES) * SLC_BLOCK
    overlap = (np.minimum(c_start[None, :] + CMP_BLOCK, s_start[:, None] + SLC_BLOCK)
               - np.maximum(c_start[None, :], s_start[:, None]))
    m_t = np.clip(overlap, 0, None).astype(np.float32) / CMP_BLOCK
    m_t = np.where((np.arange(nc)[None, :] < n_cmp) & (np.arange(LANES)[:, None] < n_slc), m_t, 0.0)
    key_blk = (np.arange(S) // SLC_BLOCK).reshape(S // NSA_TK, NSA_TK, 1)
    e3 = (key_blk == np.arange(LANES)[None, None, :]).astype(np.float32)
    lane_q = np.arange(NSA_GROUP * NSA_QB) % NSA_QB
    rel = np.arange(NSA_WIN_SPAN)[:, None] - lane_q[None, :]
    crel = (np.arange(nc) * CMP_STRIDE + CMP_BLOCK - 1)[:, None] - lane_q[None, :]
    return jnp.asarray(m_t, BF16), jnp.asarray(e3, BF16), jnp.asarray(rel, jnp.int32), jnp.asarray(crel, jnp.int32)


def _moba_kernel(q_ref, kv_ref, vt_ref, km_ref, o_ref, sel_scr, *attend_scratch):
    ci = pl.program_id(2)
    qb = MOBA_BLOCK
    n_scr = len(attend_scratch) // MOBA_HEADS_PER_STEP

    def causal_edge(s_t):
        causal = lax.broadcasted_iota(jnp.int32, s_t.shape, 0) <= lax.broadcasted_iota(jnp.int32, s_t.shape, 1)
        return jnp.where(causal, s_t, NEG)

    def head_stream(h):
        lanes = slice(h * LANES, (h + 1) * LANES)
        q = q_ref[0, :, lanes]
        km_hi, km_lo = _split2(km_ref[0, :, lanes])
        gate_t = _dot_nt(km_hi, q) + _dot_nt(km_lo, q)
        blk = lax.broadcasted_iota(jnp.int32, gate_t.shape, 0)
        valid = blk < ci
        sc = jnp.where(valid, gate_t, -jnp.inf)
        picked = jnp.zeros(gate_t.shape, F32)
        for _ in range(MOBA_TOPK):
            best = jnp.max(sc, axis=0, keepdims=True)
            idx = jnp.min(jnp.where(sc == best, blk, gate_t.shape[0]), axis=0, keepdims=True)
            pick = blk == idx
            picked = jnp.where(pick, 1.0, picked)
            sc = jnp.where(pick, -jnp.inf, sc)
        sel = jnp.where(valid, picked, jnp.where(blk == ci, 1.0, 0.0))
        sel_scr[h] = jnp.where(sel > 0.5, 0.0, NEG)

        def scores(j):
            kvj = kv_ref[0, pl.ds(pl.multiple_of(j * qb, qb), qb), lanes]
            return _dot_nt(kvj, q) + sel_scr[h, pl.ds(j, 1), :]

        return scores, causal_edge, lambda j: vt_ref[0, h, j], attend_scratch[h * n_scr:(h + 1) * n_scr]

    accs = _attend_tiles(ci, [head_stream(h) for h in range(MOBA_HEADS_PER_STEP)])
    for h, acc in enumerate(accs):
        o_ref[0, :, h * LANES:(h + 1) * LANES] = (acc * (1.0 / acc[0:1])).T


def _moba(p, mv_t, kmean):
    B, S, _ = p.shape
    qb = MOBA_BLOCK
    hp = MOBA_HEADS_PER_STEP
    w = hp * LANES
    return pl.pallas_call(
        _moba_kernel,
        grid=(B, MOBA_HEADS // hp, S // qb),
        in_specs=[
            pl.BlockSpec((1, qb, w), lambda b, h, i: (b, i, G_MOBA_Q // hp + h)),
            pl.BlockSpec((1, S, w), lambda b, h, i: (b, 0, G_MOBA_KV // hp + h)),
            pl.BlockSpec((1, hp) + mv_t.shape[2:], lambda b, h, i: (b, h, 0, 0, 0)),
            pl.BlockSpec((1, MOBA_NBP, w), lambda b, h, i: (b, 0, h)),
        ],
        out_specs=pl.BlockSpec((1, qb, w), lambda b, h, i: (b, i, h)),
        out_shape=jax.ShapeDtypeStruct((B, S, MOBA_HEADS * LANES), F32),
        scratch_shapes=[pltpu.VMEM((hp, MOBA_NBP, qb), F32)] + sum((_attend_scratch(qb, qb) for _ in range(hp)), []),
        compiler_params=_cparams(("arbitrary", "arbitrary", "arbitrary")),
        name="moba_attention",
    )(p, p, mv_t, kmean)


def _rwkv_kernel(f_ref, mu_ref, wup_ref, aup_ref, gup_ref, vec_ref, bd_ref, tri_ref, o_ref, carry_scr, st_scr):
    i = pl.program_id(1)
    tb = f_ref.shape[1]
    C = RWKV_CHUNK
    W = RWKV_WIDTH

    @pl.when(i == 0)
    def _():
        carry_scr[...] = jnp.zeros(carry_scr.shape, F32)
        st_scr[...] = jnp.zeros(st_scr.shape, F32)

    feat = f_ref[0]
    rowi = lax.broadcasted_iota(jnp.int32, feat.shape, 0)
    prev = jnp.where(rowi == 0, carry_scr[0:1, :], pltpu.roll(feat, 1, 0))
    carry_scr[0:1, :] = feat[tb - 1:tb, :]
    xs = feat + (prev - feat) * mu_ref[...]
    r = xs[:, 0:W]
    k = xs[:, W:2 * W]
    v = xs[:, 2 * W:3 * W]
    wa = xs[:, 3 * W:3 * W + DECAY_LORA + AAA_LORA]
    gd = xs[:, 3 * W + DECAY_LORA + AAA_LORA:]
    w0, a0, k_k, k_a, r_k, ln_w, ln_b = (vec_ref[n:n + 1, :] for n in range(7))
    bd = bd_ref[...]

    def hsum(t):
        hi, lo = _split2(t)
        return _dot(hi, bd) + _dot(lo, bd)

    logw = -DECAY_SCALE * jax.nn.sigmoid(w0 + _mm3(jnp.tanh(wa), wup_ref[...]))
    a = jax.nn.sigmoid(a0 + _mm1(wa, aup_ref[...]))
    gate = _mm1(jax.nn.sigmoid(gd), gup_ref[...])
    kk = k * k_k
    kk = kk / jnp.maximum(jnp.sqrt(hsum(kk * kk)), 1e-12)
    k = k * (1.0 + (a - 1.0) * k_a)
    bonus = hsum(r * k * r_k) * v
    kka = kk * a

    ri = lax.broadcasted_iota(jnp.int32, (tb, tb), 0)
    cj = lax.broadcasted_iota(jnp.int32, (tb, tb), 1)
    same = (ri // C) == (cj // C)
    strict = same & (cj < ri)
    incl = same & (cj <= ri)
    eye = jnp.where(ri == cj, 1.0, 0.0)
    eye_c = eye[0:C, 0:C]
    hi, mid, lo = _split3(logw)
    tri = tri_ref[0]
    blk = tri_ref[1]
    cum = _dot(tri, hi) + _dot(tri, mid) + _dot(tri, lo)
    tot = _dot(blk, hi) + _dot(blk, mid) + _dot(blk, lo)
    e_neg = jnp.exp(-cum)
    e_end = jnp.exp(tot - cum)
    d_end = jnp.exp(tot)
    a_t = -kk * jnp.exp(cum - logw)
    r_t = r * jnp.exp(cum)
    b_t = kka * e_neg
    k_t = k * e_neg
    b_e = kka * e_end
    k_e = k * e_end
    heads = range(RWKV_HEADS)
    hs = [slice(h * HEAD_DIM, (h + 1) * HEAD_DIM) for h in heads]
    bf = lambda t: t.astype(BF16)
    ah = [bf(a_t[:, s]) for s in hs]
    rh = [r_t[:, s] for s in hs]
    vh = [bf(v[:, s]) for s in hs]
    ar = [jnp.concatenate([ah[h], bf(rh[h])], axis=0) for h in heads]
    xb = [_dot_nt(ar[h], bf(b_t[:, hs[h]])) for h in heads]
    xk = [_dot_nt(ar[h], bf(k_t[:, hs[h]])) for h in heads]
    n = [bf(jnp.where(strict, xb[h][0:tb], 0.0)) for h in heads]
    m_ak = [bf(jnp.where(strict, xk[h][0:tb], 0.0)) for h in heads]
    m_rb = [bf(jnp.where(incl, xb[h][tb:], 0.0)) for h in heads]
    m_rk = [bf(jnp.where(incl, xk[h][tb:], 0.0)) for h in heads]
    tinv = [eye + n[h].astype(F32) for h in heads]
    npow = n
    step = 1
    while 2 * step < C:
        npow = [bf(_dot(npow[h], npow[h])) for h in heads]
        tinv = [tinv[h] + _dot(bf(tinv[h]), npow[h]) for h in heads]
        step *= 2
    tinv = [bf(t) for t in tinv]
    g = [_dot(tinv[h], ah[h]) for h in heads]
    u0 = [_dot(tinv[h], bf(_dot(m_ak[h], vh[h]))) for h in heads]
    gb = [bf(t) for t in g]
    ub = [bf(t) for t in u0]
    r_y = [rh[h] + _dot(m_rb[h], gb[h]) for h in heads]
    y0 = [_dot(m_rb[h], ub[h]) + _dot(m_rk[h], vh[h]) for h in heads]
    beh = [bf(b_e[:, s]) for s in hs]
    keh = [bf(k_e[:, s]) for s in hs]
    st = [st_scr[h] for h in heads]
    yh = [[] for _ in heads]
    for c in range(tb // C):
        cs = slice(c * C, (c + 1) * C)
        p_st = [eye_c * d_end[c * C:c * C + 1, hs[h]] + _dot_tn(beh[h][cs], gb[h][cs]) for h in heads]
        q_st = [_dot_tn(beh[h][cs], ub[h][cs]) + _dot_tn(keh[h][cs], vh[h][cs]) for h in heads]
        for h in heads:
            yh[h].append(_mm3(r_y[h][cs], st[h]) + y0[h][cs])
        st = [_mm3(p_st[h], st[h]) + q_st[h] for h in heads]
    ys = []
    for h in heads:
        st_scr[h] = st[h]
        y = jnp.concatenate(yh[h], axis=0)
        mean = jnp.mean(y, axis=-1, keepdims=True)
        yc = y - mean
        var = jnp.mean(yc * yc, axis=-1, keepdims=True)
        ys.append(yc * lax.rsqrt(var + GN_EPS))
    yn = jnp.concatenate(ys, axis=1)
    o_ref[0] = (yn * ln_w + ln_b + bonus) * gate


def _rwkv(r, mu, wup, aup, gup, vecs, bd, tri):
    B, S, _ = r.shape
    tb = RWKV_TB
    full = lambda a: pl.BlockSpec(a.shape, lambda b, i: (0,) * a.ndim)
    return pl.pallas_call(
        _rwkv_kernel,
        grid=(B, S // tb),
        in_specs=[pl.BlockSpec((1, tb, RWKV_COLS), lambda b, i: (b, i, 0)),
                  full(mu), full(wup), full(aup), full(gup), full(vecs), full(bd), full(tri)],
        out_specs=pl.BlockSpec((1, tb, RWKV_WIDTH), lambda b, i: (b, i, 0)),
        out_shape=jax.ShapeDtypeStruct((B, S, RWKV_WIDTH), F32),
        scratch_shapes=[pltpu.VMEM((8, RWKV_COLS), F32), pltpu.VMEM((RWKV_HEADS, HEAD_DIM, HEAD_DIM), F32)],
        compiler_params=_cparams(("arbitrary", "arbitrary")),
        name="rwkv7",
    )(r, mu, wup, aup, gup, vecs, bd, tri)


def _rwkv_constants():
    head = np.arange(RWKV_WIDTH) // HEAD_DIM
    bd = (head[:, None] == head[None, :]).astype(np.float32)
    t = np.arange(RWKV_TB)
    same = (t[None, :] // RWKV_CHUNK) == (t[:, None] // RWKV_CHUNK)
    tri = np.stack([same & (t[None, :] <= t[:, None]), same]).astype(np.float32)
    return jnp.asarray(bd, BF16), jnp.asarray(tri, BF16)


def _outproj_kernel(mn_ref, mm_ref, rw_ref, x_ref, g1_ref, gn_ref, gm_ref, wn_ref, wm_ref, wr_ref, o_ref):
    def head_norm(o, gain_ref, width):
        o = jnp.where(gain_ref[1:2, :] > 0.5, o, 0.0)
        ms = jnp.sum(o * o, axis=-1, keepdims=True) * (1.0 / width)
        return (o * lax.rsqrt(ms + NORM_EPS) * gain_ref[0:1, :]).astype(BF16)

    z = _dot(head_norm(mn_ref[0], gn_ref, NSA_WIDTH), wn_ref[...])
    z = z + _dot(head_norm(mm_ref[0], gm_ref, MOBA_WIDTH), wm_ref[...])
    z = z + _dot(rw_ref[0].astype(BF16), wr_ref[...])
    o_ref[0] = x_ref[0] + g1_ref[0] * z


def _outproj(mixn, mixm, orw, x, g1, gn, gm, wn, wm, wr):
    B, S, D = x.shape
    tm = 512
    full = lambda a: pl.BlockSpec(a.shape, lambda b, i: (0,) * a.ndim)
    tok = lambda a: pl.BlockSpec((1, tm, a.shape[2]), lambda b, i: (b, i, 0))
    return pl.pallas_call(
        _outproj_kernel,
        grid=(B, S // tm),
        in_specs=[tok(mixn), tok(mixm), tok(orw), tok(x), pl.BlockSpec((1, 1, D), lambda b, i: (b, 0, 0)),
                  full(gn), full(gm), full(wn), full(wm), full(wr)],
        out_specs=tok(x),
        out_shape=jax.ShapeDtypeStruct((B, S, D), F32),
        compiler_params=_cparams(("arbitrary", "arbitrary")),
        name="outproj",
    )(mixn, mixm, orw, x, g1, gn, gm, wn, wm, wr)


def _pad_moba(t):
    t = t.reshape((MOBA_HEADS, HEAD_DIM) + t.shape[1:])
    return jnp.concatenate([jnp.zeros_like(t), t], axis=1).reshape((MOBA_HEADS * LANES,) + t.shape[2:])


def _pad_nsa(t):
    w = NSA_GROUP * HEAD_DIM
    t = t.reshape((NSA_KV_HEADS, w) + t.shape[1:])
    pad = jnp.zeros((NSA_KV_HEADS, NSA_OUT_LANES - w) + t.shape[2:], t.dtype)
    return jnp.concatenate([t, pad], axis=1).reshape((NSA_KV_HEADS * NSA_OUT_LANES,) + t.shape[2:])


def _gain_and_mask(gain, pad):
    return jnp.stack([pad(gain), pad(jnp.ones_like(gain))])


def _route(logit_t, bias_col):
    aff = jax.nn.sigmoid(logit_t)
    biased = aff + bias_col
    row = lambda t, e: t[e:e + 1, :]
    gp = EXPERTS_PER_GROUP
    scores = []
    for g in range(N_EXPERT_GROUPS):
        a_, b_, c_, d_ = (row(biased, g * gp + j) for j in range(gp))
        hi1, lo1, hi2, lo2 = jnp.maximum(a_, b_), jnp.minimum(a_, b_), jnp.maximum(c_, d_), jnp.minimum(c_, d_)
        scores.append(jnp.maximum(hi1, hi2) + jnp.maximum(jnp.minimum(hi1, hi2), jnp.maximum(lo1, lo2)))
    best = jnp.zeros_like(scores[0], dtype=jnp.int32)
    best_s = scores[0]
    for g in range(1, N_EXPERT_GROUPS):
        better = scores[g] > best_s
        best = jnp.where(better, g, best)
        best_s = jnp.where(better, scores[g], best_s)

    def in_group(t, j):
        out = row(t, j)
        for g in range(1, N_EXPERT_GROUPS):
            out = jnp.where(best == g, row(t, g * gp + j), out)
        return out

    vals = [in_group(biased, j) for j in range(gp)]
    affs = [in_group(aff, j) for j in range(gp)]

    def first_argmax(vs):
        top = functools.reduce(jnp.maximum, vs)
        idx = jnp.full(top.shape, gp, jnp.int32)
        for j in reversed(range(gp)):
            idx = jnp.where(vs[j] == top, j, idx)
        return idx

    i1 = first_argmax(vals)
    i2 = first_argmax([jnp.where(i1 == j, -jnp.inf, vals[j]) for j in range(gp)])
    pick = lambda idx: functools.reduce(lambda acc, j: jnp.where(idx == j, affs[j], acc), range(gp), jnp.zeros_like(affs[0]))
    w1, w2 = pick(i1), pick(i2)
    tot = w1 + w2
    e_iota = lax.broadcasted_iota(jnp.int32, logit_t.shape, 0)
    e1 = best * gp + i1
    e2 = best * gp + i2
    return jnp.where(e_iota == e1, w1 / tot, 0.0) + jnp.where(e_iota == e2, w2 / tot, 0.0)


def _moe_kernel(x_ref, gain_ref, sc_ref, sh_ref, g2_ref, wrt_ref, rb_ref, wg_ref, wu_ref, wd_ref, fin_ref,
                o_ref, hb_scr, cb_scr, acc_scr, *, final):
    e = pl.program_id(1)
    tm = x_ref.shape[1]

    @pl.when(e == 0)
    def _():
        x = x_ref[0]
        ms = jnp.mean(x * x, axis=-1, keepdims=True)
        h = x * lax.rsqrt(ms + NORM_EPS) * gain_ref[...]
        h = h * (1.0 + sc_ref[0]) + sh_ref[0]
        hb_scr[...] = h.astype(BF16)
        comb = _route(_mm3(wrt_ref[...], h, _dot_nt), rb_ref[...])
        cb_scr[...] = jnp.concatenate([comb, jnp.zeros((LANES - N_EXPERTS, tm), F32)], axis=0).T
        acc_scr[...] = jnp.zeros(acc_scr.shape, F32)

    hb = hb_scr[...]
    comb_t = cb_scr[...]
    hes = []
    for j in range(MOE_EXPERTS_PER_STEP):
        hg = _dot(hb, wg_ref[j])
        hu = _dot(hb, wu_ref[j])
        cbe = jnp.broadcast_to(comb_t[:, j:j + 1], (tm, LANES))
        he = hg * jax.nn.sigmoid(hg) * hu * jnp.concatenate([cbe] * (EXPERT_FF // LANES), axis=1)
        hes.append(he.astype(BF16))
    cb_scr[...] = pltpu.roll(comb_t, LANES - MOE_EXPERTS_PER_STEP, 1)
    acc_scr[...] += _dot(jnp.concatenate(hes, axis=1), wd_ref[...])

    @pl.when(e == N_EXPERTS // MOE_EXPERTS_PER_STEP - 1)
    def _():
        out = x_ref[0] + g2_ref[0] * acc_scr[...]
        if final:
            ms = jnp.mean(out * out, axis=-1, keepdims=True)
            out = out * lax.rsqrt(ms + NORM_EPS) * fin_ref[...]
        o_ref[0] = out


def _moe(x, gain, sc, sh, g2, wrt, rb, wg, wu, wd, fin, final):
    B, S, D = x.shape
    tm = 1024
    tpb = S // tm
    eps = MOE_EXPERTS_PER_STEP
    wd = wd.reshape(N_EXPERTS * EXPERT_FF, D)
    tok = pl.BlockSpec((1, tm, D), lambda i, e: (i // tpb, i % tpb, 0))
    per_b = pl.BlockSpec((1, 1, D), lambda i, e: (i // tpb, 0, 0))
    full = lambda a: pl.BlockSpec(a.shape, lambda i, e: (0,) * a.ndim)
    return pl.pallas_call(
        functools.partial(_moe_kernel, final=final),
        grid=(B * tpb, N_EXPERTS // eps),
        in_specs=[tok, full(gain), per_b, per_b, per_b, full(wrt), full(rb),
                  pl.BlockSpec((eps, D, EXPERT_FF), lambda i, e: (e, 0, 0)),
                  pl.BlockSpec((eps, D, EXPERT_FF), lambda i, e: (e, 0, 0)),
                  pl.BlockSpec((eps * EXPERT_FF, D), lambda i, e: (e, 0)),
                  full(fin)],
        out_specs=tok,
        out_shape=jax.ShapeDtypeStruct((B, S, D), F32),
        scratch_shapes=[pltpu.VMEM((tm, D), BF16), pltpu.VMEM((tm, LANES), F32), pltpu.VMEM((tm, D), F32)],
        compiler_params=_cparams(("arbitrary", "arbitrary")),
        name="moe",
    )(x, gain, sc, sh, g2, wrt, rb, wg, wu, wd, fin)


def kernel(x, c, positions, w_mod, b_mod, norm_mix, norm_ffn, w_in, nsa_phi_w1, nsa_phi_w2, nsa_phi_pos, rwkv_mu, rwkv_w_up, rwkv_w0, rwkv_a_up, rwkv_a0, rwkv_g_up, rwkv_k_k, rwkv_k_a, rwkv_r_k, rwkv_ln_w, rwkv_ln_b, norm_nsa_out, norm_moba_out, w_out, w_router, router_bias, moe_w_gate, moe_w_up, moe_w_down, norm_final):
    B, S, D = x.shape
    depth = w_in.shape[0]
    assert S % NSA_TK == 0 and S % MOBA_BLOCK == 0 and S % RWKV_TB == 0 and S >= WINDOW + NSA_QB
    assert S // SLC_BLOCK <= LANES and S // MOBA_BLOCK <= MOBA_NBP

    inv = ROPE_THETA ** (-jnp.arange(0, HEAD_DIM, 2, dtype=F32) / HEAD_DIM)
    ang = positions.astype(F32)[..., None] * inv
    cos, sin = jnp.cos(ang), jnp.sin(ang)
    one, zero = jnp.ones((B, S, HEAD_DIM), F32), jnp.zeros((B, S, HEAD_DIM), F32)
    cos_t = jnp.concatenate([cos, cos, one], axis=-1)
    sin_s = jnp.concatenate([-sin, sin, zero], axis=-1)
    nc = S // CMP_STRIDE
    n_cmp = (S - CMP_BLOCK) // CMP_STRIDE + 1
    pad_c = lambda t, fill: jnp.concatenate([t[:, CMP_BLOCK - 1::CMP_STRIDE][:, :n_cmp],
                                             jnp.full((B, nc - n_cmp, LANES), fill, F32)], axis=1)
    cos_c, sin_c = pad_c(cos_t, 1.0), pad_c(sin_s, 0.0)

    nsa_consts = _nsa_constants(S)
    bd, tri = _rwkv_constants()
    mod = _modulation(c, w_mod, b_mod)
    wrt = w_router.T
    rb = router_bias.reshape(N_EXPERTS, 1)
    fin = norm_final.reshape(1, D)

    for l in range(depth):
        sh1, sc1, g1, sh2, sc2, g2 = (mod[l, :B, i * D:(i + 1) * D].reshape(B, 1, D) for i in range(6))
        p, r, km, vs_t, vw_t, mv_t = _inproj(x, norm_mix[l].reshape(1, D), sc1, sh1, cos_t, sin_s, _prep_w_in(w_in[l]))

        w1, w2, pos4 = _prep_compress(nsa_phi_w1[l], nsa_phi_w2[l], nsa_phi_pos[l])
        chunks = lambda g: p[:, :, g * LANES:(g + 1) * LANES].reshape(B, nc, CMP_STRIDE * LANES)
        kvcmp, kvcmp_t = _compress(chunks(G_KC), chunks(G_VC), pos4, w1, w2, cos_c, sin_c)
        mixn = _nsa(p, r, kvcmp, kvcmp_t, vs_t, vw_t, *nsa_consts)

        kmean = km[:, :, 0, :]
        kmean = jnp.concatenate([kmean, jnp.zeros((B, MOBA_NBP - kmean.shape[1], kmean.shape[2]), F32)], axis=1)
        mixm = _moba(p, mv_t, kmean)

        zl = jnp.zeros((DECAY_LORA, RWKV_WIDTH), F32)
        wup = jnp.concatenate([rwkv_w_up[l], zl], axis=0)
        aup = jnp.concatenate([zl, rwkv_a_up[l]], axis=0)
        vecs = jnp.stack([rwkv_w0[l], rwkv_a0[l], rwkv_k_k[l], rwkv_k_a[l], rwkv_r_k[l].reshape(-1),
                          rwkv_ln_w[l], rwkv_ln_b[l], jnp.zeros((RWKV_WIDTH,), F32)])
        orw = _rwkv(r, rwkv_mu[l].reshape(1, -1), wup, aup, rwkv_g_up[l], vecs, bd, tri)

        wo = w_out[l]
        x = _outproj(mixn, mixm, orw, x, g1,
                     _gain_and_mask(norm_nsa_out[l], _pad_nsa),
                     _gain_and_mask(norm_moba_out[l], _pad_moba),
                     _pad_nsa(wo[:NSA_WIDTH]).astype(BF16),
                     _pad_moba(wo[NSA_WIDTH:NSA_WIDTH + MOBA_WIDTH]).astype(BF16),
                     wo[NSA_WIDTH + MOBA_WIDTH:].astype(BF16))
        x = _moe(x, norm_ffn[l].reshape(1, D), sc2, sh2, g2, wrt, rb,
                 moe_w_gate[l].astype(BF16), moe_w_up[l].astype(BF16), moe_w_down[l].astype(BF16),
                 fin, final=(l == depth - 1))
    return x
```

```python
import functools

import jax
import jax.numpy as jnp
import numpy as np
from jax import lax
from jax.experimental import pallas as pl
from jax.experimental.pallas import tpu as pltpu

F32 = jnp.float32
BF16 = jnp.bfloat16

HEAD_DIM = 64
LANES = 128
ROPE_THETA = 10000.0
NORM_EPS = 1e-6
NEG = -1e30
LOG2_E = 1.4426950408889634

NSA_HEADS = 6
NSA_KV_HEADS = 2
NSA_GROUP = NSA_HEADS // NSA_KV_HEADS
NSA_WIDTH = NSA_HEADS * HEAD_DIM
NSA_KV_WIDTH = NSA_KV_HEADS * HEAD_DIM
CMP_BLOCK = 32
CMP_STRIDE = 16
SLC_BLOCK = 64
SLC_TOPN = 16
WINDOW = 512
NSA_QB = 256
NSA_TK = 256
NSA_OUT_LANES = 256
NSA_WIN_SPAN = WINDOW + NSA_QB

MOBA_HEADS = 4
MOBA_WIDTH = MOBA_HEADS * HEAD_DIM
MOBA_BLOCK = 256
MOBA_TOPK = 3
MOBA_NBP = 32
MOBA_HEADS_PER_STEP = 4

RWKV_HEADS = 6
RWKV_WIDTH = RWKV_HEADS * HEAD_DIM
DECAY_LORA = 64
AAA_LORA = 64
GATE_LORA = 128
RWKV_COLS = 3 * RWKV_WIDTH + DECAY_LORA + AAA_LORA + GATE_LORA
DECAY_SCALE = 0.606531
GN_EPS = 64e-5
RWKV_CHUNK = 64
RWKV_TB = 256

N_EXPERTS = 16
N_EXPERT_GROUPS = 4
EXPERTS_PER_GROUP = N_EXPERTS // N_EXPERT_GROUPS
EXPERT_FF = 256
MOE_EXPERTS_PER_STEP = 4

G_NSA_Q = 0
G_SLC = 6
G_WIN = 8
G_KC = 10
G_VC = 11
G_MOBA_Q = 12
G_MOBA_KV = 16
N_GROUPS = 20
ROPED_GROUPS = tuple(range(0, 10)) + tuple(range(12, 20))
P_COLS = N_GROUPS * LANES
R_GATE_BLOCK = RWKV_COLS // LANES
R_COLS = RWKV_COLS + NSA_KV_HEADS * LANES

VMEM_LIMIT = 56 * 1024 * 1024


def _cparams(sem):
    return pltpu.CompilerParams(dimension_semantics=sem, vmem_limit_bytes=VMEM_LIMIT)


def _dot(a, b):
    return jnp.dot(a, b, preferred_element_type=F32)


def _dot_nt(a, b):
    return lax.dot_general(a, b, (((1,), (1,)), ((), ())), preferred_element_type=F32)


def _dot_tn(a, b):
    return lax.dot_general(a, b, (((0,), (0,)), ((), ())), preferred_element_type=F32)


def _split2(x):
    hi = x.astype(BF16)
    lo = (x - hi.astype(F32)).astype(BF16)
    return hi, lo


def _split3(x):
    hi = x.astype(BF16)
    r1 = x - hi.astype(F32)
    mid = r1.astype(BF16)
    lo = (r1 - mid.astype(F32)).astype(BF16)
    return hi, mid, lo


def _mm3(a, b, dot=_dot):
    ah, al = _split2(a)
    bh, bl = _split2(b)
    return dot(ah, bh) + dot(ah, bl) + dot(al, bh)


def _mm1(a, b, dot=_dot):
    return dot(a.astype(BF16), b.astype(BF16))


def _mm3_exact_rhs(a, b_bf16):
    hi, mid, lo = _split3(a)
    return _dot(hi, b_bf16) + _dot(mid, b_bf16) + _dot(lo, b_bf16)


def _rope(y, cos_t, sin_s):
    lane = lax.broadcasted_iota(jnp.int32, y.shape, 1)
    rot = jnp.where(lane < HEAD_DIM // 2, pltpu.roll(y, LANES - HEAD_DIM // 2, 1), pltpu.roll(y, HEAD_DIM // 2, 1))
    return y * cos_t + rot * sin_s


def _mod_kernel(c_ref, w_ref, b_ref, o_ref):
    c = c_ref[...]
    ca = c * jax.nn.sigmoid(c)
    o_ref[0] = _dot(ca.astype(BF16), w_ref[0].astype(BF16)) + b_ref[0]


def _modulation(c, w_mod, b_mod):
    B, D = c.shape
    L, _, N = w_mod.shape
    tn = 512
    c8 = jnp.zeros((8, D), F32).at[:B].set(c)
    return pl.pallas_call(
        _mod_kernel,
        grid=(L, N // tn),
        in_specs=[
            pl.BlockSpec((8, D), lambda l, j: (0, 0)),
            pl.BlockSpec((1, D, tn), lambda l, j: (l, 0, j)),
            pl.BlockSpec((1, 1, tn), lambda l, j: (l, 0, j)),
        ],
        out_specs=pl.BlockSpec((1, 8, tn), lambda l, j: (l, 0, j)),
        out_shape=jax.ShapeDtypeStruct((L, 8, N), F32),
        compiler_params=_cparams(("arbitrary", "arbitrary")),
        name="modulation",
    )(c8, w_mod, b_mod.reshape(L, 1, N))


def _ones_over_values_t(y):
    yt = y.T
    row = lax.broadcasted_iota(jnp.int32, yt.shape, 0)
    return jnp.where(row < HEAD_DIM, 1.0, yt).astype(BF16)


def _inproj_kernel(x_ref, gain_ref, sc_ref, sh_ref, cos_ref, sin_ref, w_ref, p_ref, r_ref, km_ref,
                   vs_ref, vw_ref, mv_ref):
    x = x_ref[0]
    ms = jnp.mean(x * x, axis=-1, keepdims=True)
    h = x * lax.rsqrt(ms + NORM_EPS) * gain_ref[...]
    h = h * (1.0 + sc_ref[0]) + sh_ref[0]
    hb = h.astype(BF16)
    cos_t = cos_ref[0]
    sin_s = sin_ref[0]
    per = 4
    for g0 in range(0, N_GROUPS, per):
        y4 = _dot(hb, w_ref[:, g0 * LANES:(g0 + per) * LANES])
        for j in range(per):
            g = g0 + j
            y = y4[:, j * LANES:(j + 1) * LANES]
            if g in ROPED_GROUPS:
                y = _rope(y, cos_t, sin_s)
            if g >= G_MOBA_KV:
                km = jnp.mean(y, axis=0, keepdims=True)
                km_ref[0, 0, :, (g - G_MOBA_KV) * LANES:(g - G_MOBA_KV + 1) * LANES] = jnp.broadcast_to(km, (8, LANES))
                mv_ref[0, g - G_MOBA_KV, 0] = _ones_over_values_t(y)
            elif G_SLC <= g < G_WIN:
                vs_ref[0, g - G_SLC, 0] = _ones_over_values_t(y)
            elif G_WIN <= g < G_KC:
                yt = _ones_over_values_t(y)
                for t in range(y.shape[0] // NSA_QB):
                    vw_ref[0, g - G_WIN, t] = yt[:, t * NSA_QB:(t + 1) * NSA_QB]
            p_ref[0, :, g * LANES:(g + 1) * LANES] = y.astype(BF16)
    r_ref[0] = _dot(hb, w_ref[:, P_COLS:])


def _inproj(x, gain, sc, sh, cos_t, sin_s, w):
    B, S, D = x.shape
    tm = MOBA_BLOCK
    return pl.pallas_call(
        _inproj_kernel,
        grid=(B, S // tm),
        in_specs=[
            pl.BlockSpec((1, tm, D), lambda b, i: (b, i, 0)),
            pl.BlockSpec((1, D), lambda b, i: (0, 0)),
            pl.BlockSpec((1, 1, D), lambda b, i: (b, 0, 0)),
            pl.BlockSpec((1, 1, D), lambda b, i: (b, 0, 0)),
            pl.BlockSpec((1, tm, LANES), lambda b, i: (b, i, 0)),
            pl.BlockSpec((1, tm, LANES), lambda b, i: (b, i, 0)),
            pl.BlockSpec((D, P_COLS + R_COLS), lambda b, i: (0, 0)),
        ],
        out_specs=[
            pl.BlockSpec((1, tm, P_COLS), lambda b, i: (b, i, 0)),
            pl.BlockSpec((1, tm, R_COLS), lambda b, i: (b, i, 0)),
            pl.BlockSpec((1, 1, 8, MOBA_HEADS * LANES), lambda b, i: (b, i, 0, 0)),
            pl.BlockSpec((1, NSA_KV_HEADS, 1, LANES, tm), lambda b, i: (b, 0, i, 0, 0)),
            pl.BlockSpec((1, NSA_KV_HEADS, tm // NSA_QB, LANES, NSA_QB), lambda b, i: (b, 0, i, 0, 0)),
            pl.BlockSpec((1, MOBA_HEADS, 1, LANES, tm), lambda b, i: (b, 0, i, 0, 0)),
        ],
        out_shape=[
            jax.ShapeDtypeStruct((B, S, P_COLS), BF16),
            jax.ShapeDtypeStruct((B, S, R_COLS), F32),
            jax.ShapeDtypeStruct((B, S // tm, 8, MOBA_HEADS * LANES), F32),
            jax.ShapeDtypeStruct((B, NSA_KV_HEADS, S // tm, LANES, tm), BF16),
            jax.ShapeDtypeStruct((B, NSA_KV_HEADS, S // NSA_QB, LANES, NSA_QB), BF16),
            jax.ShapeDtypeStruct((B, MOBA_HEADS, S // tm, LANES, tm), BF16),
        ],
        compiler_params=_cparams(("arbitrary", "arbitrary")),
        name="inproj",
    )(x, gain, sc, sh, cos_t, sin_s, w)


def _prep_w_in(w):
    D = w.shape[0]
    o = 0
    parts = {}
    for name, width in (("nq", NSA_WIDTH), ("nkc", NSA_KV_WIDTH), ("nvc", NSA_KV_WIDTH), ("nks", NSA_KV_WIDTH),
                        ("nvs", NSA_KV_WIDTH), ("nkw", NSA_KV_WIDTH), ("nvw", NSA_KV_WIDTH), ("ngate", NSA_HEADS * 3),
                        ("mq", MOBA_WIDTH), ("mk", MOBA_WIDTH), ("mv", MOBA_WIDTH), ("rf", RWKV_COLS)):
        parts[name] = w[:, o:o + width]
        o += width
    hd = lambda t, h: t[:, h * HEAD_DIM:(h + 1) * HEAD_DIM]
    z = jnp.zeros((D, HEAD_DIM), F32)
    scale = HEAD_DIM ** -0.5 * LOG2_E
    cols = []
    for h in range(NSA_HEADS):
        cols += [hd(parts["nq"], h) * scale, z]
    for h in range(NSA_KV_HEADS):
        cols += [hd(parts["nks"], h), hd(parts["nvs"], h)]
    for h in range(NSA_KV_HEADS):
        cols += [hd(parts["nkw"], h), hd(parts["nvw"], h)]
    cols += [parts["nkc"], parts["nvc"]]
    for h in range(MOBA_HEADS):
        cols += [hd(parts["mq"], h) * scale, z]
    for h in range(MOBA_HEADS):
        cols += [hd(parts["mk"], h), hd(parts["mv"], h)]
    cols += [parts["rf"]]
    per = NSA_GROUP * 3
    for h in range(NSA_KV_HEADS):
        cols += [parts["ngate"][:, h * per:(h + 1) * per], jnp.zeros((D, LANES - per), F32)]
    return jnp.concatenate(cols, axis=1).astype(BF16)


def _cmp_kernel(kc_ref, vc_ref, pos_ref, w1_ref, w2_ref, cos_ref, sin_ref, o_ref, ot_ref):
    nc = kc_ref.shape[1]

    def hidden(t_ref, ia, ib):
        t = t_ref[0].astype(F32)
        a = _dot((t + pos_ref[ia:ia + 1, :]).astype(BF16), w1_ref[ia])
        b = _dot((t + pos_ref[ib:ib + 1, :]).astype(BF16), w1_ref[ib])
        return jax.nn.gelu(a + pltpu.roll(b, nc - 1, 0)).astype(BF16)

    gk = hidden(kc_ref, 0, 1)
    gv = hidden(vc_ref, 2, 3)
    for h in range(NSA_KV_HEADS):
        y = _dot(gk, w2_ref[2 * h]) + _dot(gv, w2_ref[2 * h + 1])
        y = _rope(y, cos_ref[0], sin_ref[0])
        o_ref[0, h] = y.astype(BF16)
        ot_ref[0, h] = _ones_over_values_t(y)


def _compress(kc16, vc16, pos4, w1, w2, cos_c, sin_c):
    B, nc, K = kc16.shape
    return pl.pallas_call(
        _cmp_kernel,
        grid=(B,),
        in_specs=[
            pl.BlockSpec((1, nc, K), lambda b: (b, 0, 0)),
            pl.BlockSpec((1, nc, K), lambda b: (b, 0, 0)),
            pl.BlockSpec((4, K), lambda b: (0, 0)),
            pl.BlockSpec((4, K, LANES), lambda b: (0, 0, 0)),
            pl.BlockSpec((4, LANES, LANES), lambda b: (0, 0, 0)),
            pl.BlockSpec((1, nc, LANES), lambda b: (b, 0, 0)),
            pl.BlockSpec((1, nc, LANES), lambda b: (b, 0, 0)),
        ],
        out_specs=[pl.BlockSpec((1, NSA_KV_HEADS, nc, LANES), lambda b: (b, 0, 0, 0)),
                   pl.BlockSpec((1, NSA_KV_HEADS, LANES, nc), lambda b: (b, 0, 0, 0))],
        out_shape=[jax.ShapeDtypeStruct((B, NSA_KV_HEADS, nc, LANES), BF16),
                   jax.ShapeDtypeStruct((B, NSA_KV_HEADS, LANES, nc), BF16)],
        compiler_params=_cparams(("arbitrary",)),
        name="nsa_compress",
    )(kc16, vc16, pos4, w1, w2, cos_c, sin_c)


def _prep_compress(phi_w1, phi_w2, phi_pos):
    half = CMP_BLOCK // 2
    eye = jnp.eye(NSA_KV_HEADS, dtype=F32)
    w1, pos = [], []
    for t in range(2):
        for part in range(2):
            w = phi_w1[t, part * half:(part + 1) * half]
            w1.append(jnp.einsum("lde,kK->lkdKe", w, eye).reshape(half * NSA_KV_WIDTH, NSA_KV_WIDTH))
            p = phi_pos[t, part * half:(part + 1) * half]
            pos.append(jnp.broadcast_to(p[:, None, :], (half, NSA_KV_HEADS, HEAD_DIM)).reshape(-1))
    w2 = []
    for h in range(NSA_KV_HEADS):
        for t in range(2):
            m = jnp.zeros((LANES, LANES), F32)
            m = m.at[h * HEAD_DIM:(h + 1) * HEAD_DIM, t * HEAD_DIM:(t + 1) * HEAD_DIM].set(phi_w2[t])
            w2.append(m)
    return jnp.stack(w1).astype(BF16), jnp.stack(w2).astype(BF16), jnp.stack(pos)


def _attend_tiles(n_loop, streams, unroll):
    for _, _, _, (_, _, p_scr, alpha_scr, m_scr, acc_scr) in streams:
        m_scr[...] = jnp.full(m_scr.shape, -jnp.inf, F32)
        alpha_scr[...] = jnp.ones(alpha_scr.shape, F32)
        acc_scr[...] = jnp.zeros(acc_scr.shape, F32)
        p_scr[...] = jnp.zeros(p_scr.shape, BF16)

    def fetch(j, buf):
        for scores, _, _, scratch in streams:
            scratch[buf][...] = scores(j)

    def apply_weights(j):
        for _, _, values_t, (_, _, p_scr, alpha_scr, _, acc_scr) in streams:
            acc_scr[...] = alpha_scr[0:1, :] * acc_scr[...] + _dot(values_t(j), p_scr[...])

    def softmax_tiles(buf, last=False):
        for _, mask_last, _, scratch in streams:
            p_scr, alpha_scr, m_scr = scratch[2], scratch[3], scratch[4]
            s_t = scratch[buf][...]
            if last:
                s_t = mask_last(s_t)
            m_prev = m_scr[0:1, :]
            m_new = jnp.maximum(m_prev, jnp.max(s_t, axis=0, keepdims=True))
            alpha_scr[0:1, :] = jnp.exp2(m_prev - m_new)
            p_scr[...] = jnp.exp2(s_t - m_new).astype(BF16)
            m_scr[0:1, :] = m_new

    def step(j, cur, nxt):
        apply_weights(jnp.maximum(j - 1, 0))
        softmax_tiles(cur)
        fetch(j + 1, nxt)

    fetch(0, 0)

    assert unroll in (2, 4)

    def body(i, carry):
        for u in range(unroll):
            step(unroll * i + u, u % 2, (u + 1) % 2)
        return carry

    lax.fori_loop(0, n_loop // unroll, body, 0)
    if unroll == 4:
        done = n_loop - n_loop % unroll

        @pl.when(n_loop % unroll >= 2)
        def _():
            step(done, 0, 1)
            step(done + 1, 1, 0)

    @pl.when(n_loop % 2 == 1)
    def _():
        step(n_loop - 1, 0, 0)

    apply_weights(jnp.maximum(n_loop - 1, 0))
    softmax_tiles(0, last=True)
    apply_weights(n_loop)
    return [scratch[5][...] for _, _, _, scratch in streams]


def _attend_scratch(tk, nq):
    return [pltpu.VMEM((tk, nq), F32), pltpu.VMEM((tk, nq), F32), pltpu.VMEM((tk, nq), BF16),
            pltpu.VMEM((8, nq), F32), pltpu.VMEM((8, nq), F32), pltpu.VMEM((LANES, nq), F32)]


def _nsa_kernel(q_ref, kvc_ref, kvct_ref, kvs_ref, vst_ref, kvw_ref, vwt_ref, gate0_ref, gate1_ref, mt_ref,
                et_ref, rel_ref, crel_ref, o_ref, *attend_scratch, n_pick):
    ci = pl.program_id(1)
    qb = NSA_QB
    tk = NSA_TK
    last = (ci * qb) // tk
    heads = range(NSA_KV_HEADS)
    n_scr = len(attend_scratch) // NSA_KV_HEADS
    lanes = lambda g: slice(g * LANES, (g + 1) * LANES)
    q = [jnp.concatenate([q_ref[0, :, lanes(NSA_GROUP * h + g)] for g in range(NSA_GROUP)], axis=0)
         for h in heads]

    o_c, imp_t = [], []
    mt = mt_ref[...]
    for h in heads:
        sm = jnp.where(crel_ref[...] <= ci * qb, _dot_nt(kvc_ref[0, h], q[h]), NEG)
        mx = jnp.max(sm, axis=0, keepdims=True)
        e = jnp.exp2(sm - mx)
        den = jnp.maximum(jnp.sum(e, axis=0, keepdims=True), 1e-30)
        p_c = e * jnp.where(mx > 0.5 * NEG, 1.0 / den, 0.0)
        o_c.append(_dot(kvct_ref[0, h], p_c.astype(BF16)))
        hi, mid, lo = _split3(p_c[:, 0:qb] + p_c[:, qb:2 * qb] + p_c[:, 2 * qb:3 * qb])
        imp_t.append(_dot(mt, hi) + _dot(mt, mid) + _dot(mt, lo))

    blk = lax.broadcasted_iota(jnp.int32, imp_t[0].shape, 0)
    cur = (ci * qb + lax.broadcasted_iota(jnp.int32, imp_t[0].shape, 1)) // SLC_BLOCK
    forced = (blk == 0) | (blk == cur) | (blk == cur - 1)
    free = (blk <= cur) & jnp.logical_not(forced)
    sc = [jnp.where(free, imp_t[h], -1.0) for h in heads]
    for _ in range(n_pick):
        best = [jnp.max(sc[h], axis=0, keepdims=True) for h in heads]
        idx = [jnp.min(jnp.where(sc[h] == best[h], blk, blk.shape[0]), axis=0, keepdims=True) for h in heads]
        sc = [jnp.where(blk == idx[h], -2.0, sc[h]) for h in heads]
    w_nt = []
    for h in heads:
        bias = jnp.where(free, jnp.where(sc[h] < -1.5, 0.0, NEG), jnp.where(forced, 0.0, NEG)).T.astype(BF16)
        w_nt.append(jnp.concatenate([q[h], jnp.concatenate([bias] * NSA_GROUP, axis=0)], axis=1))
    o_w = [_nsa_window(ci, q[h], kvw_ref, vwt_ref, rel_ref, h) for h in heads]

    def causal_edge(s_t):
        return jnp.where(rel_ref[0:tk, :] <= ci * qb - last * tk, s_t, NEG)

    def stream(h):
        def scores(j):
            kv = kvs_ref[0, pl.ds(pl.multiple_of(j * tk, tk), tk), lanes(h)]
            return _dot_nt(jnp.concatenate([kv, et_ref[j]], axis=1), w_nt[h])
        return scores, causal_edge, lambda j: vst_ref[0, h, j], attend_scratch[h * n_scr:(h + 1) * n_scr]

    accs = _attend_tiles(last, [stream(h) for h in heads], unroll=4)

    for h, gate_ref in zip(heads, (gate0_ref, gate1_ref)):
        o_s = accs[h][HEAD_DIM:] * (1.0 / accs[h][0:1])
        gt = jax.nn.sigmoid(gate_ref[0]).T
        outs = []
        for g in range(NSA_GROUP):
            ls = slice(g * qb, (g + 1) * qb)
            outs.append(gt[3 * g:3 * g + 1] * o_c[h][HEAD_DIM:, ls] + gt[3 * g + 1:3 * g + 2] * o_s[:, ls]
                        + gt[3 * g + 2:3 * g + 3] * o_w[h][:, ls])
        outs.append(jnp.zeros((NSA_OUT_LANES - NSA_GROUP * HEAD_DIM, qb), F32))
        o_ref[0, :, h * NSA_OUT_LANES:(h + 1) * NSA_OUT_LANES] = jnp.concatenate(outs, axis=0).T


def _nsa_window(ci, q, kvw_ref, vwt_ref, rel_ref, h):
    qb = NSA_QB
    n_wt = NSA_WIN_SPAN // qb
    first = jnp.maximum(ci - WINDOW // qb, 0)
    kvw = kvw_ref[0, pl.ds(pl.multiple_of(first * qb, qb), NSA_WIN_SPAN), h * LANES:(h + 1) * LANES]
    s_t = _dot_nt(kvw, q)
    dist = (ci - first) * qb - rel_ref[...]
    sm = jnp.where(lax.bitcast_convert_type(dist, jnp.uint32) < WINDOW, s_t, NEG)
    e_w = jnp.exp2(sm - jnp.max(sm, axis=0, keepdims=True))
    vw_t = jnp.concatenate([vwt_ref[0, h, first + t] for t in range(n_wt)], axis=1)
    acc_w = _dot(vw_t, e_w.astype(BF16))
    return acc_w[HEAD_DIM:] * (1.0 / acc_w[0:1])


def _nsa(p, r, kvcmp, kvcmp_t, vs_t, vw_t, cmp_to_slc_t, e3, rel, crel):
    B, S, _ = p.shape
    qb = NSA_QB
    n_top = min(SLC_TOPN, S // SLC_BLOCK)
    nq = NSA_GROUP * qb
    kvw = NSA_KV_HEADS * LANES
    assert G_NSA_Q == 0 and G_SLC * LANES % kvw == 0 and G_WIN * LANES % kvw == 0
    const = lambda a: pl.BlockSpec(a.shape, lambda b, i: (0,) * a.ndim)
    per_b = lambda a: pl.BlockSpec((1,) + a.shape[1:], lambda b, i: (b,) + (0,) * (a.ndim - 1))
    return pl.pallas_call(
        functools.partial(_nsa_kernel, n_pick=max(n_top - 3, 0)),
        grid=(B, S // qb),
        in_specs=[
            pl.BlockSpec((1, qb, NSA_HEADS * LANES), lambda b, i: (b, i, 0)),
            per_b(kvcmp), per_b(kvcmp_t),
            pl.BlockSpec((1, S, kvw), lambda b, i: (b, 0, G_SLC * LANES // kvw)),
            per_b(vs_t),
            pl.BlockSpec((1, S, kvw), lambda b, i: (b, 0, G_WIN * LANES // kvw)),
            per_b(vw_t),
            pl.BlockSpec((1, qb, LANES), lambda b, i: (b, i, R_GATE_BLOCK)),
            pl.BlockSpec((1, qb, LANES), lambda b, i: (b, i, R_GATE_BLOCK + 1)),
            const(cmp_to_slc_t), const(e3), const(rel), const(crel),
        ],
        out_specs=pl.BlockSpec((1, qb, NSA_KV_HEADS * NSA_OUT_LANES), lambda b, i: (b, i, 0)),
        out_shape=jax.ShapeDtypeStruct((B, S, NSA_KV_HEADS * NSA_OUT_LANES), F32),
        scratch_shapes=sum((_attend_scratch(NSA_TK, nq) for _ in range(NSA_KV_HEADS)), []),
        compiler_params=_cparams(("arbitrary", "arbitrary")),
        name="nsa_attention",
    )(p, kvcmp, kvcmp_t, p, vs_t, p, vw_t, r, r, cmp_to_slc_t, e3, rel, crel)


def _nsa_constants(S):
    n_cmp = (S - CMP_BLOCK) // CMP_STRIDE + 1
    nc = S // CMP_STRIDE
    n_slc = S // SLC_BLOCK
    c_start = np.arange(nc) * CMP_STRIDE
    s_start = np.arange(LANES) * SLC_BLOCK
    overlap = (np.minimum(c_start[None, :] + CMP_BLOCK, s_start[:, None] + SLC_BLOCK)
               - np.maximum(c_start[None, :], s_start[:, None]))
    m_t = np.clip(overlap, 0, None).astype(np.float32) / CMP_BLOCK
    m_t = np.where((np.arange(nc)[None, :] < n_cmp) & (np.arange(LANES)[:, None] < n_slc), m_t, 0.0)
    key_blk = (np.arange(S) // SLC_BLOCK).reshape(S // NSA_TK, NSA_TK, 1)
    e3 = (key_blk == np.arange(LANES)[None, None, :]).astype(np.float32)
    lane_q = np.arange(NSA_GROUP * NSA_QB) % NSA_QB
    rel = np.arange(NSA_WIN_SPAN)[:, None] - lane_q[None, :]
    crel = (np.arange(nc) * CMP_STRIDE + CMP_BLOCK - 1)[:, None] - lane_q[None, :]
    return jnp.asarray(m_t, BF16), jnp.asarray(e3, BF16), jnp.asarray(rel, jnp.int32), jnp.asarray(crel, jnp.int32)


def _moba_kernel(q_ref, kv_ref, vt_ref, km_ref, o_ref, sel_scr, *attend_scratch):
    ci = pl.program_id(2)
    qb = MOBA_BLOCK
    n_scr = len(attend_scratch) // MOBA_HEADS_PER_STEP

    def causal_edge(s_t):
        causal = lax.broadcasted_iota(jnp.int32, s_t.shape, 0) <= lax.broadcasted_iota(jnp.int32, s_t.shape, 1)
        return jnp.where(causal, s_t, NEG)

    def head_stream(h):
        lanes = slice(h * LANES, (h + 1) * LANES)
        q = q_ref[0, :, lanes]
        km_hi, km_lo = _split2(km_ref[0, :, lanes])
        gate_t = _dot_nt(km_hi, q) + _dot_nt(km_lo, q)
        blk = lax.broadcasted_iota(jnp.int32, gate_t.shape, 0)
        valid = blk < ci
        sc = jnp.where(valid, gate_t, -jnp.inf)
        picked = jnp.zeros(gate_t.shape, F32)
        for _ in range(MOBA_TOPK):
            best = jnp.max(sc, axis=0, keepdims=True)
            idx = jnp.min(jnp.where(sc == best, blk, gate_t.shape[0]), axis=0, keepdims=True)
            pick = blk == idx
            picked = jnp.where(pick, 1.0, picked)
            sc = jnp.where(pick, -jnp.inf, sc)
        sel = jnp.where(valid, picked, jnp.where(blk == ci, 1.0, 0.0))
        sel_scr[h] = jnp.where(sel > 0.5, 0.0, NEG)

        def scores(j):
            kvj = kv_ref[0, pl.ds(pl.multiple_of(j * qb, qb), qb), lanes]
            return _dot_nt(kvj, q) + sel_scr[h, pl.ds(j, 1), :]

        return scores, causal_edge, lambda j: vt_ref[0, h, j], attend_scratch[h * n_scr:(h + 1) * n_scr]

    accs = _attend_tiles(ci, [head_stream(h) for h in range(MOBA_HEADS_PER_STEP)], unroll=2)
    for h, acc in enumerate(accs):
        o_ref[0, :, h * LANES:(h + 1) * LANES] = (acc * (1.0 / acc[0:1])).T


def _moba(p, mv_t, kmean):
    B, S, _ = p.shape
    qb = MOBA_BLOCK
    hp = MOBA_HEADS_PER_STEP
    w = hp * LANES
    return pl.pallas_call(
        _moba_kernel,
        grid=(B, MOBA_HEADS // hp, S // qb),
        in_specs=[
            pl.BlockSpec((1, qb, w), lambda b, h, i: (b, i, G_MOBA_Q // hp + h)),
            pl.BlockSpec((1, S, w), lambda b, h, i: (b, 0, G_MOBA_KV // hp + h)),
            pl.BlockSpec((1, hp) + mv_t.shape[2:], lambda b, h, i: (b, h, 0, 0, 0)),
            pl.BlockSpec((1, MOBA_NBP, w), lambda b, h, i: (b, 0, h)),
        ],
        out_specs=pl.BlockSpec((1, qb, w), lambda b, h, i: (b, i, h)),
        out_shape=jax.ShapeDtypeStruct((B, S, MOBA_HEADS * LANES), F32),
        scratch_shapes=[pltpu.VMEM((hp, MOBA_NBP, qb), F32)] + sum((_attend_scratch(qb, qb) for _ in range(hp)), []),
        compiler_params=_cparams(("arbitrary", "arbitrary", "arbitrary")),
        name="moba_attention",
    )(p, p, mv_t, kmean)


def _rwkv_kernel(f_ref, mu_ref, wup_ref, aup_ref, gup_ref, vec_ref, bd_ref, tri_ref, o_ref, carry_scr, st_scr):
    i = pl.program_id(1)
    tb = f_ref.shape[1]
    C = RWKV_CHUNK
    W = RWKV_WIDTH

    @pl.when(i == 0)
    def _():
        carry_scr[...] = jnp.zeros(carry_scr.shape, F32)
        st_scr[...] = jnp.zeros(st_scr.shape, F32)

    feat = f_ref[0]
    rowi = lax.broadcasted_iota(jnp.int32, feat.shape, 0)
    prev = jnp.where(rowi == 0, carry_scr[0:1, :], pltpu.roll(feat, 1, 0))
    carry_scr[0:1, :] = feat[tb - 1:tb, :]
    xs = feat + (prev - feat) * mu_ref[...]
    r = xs[:, 0:W]
    k = xs[:, W:2 * W]
    v = xs[:, 2 * W:3 * W]
    wa = xs[:, 3 * W:3 * W + DECAY_LORA + AAA_LORA]
    gd = xs[:, 3 * W + DECAY_LORA + AAA_LORA:]
    w0, a0, k_k, k_a, r_k, ln_w, ln_b = (vec_ref[n:n + 1, :] for n in range(7))
    bd = bd_ref[...]

    def hsum(t):
        hi, lo = _split2(t)
        return _dot(hi, bd) + _dot(lo, bd)

    logw = -DECAY_SCALE * jax.nn.sigmoid(w0 + _mm3(jnp.tanh(wa), wup_ref[...]))
    a = jax.nn.sigmoid(a0 + _mm1(wa, aup_ref[...]))
    gate = _mm1(jax.nn.sigmoid(gd), gup_ref[...])
    kk = k * k_k
    kk = kk / jnp.maximum(jnp.sqrt(hsum(kk * kk)), 1e-12)
    k = k * (1.0 + (a - 1.0) * k_a)
    bonus = hsum(r * k * r_k) * v
    kka = kk * a

    ri = lax.broadcasted_iota(jnp.int32, (tb, tb), 0)
    cj = lax.broadcasted_iota(jnp.int32, (tb, tb), 1)
    same = (ri // C) == (cj // C)
    strict = same & (cj < ri)
    incl = same & (cj <= ri)
    eye = jnp.where(ri == cj, 1.0, 0.0)
    eye_c = eye[0:C, 0:C]
    hi, mid, lo = _split3(logw)
    tri = tri_ref[...]
    cum = _dot(tri, hi) + _dot(tri, mid) + _dot(tri, lo)
    tot = jnp.concatenate([jnp.broadcast_to(cum[c * C + C - 1:c * C + C], (C, W)) for c in range(tb // C)], axis=0)
    e_neg = jnp.exp(-cum)
    e_end = jnp.exp(tot - cum)
    d_end = jnp.exp(tot)
    a_t = -kk * jnp.exp(cum - logw)
    r_t = r * jnp.exp(cum)
    b_t = kka * e_neg
    k_t = k * e_neg
    b_e = kka * e_end
    k_e = k * e_end
    heads = range(RWKV_HEADS)
    hs = [slice(h * HEAD_DIM, (h + 1) * HEAD_DIM) for h in heads]
    bf = lambda t: t.astype(BF16)
    ah = [bf(a_t[:, s]) for s in hs]
    rh = [r_t[:, s] for s in hs]
    vh = [bf(v[:, s]) for s in hs]
    ar = [jnp.concatenate([ah[h], bf(rh[h])], axis=0) for h in heads]
    xb = [_dot_nt(ar[h], bf(b_t[:, hs[h]])) for h in heads]
    xk = [_dot_nt(ar[h], bf(k_t[:, hs[h]])) for h in heads]
    n = [bf(jnp.where(strict, xb[h][0:tb], 0.0)) for h in heads]
    m_ak = [bf(jnp.where(strict, xk[h][0:tb], 0.0)) for h in heads]
    m_rb = [bf(jnp.where(incl, xb[h][tb:], 0.0)) for h in heads]
    m_rk = [bf(jnp.where(incl, xk[h][tb:], 0.0)) for h in heads]
    tinv = [eye + n[h].astype(F32) for h in heads]
    npow = n
    step = 1
    while 2 * step < C:
        npow = [bf(_dot(npow[h], npow[h])) for h in heads]
        tinv = [tinv[h] + _dot(bf(tinv[h]), npow[h]) for h in heads]
        step *= 2
    tinv = [bf(t) for t in tinv]
    g = [_dot(tinv[h], ah[h]) for h in heads]
    u0 = [_dot(tinv[h], bf(_dot(m_ak[h], vh[h]))) for h in heads]
    gb = [bf(t) for t in g]
    ub = [bf(t) for t in u0]
    r_y = [rh[h] + _dot(m_rb[h], gb[h]) for h in heads]
    y0 = [_dot(m_rb[h], ub[h]) + _dot(m_rk[h], vh[h]) for h in heads]
    beh = [bf(b_e[:, s]) for s in hs]
    keh = [bf(k_e[:, s]) for s in hs]
    st = [st_scr[h] for h in heads]
    yh = [[] for _ in heads]
    for c in range(tb // C):
        cs = slice(c * C, (c + 1) * C)
        p_st = [eye_c * d_end[c * C:c * C + 1, hs[h]] + _dot_tn(beh[h][cs], gb[h][cs]) for h in heads]
        q_st = [_dot_tn(beh[h][cs], ub[h][cs]) + _dot_tn(keh[h][cs], vh[h][cs]) for h in heads]
        for h in heads:
            yh[h].append(_mm3(r_y[h][cs], st[h]) + y0[h][cs])
        st = [_mm3(p_st[h], st[h]) + q_st[h] for h in heads]
    ys = []
    for h in heads:
        st_scr[h] = st[h]
        y = jnp.concatenate(yh[h], axis=0)
        mean = jnp.mean(y, axis=-1, keepdims=True)
        yc = y - mean
        var = jnp.mean(yc * yc, axis=-1, keepdims=True)
        ys.append(yc * lax.rsqrt(var + GN_EPS))
    yn = jnp.concatenate(ys, axis=1)
    o_ref[0] = (yn * ln_w + ln_b + bonus) * gate


def _rwkv(r, mu, wup, aup, gup, vecs, bd, tri):
    B, S, _ = r.shape
    tb = RWKV_TB
    full = lambda a: pl.BlockSpec(a.shape, lambda b, i: (0,) * a.ndim)
    return pl.pallas_call(
        _rwkv_kernel,
        grid=(B, S // tb),
        in_specs=[pl.BlockSpec((1, tb, RWKV_COLS), lambda b, i: (b, i, 0)),
                  full(mu), full(wup), full(aup), full(gup), full(vecs), full(bd), full(tri)],
        out_specs=pl.BlockSpec((1, tb, RWKV_WIDTH), lambda b, i: (b, i, 0)),
        out_shape=jax.ShapeDtypeStruct((B, S, RWKV_WIDTH), F32),
        scratch_shapes=[pltpu.VMEM((8, RWKV_COLS), F32), pltpu.VMEM((RWKV_HEADS, HEAD_DIM, HEAD_DIM), F32)],
        compiler_params=_cparams(("arbitrary", "arbitrary")),
        name="rwkv7",
    )(r, mu, wup, aup, gup, vecs, bd, tri)


def _rwkv_constants():
    head = np.arange(RWKV_WIDTH) // HEAD_DIM
    bd = (head[:, None] == head[None, :]).astype(np.float32)
    t = np.arange(RWKV_TB)
    same = (t[None, :] // RWKV_CHUNK) == (t[:, None] // RWKV_CHUNK)
    tri = (same & (t[None, :] <= t[:, None])).astype(np.float32)
    return jnp.asarray(bd, BF16), jnp.asarray(tri, BF16)


def _outproj_kernel(mn_ref, mm_ref, rw_ref, x_ref, g1_ref, gn_ref, gm_ref, wn_ref, wm_ref, wr_ref, o_ref):
    def head_norm(o, gain_ref, width):
        o = jnp.where(gain_ref[1:2, :] > 0.5, o, 0.0)
        ms = jnp.sum(o * o, axis=-1, keepdims=True) * (1.0 / width)
        return (o * lax.rsqrt(ms + NORM_EPS) * gain_ref[0:1, :]).astype(BF16)

    z = _dot(head_norm(mn_ref[0], gn_ref, NSA_WIDTH), wn_ref[...])
    z = z + _dot(head_norm(mm_ref[0], gm_ref, MOBA_WIDTH), wm_ref[...])
    z = z + _dot(rw_ref[0].astype(BF16), wr_ref[...])
    o_ref[0] = x_ref[0] + g1_ref[0] * z


def _outproj(mixn, mixm, orw, x, g1, gn, gm, wn, wm, wr):
    B, S, D = x.shape
    tm = 512
    full = lambda a: pl.BlockSpec(a.shape, lambda b, i: (0,) * a.ndim)
    tok = lambda a: pl.BlockSpec((1, tm, a.shape[2]), lambda b, i: (b, i, 0))
    return pl.pallas_call(
        _outproj_kernel,
        grid=(B, S // tm),
        in_specs=[tok(mixn), tok(mixm), tok(orw), tok(x), pl.BlockSpec((1, 1, D), lambda b, i: (b, 0, 0)),
                  full(gn), full(gm), full(wn), full(wm), full(wr)],
        out_specs=tok(x),
        out_shape=jax.ShapeDtypeStruct((B, S, D), F32),
        compiler_params=_cparams(("arbitrary", "arbitrary")),
        name="outproj",
    )(mixn, mixm, orw, x, g1, gn, gm, wn, wm, wr)


def _pad_moba(t):
    t = t.reshape((MOBA_HEADS, HEAD_DIM) + t.shape[1:])
    return jnp.concatenate([jnp.zeros_like(t), t], axis=1).reshape((MOBA_HEADS * LANES,) + t.shape[2:])


def _pad_nsa(t):
    w = NSA_GROUP * HEAD_DIM
    t = t.reshape((NSA_KV_HEADS, w) + t.shape[1:])
    pad = jnp.zeros((NSA_KV_HEADS, NSA_OUT_LANES - w) + t.shape[2:], t.dtype)
    return jnp.concatenate([t, pad], axis=1).reshape((NSA_KV_HEADS * NSA_OUT_LANES,) + t.shape[2:])


def _gain_and_mask(gain, pad):
    return jnp.stack([pad(gain), pad(jnp.ones_like(gain))])


def _route(logit_t, bias_col):
    aff = jax.nn.sigmoid(logit_t)
    biased = aff + bias_col
    row = lambda t, e: t[e:e + 1, :]
    gp = EXPERTS_PER_GROUP
    scores = []
    for g in range(N_EXPERT_GROUPS):
        a_, b_, c_, d_ = (row(biased, g * gp + j) for j in range(gp))
        hi1, lo1, hi2, lo2 = jnp.maximum(a_, b_), jnp.minimum(a_, b_), jnp.maximum(c_, d_), jnp.minimum(c_, d_)
        scores.append(jnp.maximum(hi1, hi2) + jnp.maximum(jnp.minimum(hi1, hi2), jnp.maximum(lo1, lo2)))
    best = jnp.zeros_like(scores[0], dtype=jnp.int32)
    best_s = scores[0]
    for g in range(1, N_EXPERT_GROUPS):
        better = scores[g] > best_s
        best = jnp.where(better, g, best)
        best_s = jnp.where(better, scores[g], best_s)

    def in_group(t, j):
        out = row(t, j)
        for g in range(1, N_EXPERT_GROUPS):
            out = jnp.where(best == g, row(t, g * gp + j), out)
        return out

    vals = [in_group(biased, j) for j in range(gp)]
    affs = [in_group(aff, j) for j in range(gp)]

    def first_argmax(vs):
        top = functools.reduce(jnp.maximum, vs)
        idx = jnp.full(top.shape, gp, jnp.int32)
        for j in reversed(range(gp)):
            idx = jnp.where(vs[j] == top, j, idx)
        return idx

    i1 = first_argmax(vals)
    i2 = first_argmax([jnp.where(i1 == j, -jnp.inf, vals[j]) for j in range(gp)])
    pick = lambda idx: functools.reduce(lambda acc, j: jnp.where(idx == j, affs[j], acc), range(gp), jnp.zeros_like(affs[0]))
    w1, w2 = pick(i1), pick(i2)
    tot = w1 + w2
    e_iota = lax.broadcasted_iota(jnp.int32, logit_t.shape, 0)
    e1 = best * gp + i1
    e2 = best * gp + i2
    return jnp.where(e_iota == e1, w1 / tot, 0.0) + jnp.where(e_iota == e2, w2 / tot, 0.0)


def _moe_kernel(x_ref, gain_ref, sc_ref, sh_ref, g2_ref, wrt_ref, rb_ref, wg_ref, wu_ref, wd_ref, fin_ref,
                o_ref, hb_scr, cb_scr, acc_scr, *, final):
    e = pl.program_id(1)
    tm = x_ref.shape[1]

    @pl.when(e == 0)
    def _():
        x = x_ref[0]
        ms = jnp.mean(x * x, axis=-1, keepdims=True)
        h = x * lax.rsqrt(ms + NORM_EPS) * gain_ref[...]
        h = h * (1.0 + sc_ref[0]) + sh_ref[0]
        hb_scr[...] = h.astype(BF16)
        comb = _route(_mm3(wrt_ref[...], h, _dot_nt), rb_ref[...])
        cb_scr[...] = jnp.concatenate([comb, jnp.zeros((LANES - N_EXPERTS, tm), F32)], axis=0).T
        acc_scr[...] = jnp.zeros(acc_scr.shape, F32)

    hb = hb_scr[...]
    comb_t = cb_scr[...]
    hes = []
    for j in range(MOE_EXPERTS_PER_STEP):
        hg = _dot(hb, wg_ref[j])
        hu = _dot(hb, wu_ref[j])
        cbe = jnp.broadcast_to(comb_t[:, j:j + 1], (tm, LANES))
        he = hg * jax.nn.sigmoid(hg) * hu * jnp.concatenate([cbe] * (EXPERT_FF // LANES), axis=1)
        hes.append(he.astype(BF16))
    cb_scr[...] = pltpu.roll(comb_t, LANES - MOE_EXPERTS_PER_STEP, 1)
    acc_scr[...] += _dot(jnp.concatenate(hes, axis=1), wd_ref[...])

    @pl.when(e == N_EXPERTS // MOE_EXPERTS_PER_STEP - 1)
    def _():
        out = x_ref[0] + g2_ref[0] * acc_scr[...]
        if final:
            ms = jnp.mean(out * out, axis=-1, keepdims=True)
            out = out * lax.rsqrt(ms + NORM_EPS) * fin_ref[...]
        o_ref[0] = out


def _moe(x, gain, sc, sh, g2, wrt, rb, wg, wu, wd, fin, final):
    B, S, D = x.shape
    tm = 1024
    tpb = S // tm
    eps = MOE_EXPERTS_PER_STEP
    wd = wd.reshape(N_EXPERTS * EXPERT_FF, D)
    tok = pl.BlockSpec((1, tm, D), lambda i, e: (i // tpb, i % tpb, 0))
    per_b = pl.BlockSpec((1, 1, D), lambda i, e: (i // tpb, 0, 0))
    full = lambda a: pl.BlockSpec(a.shape, lambda i, e: (0,) * a.ndim)
    return pl.pallas_call(
        functools.partial(_moe_kernel, final=final),
        grid=(B * tpb, N_EXPERTS // eps),
        in_specs=[tok, full(gain), per_b, per_b, per_b, full(wrt), full(rb),
                  pl.BlockSpec((eps, D, EXPERT_FF), lambda i, e: (e, 0, 0)),
                  pl.BlockSpec((eps, D, EXPERT_FF), lambda i, e: (e, 0, 0)),
                  pl.BlockSpec((eps * EXPERT_FF, D), lambda i, e: (e, 0)),
                  full(fin)],
        out_specs=tok,
        out_shape=jax.ShapeDtypeStruct((B, S, D), F32),
        scratch_shapes=[pltpu.VMEM((tm, D), BF16), pltpu.VMEM((tm, LANES), F32), pltpu.VMEM((tm, D), F32)],
        compiler_params=_cparams(("arbitrary", "arbitrary")),
        name="moe",
    )(x, gain, sc, sh, g2, wrt, rb, wg, wu, wd, fin)


def kernel(x, c, positions, w_mod, b_mod, norm_mix, norm_ffn, w_in, nsa_phi_w1, nsa_phi_w2, nsa_phi_pos, rwkv_mu, rwkv_w_up, rwkv_w0, rwkv_a_up, rwkv_a0, rwkv_g_up, rwkv_k_k, rwkv_k_a, rwkv_r_k, rwkv_ln_w, rwkv_ln_b, norm_nsa_out, norm_moba_out, w_out, w_router, router_bias, moe_w_gate, moe_w_up, moe_w_down, norm_final):
    B, S, D = x.shape
    depth = w_in.shape[0]
    assert S % NSA_TK == 0 and S % MOBA_BLOCK == 0 and S % RWKV_TB == 0 and S >= WINDOW + NSA_QB
    assert S // SLC_BLOCK <= LANES and S // MOBA_BLOCK <= MOBA_NBP

    inv = ROPE_THETA ** (-jnp.arange(0, HEAD_DIM, 2, dtype=F32) / HEAD_DIM)
    ang = positions.astype(F32)[..., None] * inv
    cos, sin = jnp.cos(ang), jnp.sin(ang)
    one, zero = jnp.ones((B, S, HEAD_DIM), F32), jnp.zeros((B, S, HEAD_DIM), F32)
    cos_t = jnp.concatenate([cos, cos, one], axis=-1)
    sin_s = jnp.concatenate([-sin, sin, zero], axis=-1)
    nc = S // CMP_STRIDE
    n_cmp = (S - CMP_BLOCK) // CMP_STRIDE + 1
    pad_c = lambda t, fill: jnp.concatenate([t[:, CMP_BLOCK - 1::CMP_STRIDE][:, :n_cmp],
                                             jnp.full((B, nc - n_cmp, LANES), fill, F32)], axis=1)
    cos_c, sin_c = pad_c(cos_t, 1.0), pad_c(sin_s, 0.0)

    nsa_consts = _nsa_constants(S)
    bd, tri = _rwkv_constants()
    mod = _modulation(c, w_mod, b_mod)
    wrt = w_router.T
    rb = router_bias.reshape(N_EXPERTS, 1)
    fin = norm_final.reshape(1, D)

    for l in range(depth):
        sh1, sc1, g1, sh2, sc2, g2 = (mod[l, :B, i * D:(i + 1) * D].reshape(B, 1, D) for i in range(6))
        p, r, km, vs_t, vw_t, mv_t = _inproj(x, norm_mix[l].reshape(1, D), sc1, sh1, cos_t, sin_s, _prep_w_in(w_in[l]))

        w1, w2, pos4 = _prep_compress(nsa_phi_w1[l], nsa_phi_w2[l], nsa_phi_pos[l])
        chunks = lambda g: p[:, :, g * LANES:(g + 1) * LANES].reshape(B, nc, CMP_STRIDE * LANES)
        kvcmp, kvcmp_t = _compress(chunks(G_KC), chunks(G_VC), pos4, w1, w2, cos_c, sin_c)
        mixn = _nsa(p, r, kvcmp, kvcmp_t, vs_t, vw_t, *nsa_consts)

        kmean = km[:, :, 0, :]
        kmean = jnp.concatenate([kmean, jnp.zeros((B, MOBA_NBP - kmean.shape[1], kmean.shape[2]), F32)], axis=1)
        mixm = _moba(p, mv_t, kmean)

        zl = jnp.zeros((DECAY_LORA, RWKV_WIDTH), F32)
        wup = jnp.concatenate([rwkv_w_up[l], zl], axis=0)
        aup = jnp.concatenate([zl, rwkv_a_up[l]], axis=0)
        vecs = jnp.stack([rwkv_w0[l], rwkv_a0[l], rwkv_k_k[l], rwkv_k_a[l], rwkv_r_k[l].reshape(-1),
                          rwkv_ln_w[l], rwkv_ln_b[l], jnp.zeros((RWKV_WIDTH,), F32)])
        orw = _rwkv(r, rwkv_mu[l].reshape(1, -1), wup, aup, rwkv_g_up[l], vecs, bd, tri)

        wo = w_out[l]
        x = _outproj(mixn, mixm, orw, x, g1,
                     _gain_and_mask(norm_nsa_out[l], _pad_nsa),
                     _gain_and_mask(norm_moba_out[l], _pad_moba),
                     _pad_nsa(wo[:NSA_WIDTH]).astype(BF16),
                     _pad_moba(wo[NSA_WIDTH:NSA_WIDTH + MOBA_WIDTH]).astype(BF16),
                     wo[NSA_WIDTH + MOBA_WIDTH:].astype(BF16))
        x = _moe(x, norm_ffn[l].reshape(1, D), sc2, sh2, g2, wrt, rb,
                 moe_w_gate[l].astype(BF16), moe_w_up[l].astype(BF16), moe_w_down[l].astype(BF16),
                 fin, final=(l == depth - 1))
    return x
```

```python
import functools

import jax
import jax.numpy as jnp
import numpy as np
from jax import lax
from jax.experimental import pallas as pl
from jax.experimental.pallas import tpu as pltpu

F32 = jnp.float32
BF16 = jnp.bfloat16

HEAD_DIM = 64
LANES = 128
ROPE_THETA = 10000.0
NORM_EPS = 1e-6
NEG = -1e30
LOG2_E = 1.4426950408889634

NSA_HEADS = 6
NSA_KV_HEADS = 2
NSA_GROUP = NSA_HEADS // NSA_KV_HEADS
NSA_WIDTH = NSA_HEADS * HEAD_DIM
NSA_KV_WIDTH = NSA_KV_HEADS * HEAD_DIM
CMP_BLOCK = 32
CMP_STRIDE = 16
SLC_BLOCK = 64
SLC_TOPN = 16
WINDOW = 512
NSA_QB = 256
NSA_TK = 256
NSA_OUT_LANES = 256
NSA_WIN_SPAN = WINDOW + NSA_QB
ATTEND_UNROLL = 4

MOBA_HEADS = 4
MOBA_WIDTH = MOBA_HEADS * HEAD_DIM
MOBA_BLOCK = 256
MOBA_TOPK = 3
MOBA_NBP = 32
MOBA_HEADS_PER_STEP = 4

RWKV_HEADS = 6
RWKV_WIDTH = RWKV_HEADS * HEAD_DIM
DECAY_LORA = 64
AAA_LORA = 64
GATE_LORA = 128
RWKV_COLS = 3 * RWKV_WIDTH + DECAY_LORA + AAA_LORA + GATE_LORA
DECAY_SCALE = 0.606531
GN_EPS = 64e-5
RWKV_CHUNK = 64
RWKV_TB = 256

N_EXPERTS = 16
N_EXPERT_GROUPS = 4
EXPERTS_PER_GROUP = N_EXPERTS // N_EXPERT_GROUPS
EXPERT_FF = 256
MOE_EXPERTS_PER_STEP = 8

G_NSA_Q = 0
G_SLC = 6
G_WIN = 8
G_KC = 10
G_VC = 11
G_MOBA_Q = 12
G_MOBA_KV = 16
N_GROUPS = 20
ROPED_GROUPS = tuple(range(0, 10)) + tuple(range(12, 20))
P_COLS = N_GROUPS * LANES
R_GATE_BLOCK = RWKV_COLS // LANES
R_COLS = RWKV_COLS + NSA_KV_HEADS * LANES

VMEM_LIMIT = 56 * 1024 * 1024


def _cparams(sem):
    return pltpu.CompilerParams(dimension_semantics=sem, vmem_limit_bytes=VMEM_LIMIT)


def _dot(a, b):
    return jnp.dot(a, b, preferred_element_type=F32)


def _dot_nt(a, b):
    return lax.dot_general(a, b, (((1,), (1,)), ((), ())), preferred_element_type=F32)


def _dot_tn(a, b):
    return lax.dot_general(a, b, (((0,), (0,)), ((), ())), preferred_element_type=F32)


def _split2(x):
    hi = x.astype(BF16)
    lo = (x - hi.astype(F32)).astype(BF16)
    return hi, lo


def _split3(x):
    hi = x.astype(BF16)
    r1 = x - hi.astype(F32)
    mid = r1.astype(BF16)
    lo = (r1 - mid.astype(F32)).astype(BF16)
    return hi, mid, lo


def _mm3(a, b, dot=_dot):
    ah, al = _split2(a)
    bh, bl = _split2(b)
    return dot(ah, bh) + dot(ah, bl) + dot(al, bh)


def _mm1(a, b, dot=_dot):
    return dot(a.astype(BF16), b.astype(BF16))


def _mm3_exact_rhs(a, b_bf16):
    hi, mid, lo = _split3(a)
    return _dot(hi, b_bf16) + _dot(mid, b_bf16) + _dot(lo, b_bf16)


def _rope(y, cos_t, sin_s):
    lane = lax.broadcasted_iota(jnp.int32, y.shape, 1)
    rot = jnp.where(lane < HEAD_DIM // 2, pltpu.roll(y, LANES - HEAD_DIM // 2, 1), pltpu.roll(y, HEAD_DIM // 2, 1))
    return y * cos_t + rot * sin_s


def _mod_kernel(c_ref, w_ref, b_ref, o_ref):
    c = c_ref[...]
    ca = c * jax.nn.sigmoid(c)
    o_ref[0] = _dot(ca.astype(BF16), w_ref[0].astype(BF16)) + b_ref[0]


def _modulation(c, w_mod, b_mod):
    B, D = c.shape
    L, _, N = w_mod.shape
    tn = 512
    c8 = jnp.zeros((8, D), F32).at[:B].set(c)
    return pl.pallas_call(
        _mod_kernel,
        grid=(L, N // tn),
        in_specs=[
            pl.BlockSpec((8, D), lambda l, j: (0, 0)),
            pl.BlockSpec((1, D, tn), lambda l, j: (l, 0, j)),
            pl.BlockSpec((1, 1, tn), lambda l, j: (l, 0, j)),
        ],
        out_specs=pl.BlockSpec((1, 8, tn), lambda l, j: (l, 0, j)),
        out_shape=jax.ShapeDtypeStruct((L, 8, N), F32),
        compiler_params=_cparams(("arbitrary", "arbitrary")),
        name="modulation",
    )(c8, w_mod, b_mod.reshape(L, 1, N))


def _ones_over_values_t(y):
    yt = y.T
    row = lax.broadcasted_iota(jnp.int32, yt.shape, 0)
    return jnp.where(row < HEAD_DIM, 1.0, yt).astype(BF16)


def _inproj_kernel(x_ref, gain_ref, sc_ref, sh_ref, cos_ref, sin_ref, w_ref, p_ref, r_ref, km_ref,
                   vs_ref, vw_ref, mv_ref):
    x = x_ref[0]
    ms = jnp.mean(x * x, axis=-1, keepdims=True)
    h = x * lax.rsqrt(ms + NORM_EPS) * gain_ref[...]
    h = h * (1.0 + sc_ref[0]) + sh_ref[0]
    hb = h.astype(BF16)
    cos_t = cos_ref[0]
    sin_s = sin_ref[0]
    per = 4
    for g0 in range(0, N_GROUPS, per):
        y4 = _dot(hb, w_ref[:, g0 * LANES:(g0 + per) * LANES])
        for j in range(per):
            g = g0 + j
            y = y4[:, j * LANES:(j + 1) * LANES]
            if g in ROPED_GROUPS:
                y = _rope(y, cos_t, sin_s)
            if g >= G_MOBA_KV:
                km = jnp.mean(y, axis=0, keepdims=True)
                km_ref[0, 0, :, (g - G_MOBA_KV) * LANES:(g - G_MOBA_KV + 1) * LANES] = jnp.broadcast_to(km, (8, LANES))
                mv_ref[0, g - G_MOBA_KV, 0] = _ones_over_values_t(y)
            elif G_SLC <= g < G_WIN:
                vs_ref[0, g - G_SLC, 0] = _ones_over_values_t(y)
            elif G_WIN <= g < G_KC:
                yt = _ones_over_values_t(y)
                for t in range(y.shape[0] // NSA_QB):
                    vw_ref[0, g - G_WIN, t] = yt[:, t * NSA_QB:(t + 1) * NSA_QB]
            p_ref[0, :, g * LANES:(g + 1) * LANES] = y.astype(BF16)
    r_ref[0] = _dot(hb, w_ref[:, P_COLS:])


def _inproj(x, gain, sc, sh, cos_t, sin_s, w):
    B, S, D = x.shape
    tm = MOBA_BLOCK
    return pl.pallas_call(
        _inproj_kernel,
        grid=(B, S // tm),
        in_specs=[
            pl.BlockSpec((1, tm, D), lambda b, i: (b, i, 0)),
            pl.BlockSpec((1, D), lambda b, i: (0, 0)),
            pl.BlockSpec((1, 1, D), lambda b, i: (b, 0, 0)),
            pl.BlockSpec((1, 1, D), lambda b, i: (b, 0, 0)),
            pl.BlockSpec((1, tm, LANES), lambda b, i: (b, i, 0)),
            pl.BlockSpec((1, tm, LANES), lambda b, i: (b, i, 0)),
            pl.BlockSpec((D, P_COLS + R_COLS), lambda b, i: (0, 0)),
        ],
        out_specs=[
            pl.BlockSpec((1, tm, P_COLS), lambda b, i: (b, i, 0)),
            pl.BlockSpec((1, tm, R_COLS), lambda b, i: (b, i, 0)),
            pl.BlockSpec((1, 1, 8, MOBA_HEADS * LANES), lambda b, i: (b, i, 0, 0)),
            pl.BlockSpec((1, NSA_KV_HEADS, 1, LANES, tm), lambda b, i: (b, 0, i, 0, 0)),
            pl.BlockSpec((1, NSA_KV_HEADS, tm // NSA_QB, LANES, NSA_QB), lambda b, i: (b, 0, i, 0, 0)),
            pl.BlockSpec((1, MOBA_HEADS, 1, LANES, tm), lambda b, i: (b, 0, i, 0, 0)),
        ],
        out_shape=[
            jax.ShapeDtypeStruct((B, S, P_COLS), BF16),
            jax.ShapeDtypeStruct((B, S, R_COLS), F32),
            jax.ShapeDtypeStruct((B, S // tm, 8, MOBA_HEADS * LANES), F32),
            jax.ShapeDtypeStruct((B, NSA_KV_HEADS, S // tm, LANES, tm), BF16),
            jax.ShapeDtypeStruct((B, NSA_KV_HEADS, S // NSA_QB, LANES, NSA_QB), BF16),
            jax.ShapeDtypeStruct((B, MOBA_HEADS, S // tm, LANES, tm), BF16),
        ],
        compiler_params=_cparams(("arbitrary", "arbitrary")),
        name="inproj",
    )(x, gain, sc, sh, cos_t, sin_s, w)


def _prep_w_in(w):
    D = w.shape[0]
    o = 0
    parts = {}
    for name, width in (("nq", NSA_WIDTH), ("nkc", NSA_KV_WIDTH), ("nvc", NSA_KV_WIDTH), ("nks", NSA_KV_WIDTH),
                        ("nvs", NSA_KV_WIDTH), ("nkw", NSA_KV_WIDTH), ("nvw", NSA_KV_WIDTH), ("ngate", NSA_HEADS * 3),
                        ("mq", MOBA_WIDTH), ("mk", MOBA_WIDTH), ("mv", MOBA_WIDTH), ("rf", RWKV_COLS)):
        parts[name] = w[:, o:o + width]
        o += width
    hd = lambda t, h: t[:, h * HEAD_DIM:(h + 1) * HEAD_DIM]
    z = jnp.zeros((D, HEAD_DIM), F32)
    scale = HEAD_DIM ** -0.5 * LOG2_E
    cols = []
    for h in range(NSA_HEADS):
        cols += [hd(parts["nq"], h) * scale, z]
    for h in range(NSA_KV_HEADS):
        cols += [hd(parts["nks"], h), hd(parts["nvs"], h)]
    for h in range(NSA_KV_HEADS):
        cols += [hd(parts["nkw"], h), hd(parts["nvw"], h)]
    cols += [parts["nkc"], parts["nvc"]]
    for h in range(MOBA_HEADS):
        cols += [hd(parts["mq"], h) * scale, z]
    for h in range(MOBA_HEADS):
        cols += [hd(parts["mk"], h), hd(parts["mv"], h)]
    cols += [parts["rf"]]
    per = NSA_GROUP * 3
    for h in range(NSA_KV_HEADS):
        cols += [parts["ngate"][:, h * per:(h + 1) * per], jnp.zeros((D, LANES - per), F32)]
    return jnp.concatenate(cols, axis=1).astype(BF16)


def _cmp_kernel(kc_ref, vc_ref, pos_ref, w1_ref, w2_ref, cos_ref, sin_ref, o_ref, ot_ref):
    nc = kc_ref.shape[1]

    def hidden(t_ref, ia, ib):
        t = t_ref[0].astype(F32)
        a = _dot((t + pos_ref[ia:ia + 1, :]).astype(BF16), w1_ref[ia])
        b = _dot((t + pos_ref[ib:ib + 1, :]).astype(BF16), w1_ref[ib])
        return jax.nn.gelu(a + pltpu.roll(b, nc - 1, 0)).astype(BF16)

    gk = hidden(kc_ref, 0, 1)
    gv = hidden(vc_ref, 2, 3)
    for h in range(NSA_KV_HEADS):
        y = _dot(gk, w2_ref[2 * h]) + _dot(gv, w2_ref[2 * h + 1])
        y = _rope(y, cos_ref[0], sin_ref[0])
        o_ref[0, h] = y.astype(BF16)
        ot_ref[0, h] = _ones_over_values_t(y)


def _compress(kc16, vc16, pos4, w1, w2, cos_c, sin_c):
    B, nc, K = kc16.shape
    return pl.pallas_call(
        _cmp_kernel,
        grid=(B,),
        in_specs=[
            pl.BlockSpec((1, nc, K), lambda b: (b, 0, 0)),
            pl.BlockSpec((1, nc, K), lambda b: (b, 0, 0)),
            pl.BlockSpec((4, K), lambda b: (0, 0)),
            pl.BlockSpec((4, K, LANES), lambda b: (0, 0, 0)),
            pl.BlockSpec((4, LANES, LANES), lambda b: (0, 0, 0)),
            pl.BlockSpec((1, nc, LANES), lambda b: (b, 0, 0)),
            pl.BlockSpec((1, nc, LANES), lambda b: (b, 0, 0)),
        ],
        out_specs=[pl.BlockSpec((1, NSA_KV_HEADS, nc, LANES), lambda b: (b, 0, 0, 0)),
                   pl.BlockSpec((1, NSA_KV_HEADS, LANES, nc), lambda b: (b, 0, 0, 0))],
        out_shape=[jax.ShapeDtypeStruct((B, NSA_KV_HEADS, nc, LANES), BF16),
                   jax.ShapeDtypeStruct((B, NSA_KV_HEADS, LANES, nc), BF16)],
        compiler_params=_cparams(("arbitrary",)),
        name="nsa_compress",
    )(kc16, vc16, pos4, w1, w2, cos_c, sin_c)


def _prep_compress(phi_w1, phi_w2, phi_pos):
    half = CMP_BLOCK // 2
    eye = jnp.eye(NSA_KV_HEADS, dtype=F32)
    w1, pos = [], []
    for t in range(2):
        for part in range(2):
            w = phi_w1[t, part * half:(part + 1) * half]
            w1.append(jnp.einsum("lde,kK->lkdKe", w, eye).reshape(half * NSA_KV_WIDTH, NSA_KV_WIDTH))
            p = phi_pos[t, part * half:(part + 1) * half]
            pos.append(jnp.broadcast_to(p[:, None, :], (half, NSA_KV_HEADS, HEAD_DIM)).reshape(-1))
    w2 = []
    for h in range(NSA_KV_HEADS):
        for t in range(2):
            m = jnp.zeros((LANES, LANES), F32)
            m = m.at[h * HEAD_DIM:(h + 1) * HEAD_DIM, t * HEAD_DIM:(t + 1) * HEAD_DIM].set(phi_w2[t])
            w2.append(m)
    return jnp.stack(w1).astype(BF16), jnp.stack(w2).astype(BF16), jnp.stack(pos)


def _attend_tiles(n_loop, streams):
    for _, _, _, (_, _, p_scr, alpha_scr, m_scr, acc_scr) in streams:
        m_scr[...] = jnp.full(m_scr.shape, -jnp.inf, F32)
        alpha_scr[...] = jnp.ones(alpha_scr.shape, F32)
        acc_scr[...] = jnp.zeros(acc_scr.shape, F32)
        p_scr[...] = jnp.zeros(p_scr.shape, BF16)

    def fetch(j, buf):
        for scores, _, _, scratch in streams:
            scratch[buf][...] = scores(j)

    def apply_weights(j):
        for _, _, values_t, (_, _, p_scr, alpha_scr, _, acc_scr) in streams:
            acc_scr[...] = alpha_scr[0:1, :] * acc_scr[...] + _dot(values_t(j), p_scr[...])

    def softmax_tiles(buf, last=False):
        for _, mask_last, _, scratch in streams:
            p_scr, alpha_scr, m_scr = scratch[2], scratch[3], scratch[4]
            s_t = scratch[buf][...]
            if last:
                s_t = mask_last(s_t)
            m_prev = m_scr[0:1, :]
            m_new = jnp.maximum(m_prev, jnp.max(s_t, axis=0, keepdims=True))
            alpha_scr[0:1, :] = jnp.exp2(m_prev - m_new)
            p_scr[...] = jnp.exp2(s_t - m_new).astype(BF16)
            m_scr[0:1, :] = m_new

    def step(j, cur, nxt):
        apply_weights(jnp.maximum(j - 1, 0))
        softmax_tiles(cur)
        fetch(j + 1, nxt)

    fetch(0, 0)

    def quad(i, carry):
        for u in range(ATTEND_UNROLL):
            step(ATTEND_UNROLL * i + u, u % 2, (u + 1) % 2)
        return carry

    lax.fori_loop(0, n_loop // ATTEND_UNROLL, quad, 0)
    done = n_loop - n_loop % ATTEND_UNROLL

    @pl.when(n_loop % ATTEND_UNROLL >= 2)
    def _():
        step(done, 0, 1)
        step(done + 1, 1, 0)

    @pl.when(n_loop % 2 == 1)
    def _():
        step(n_loop - 1, 0, 0)

    apply_weights(jnp.maximum(n_loop - 1, 0))
    softmax_tiles(0, last=True)
    apply_weights(n_loop)
    return [scratch[5][...] for _, _, _, scratch in streams]


def _attend_scratch(tk, nq):
    return [pltpu.VMEM((tk, nq), F32), pltpu.VMEM((tk, nq), F32), pltpu.VMEM((tk, nq), BF16),
            pltpu.VMEM((8, nq), F32), pltpu.VMEM((8, nq), F32), pltpu.VMEM((LANES, nq), F32)]


def _nsa_kernel(q_ref, kvc_ref, kvct_ref, kvs_ref, vst_ref, kvw_ref, vwt_ref, gate0_ref, gate1_ref, mt_ref,
                et_ref, rel_ref, crel_ref, o_ref, *attend_scratch, n_pick):
    ci = pl.program_id(1)
    qb = NSA_QB
    tk = NSA_TK
    last = (ci * qb) // tk
    heads = range(NSA_KV_HEADS)
    n_scr = len(attend_scratch) // NSA_KV_HEADS
    lanes = lambda g: slice(g * LANES, (g + 1) * LANES)
    q = [jnp.concatenate([q_ref[0, :, lanes(NSA_GROUP * h + g)] for g in range(NSA_GROUP)], axis=0)
         for h in heads]

    o_c, imp_t = [], []
    mt = mt_ref[...]
    for h in heads:
        sm = jnp.where(crel_ref[...] <= ci * qb, _dot_nt(kvc_ref[0, h], q[h]), NEG)
        mx = jnp.max(sm, axis=0, keepdims=True)
        e = jnp.exp2(sm - mx)
        den = jnp.maximum(jnp.sum(e, axis=0, keepdims=True), 1e-30)
        p_c = e * jnp.where(mx > 0.5 * NEG, 1.0 / den, 0.0)
        o_c.append(_dot(kvct_ref[0, h], p_c.astype(BF16)))
        hi, mid, lo = _split3(p_c[:, 0:qb] + p_c[:, qb:2 * qb] + p_c[:, 2 * qb:3 * qb])
        imp_t.append(_dot(mt, hi) + _dot(mt, mid) + _dot(mt, lo))

    blk = lax.broadcasted_iota(jnp.int32, imp_t[0].shape, 0)
    cur = (ci * qb + lax.broadcasted_iota(jnp.int32, imp_t[0].shape, 1)) // SLC_BLOCK
    forced = (blk == 0) | (blk == cur) | (blk == cur - 1)
    free = (blk <= cur) & jnp.logical_not(forced)
    sc = [jnp.where(free, imp_t[h], -1.0) for h in heads]
    for _ in range(n_pick):
        best = [jnp.max(sc[h], axis=0, keepdims=True) for h in heads]
        idx = [jnp.min(jnp.where(sc[h] == best[h], blk, blk.shape[0]), axis=0, keepdims=True) for h in heads]
        sc = [jnp.where(blk == idx[h], -2.0, sc[h]) for h in heads]
    w_nt = []
    for h in heads:
        bias = jnp.where(free, jnp.where(sc[h] < -1.5, 0.0, NEG), jnp.where(forced, 0.0, NEG)).T.astype(BF16)
        w_nt.append(jnp.concatenate([q[h], jnp.concatenate([bias] * NSA_GROUP, axis=0)], axis=1))
    o_w = [_nsa_window(ci, q[h], kvw_ref, vwt_ref, rel_ref, h) for h in heads]

    def causal_edge(s_t):
        return jnp.where(rel_ref[0:tk, :] <= ci * qb - last * tk, s_t, NEG)

    def stream(h):
        def scores(j):
            kv = kvs_ref[0, pl.ds(pl.multiple_of(j * tk, tk), tk), lanes(h)]
            return _dot_nt(jnp.concatenate([kv, et_ref[j]], axis=1), w_nt[h])
        return scores, causal_edge, lambda j: vst_ref[0, h, j], attend_scratch[h * n_scr:(h + 1) * n_scr]

    accs = _attend_tiles(last, [stream(h) for h in heads])

    for h, gate_ref in zip(heads, (gate0_ref, gate1_ref)):
        o_s = accs[h][HEAD_DIM:] * (1.0 / accs[h][0:1])
        gt = jax.nn.sigmoid(gate_ref[0]).T
        outs = []
        for g in range(NSA_GROUP):
            ls = slice(g * qb, (g + 1) * qb)
            outs.append(gt[3 * g:3 * g + 1] * o_c[h][HEAD_DIM:, ls] + gt[3 * g + 1:3 * g + 2] * o_s[:, ls]
                        + gt[3 * g + 2:3 * g + 3] * o_w[h][:, ls])
        outs.append(jnp.zeros((NSA_OUT_LANES - NSA_GROUP * HEAD_DIM, qb), F32))
        o_ref[0, :, h * NSA_OUT_LANES:(h + 1) * NSA_OUT_LANES] = jnp.concatenate(outs, axis=0).T


def _nsa_window(ci, q, kvw_ref, vwt_ref, rel_ref, h):
    qb = NSA_QB
    n_wt = NSA_WIN_SPAN // qb
    first = jnp.maximum(ci - WINDOW // qb, 0)
    kvw = kvw_ref[0, pl.ds(pl.multiple_of(first * qb, qb), NSA_WIN_SPAN), h * LANES:(h + 1) * LANES]
    s_t = _dot_nt(kvw, q)
    dist = (ci - first) * qb - rel_ref[...]
    sm = jnp.where(lax.bitcast_convert_type(dist, jnp.uint32) < WINDOW, s_t, NEG)
    e_w = jnp.exp2(sm - jnp.max(sm, axis=0, keepdims=True))
    vw_t = jnp.concatenate([vwt_ref[0, h, first + t] for t in range(n_wt)], axis=1)
    acc_w = _dot(vw_t, e_w.astype(BF16))
    return acc_w[HEAD_DIM:] * (1.0 / acc_w[0:1])


def _nsa(p, r, kvcmp, kvcmp_t, vs_t, vw_t, cmp_to_slc_t, e3, rel, crel):
    B, S, _ = p.shape
    qb = NSA_QB
    n_top = min(SLC_TOPN, S // SLC_BLOCK)
    nq = NSA_GROUP * qb
    kvw = NSA_KV_HEADS * LANES
    assert G_NSA_Q == 0 and G_SLC * LANES % kvw == 0 and G_WIN * LANES % kvw == 0
    const = lambda a: pl.BlockSpec(a.shape, lambda b, i: (0,) * a.ndim)
    per_b = lambda a: pl.BlockSpec((1,) + a.shape[1:], lambda b, i: (b,) + (0,) * (a.ndim - 1))
    return pl.pallas_call(
        functools.partial(_nsa_kernel, n_pick=max(n_top - 3, 0)),
        grid=(B, S // qb),
        in_specs=[
            pl.BlockSpec((1, qb, NSA_HEADS * LANES), lambda b, i: (b, i, 0)),
            per_b(kvcmp), per_b(kvcmp_t),
            pl.BlockSpec((1, S, kvw), lambda b, i: (b, 0, G_SLC * LANES // kvw)),
            per_b(vs_t),
            pl.BlockSpec((1, S, kvw), lambda b, i: (b, 0, G_WIN * LANES // kvw)),
            per_b(vw_t),
            pl.BlockSpec((1, qb, LANES), lambda b, i: (b, i, R_GATE_BLOCK)),
            pl.BlockSpec((1, qb, LANES), lambda b, i: (b, i, R_GATE_BLOCK + 1)),
            const(cmp_to_slc_t), const(e3), const(rel), const(crel),
        ],
        out_specs=pl.BlockSpec((1, qb, NSA_KV_HEADS * NSA_OUT_LANES), lambda b, i: (b, i, 0)),
        out_shape=jax.ShapeDtypeStruct((B, S, NSA_KV_HEADS * NSA_OUT_LANES), F32),
        scratch_shapes=sum((_attend_scratch(NSA_TK, nq) for _ in range(NSA_KV_HEADS)), []),
        compiler_params=_cparams(("arbitrary", "arbitrary")),
        name="nsa_attention",
    )(p, kvcmp, kvcmp_t, p, vs_t, p, vw_t, r, r, cmp_to_slc_t, e3, rel, crel)


def _nsa_constants(S):
    n_cmp = (S - CMP_BLOCK) // CMP_STRIDE + 1
    nc = S // CMP_STRIDE
    n_slc = S // SLC_BLOCK
    c_start = np.arange(nc) * CMP_STRIDE
    s_start = np.arange(LANES) * SLC_BLOCK
    overlap = (np.minimum(c_start[None, :] + CMP_BLOCK, s_start[:, None] + SLC_BLOCK)
               - np.maximum(c_start[None, :], s_start[:, None]))
    m_t = np.clip(overlap, 0, None).astype(np.float32) / CMP_BLOCK
    m_t = np.where((np.arange(nc)[None, :] < n_cmp) & (np.arange(LANES)[:, None] < n_slc), m_t, 0.0)
    key_blk = (np.arange(S) // SLC_BLOCK).reshape(S // NSA_TK, NSA_TK, 1)
    e3 = (key_blk == np.arange(LANES)[None, None, :]).astype(np.float32)
    lane_q = np.arange(NSA_GROUP * NSA_QB) % NSA_QB
    rel = np.arange(NSA_WIN_SPAN)[:, None] - lane_q[None, :]
    crel = (np.arange(nc) * CMP_STRIDE + CMP_BLOCK - 1)[:, None] - lane_q[None, :]
    return jnp.asarray(m_t, BF16), jnp.asarray(e3, BF16), jnp.asarray(rel, jnp.int32), jnp.asarray(crel, jnp.int32)


def _moba_kernel(q_ref, kv_ref, vt_ref, km_ref, o_ref, sel_scr, *attend_scratch):
    ci = pl.program_id(2)
    qb = MOBA_BLOCK
    n_scr = len(attend_scratch) // MOBA_HEADS_PER_STEP

    def causal_edge(s_t):
        causal = lax.broadcasted_iota(jnp.int32, s_t.shape, 0) <= lax.broadcasted_iota(jnp.int32, s_t.shape, 1)
        return jnp.where(causal, s_t, NEG)

    def head_stream(h):
        lanes = slice(h * LANES, (h + 1) * LANES)
        q = q_ref[0, :, lanes]
        km_hi, km_lo = _split2(km_ref[0, :, lanes])
        gate_t = _dot_nt(km_hi, q) + _dot_nt(km_lo, q)
        blk = lax.broadcasted_iota(jnp.int32, gate_t.shape, 0)
        valid = blk < ci
        sc = jnp.where(valid, gate_t, -jnp.inf)
        picked = jnp.zeros(gate_t.shape, F32)
        for _ in range(MOBA_TOPK):
            best = jnp.max(sc, axis=0, keepdims=True)
            idx = jnp.min(jnp.where(sc == best, blk, gate_t.shape[0]), axis=0, keepdims=True)
            pick = blk == idx
            picked = jnp.where(pick, 1.0, picked)
            sc = jnp.where(pick, -jnp.inf, sc)
        sel = jnp.where(valid, picked, jnp.where(blk == ci, 1.0, 0.0))
        sel_scr[h] = jnp.where(sel > 0.5, 0.0, NEG)

        def scores(j):
            kvj = kv_ref[0, pl.ds(pl.multiple_of(j * qb, qb), qb), lanes]
            return _dot_nt(kvj, q) + sel_scr[h, pl.ds(j, 1), :]

        return scores, causal_edge, lambda j: vt_ref[0, h, j], attend_scratch[h * n_scr:(h + 1) * n_scr]

    accs = _attend_tiles(ci, [head_stream(h) for h in range(MOBA_HEADS_PER_STEP)])
    for h, acc in enumerate(accs):
        o_ref[0, :, h * LANES:(h + 1) * LANES] = (acc * (1.0 / acc[0:1])).T


def _moba(p, mv_t, kmean):
    B, S, _ = p.shape
    qb = MOBA_BLOCK
    hp = MOBA_HEADS_PER_STEP
    w = hp * LANES
    return pl.pallas_call(
        _moba_kernel,
        grid=(B, MOBA_HEADS // hp, S // qb),
        in_specs=[
            pl.BlockSpec((1, qb, w), lambda b, h, i: (b, i, G_MOBA_Q // hp + h)),
            pl.BlockSpec((1, S, w), lambda b, h, i: (b, 0, G_MOBA_KV // hp + h)),
            pl.BlockSpec((1, hp) + mv_t.shape[2:], lambda b, h, i: (b, h, 0, 0, 0)),
            pl.BlockSpec((1, MOBA_NBP, w), lambda b, h, i: (b, 0, h)),
        ],
        out_specs=pl.BlockSpec((1, qb, w), lambda b, h, i: (b, i, h)),
        out_shape=jax.ShapeDtypeStruct((B, S, MOBA_HEADS * LANES), F32),
        scratch_shapes=[pltpu.VMEM((hp, MOBA_NBP, qb), F32)] + sum((_attend_scratch(qb, qb) for _ in range(hp)), []),
        compiler_params=_cparams(("arbitrary", "arbitrary", "arbitrary")),
        name="moba_attention",
    )(p, p, mv_t, kmean)


def _rwkv_kernel(f_ref, mu_ref, wup_ref, aup_ref, gup_ref, vec_ref, bd_ref, tri_ref, o_ref, carry_scr, st_scr):
    i = pl.program_id(1)
    tb = f_ref.shape[1]
    C = RWKV_CHUNK
    W = RWKV_WIDTH

    @pl.when(i == 0)
    def _():
        carry_scr[...] = jnp.zeros(carry_scr.shape, F32)
        st_scr[...] = jnp.zeros(st_scr.shape, F32)

    feat = f_ref[0]
    rowi = lax.broadcasted_iota(jnp.int32, feat.shape, 0)
    prev = jnp.where(rowi == 0, carry_scr[0:1, :], pltpu.roll(feat, 1, 0))
    carry_scr[0:1, :] = feat[tb - 1:tb, :]
    xs = feat + (prev - feat) * mu_ref[...]
    r = xs[:, 0:W]
    k = xs[:, W:2 * W]
    v = xs[:, 2 * W:3 * W]
    wa = xs[:, 3 * W:3 * W + DECAY_LORA + AAA_LORA]
    gd = xs[:, 3 * W + DECAY_LORA + AAA_LORA:]
    w0, a0, k_k, k_a, r_k, ln_w, ln_b = (vec_ref[n:n + 1, :] for n in range(7))
    bd = bd_ref[...]

    def hsum(t):
        hi, lo = _split2(t)
        return _dot(hi, bd) + _dot(lo, bd)

    logw = -DECAY_SCALE * jax.nn.sigmoid(w0 + _mm3(jnp.tanh(wa), wup_ref[...]))
    a = jax.nn.sigmoid(a0 + _mm1(wa, aup_ref[...]))
    gate = _mm1(jax.nn.sigmoid(gd), gup_ref[...])
    kk = k * k_k
    kk = kk / jnp.maximum(jnp.sqrt(hsum(kk * kk)), 1e-12)
    k = k * (1.0 + (a - 1.0) * k_a)
    bonus = hsum(r * k * r_k) * v
    kka = kk * a

    ri = lax.broadcasted_iota(jnp.int32, (tb, tb), 0)
    cj = lax.broadcasted_iota(jnp.int32, (tb, tb), 1)
    same = (ri // C) == (cj // C)
    strict = same & (cj < ri)
    incl = same & (cj <= ri)
    eye = jnp.where(ri == cj, 1.0, 0.0)
    eye_c = eye[0:C, 0:C]
    hi, mid, lo = _split3(logw)
    tri = tri_ref[0]
    blk = tri_ref[1]
    cum = _dot(tri, hi) + _dot(tri, mid) + _dot(tri, lo)
    tot = _dot(blk, hi) + _dot(blk, mid) + _dot(blk, lo)
    e_neg = jnp.exp(-cum)
    e_end = jnp.exp(tot - cum)
    d_end = jnp.exp(tot)
    a_t = -kk * jnp.exp(cum - logw)
    r_t = r * jnp.exp(cum)
    b_t = kka * e_neg
    k_t = k * e_neg
    b_e = kka * e_end
    k_e = k * e_end
    heads = range(RWKV_HEADS)
    hs = [slice(h * HEAD_DIM, (h + 1) * HEAD_DIM) for h in heads]
    bf = lambda t: t.astype(BF16)
    ah = [bf(a_t[:, s]) for s in hs]
    rh = [r_t[:, s] for s in hs]
    vh = [bf(v[:, s]) for s in hs]
    ar = [jnp.concatenate([ah[h], bf(rh[h])], axis=0) for h in heads]
    xb = [_dot_nt(ar[h], bf(b_t[:, hs[h]])) for h in heads]
    xk = [_dot_nt(ar[h], bf(k_t[:, hs[h]])) for h in heads]
    n = [bf(jnp.where(strict, xb[h][0:tb], 0.0)) for h in heads]
    m_ak = [bf(jnp.where(strict, xk[h][0:tb], 0.0)) for h in heads]
    m_rb = [bf(jnp.where(incl, xb[h][tb:], 0.0)) for h in heads]
    m_rk = [bf(jnp.where(incl, xk[h][tb:], 0.0)) for h in heads]
    tinv = [eye + n[h].astype(F32) for h in heads]
    npow = n
    step = 1
    while 2 * step < C:
        npow = [bf(_dot(npow[h], npow[h])) for h in heads]
        tinv = [tinv[h] + _dot(bf(tinv[h]), npow[h]) for h in heads]
        step *= 2
    tinv = [bf(t) for t in tinv]
    g = [_dot(tinv[h], ah[h]) for h in heads]
    u0 = [_dot(tinv[h], bf(_dot(m_ak[h], vh[h]))) for h in heads]
    gb = [bf(t) for t in g]
    ub = [bf(t) for t in u0]
    r_y = [rh[h] + _dot(m_rb[h], gb[h]) for h in heads]
    y0 = [_dot(m_rb[h], ub[h]) + _dot(m_rk[h], vh[h]) for h in heads]
    beh = [bf(b_e[:, s]) for s in hs]
    keh = [bf(k_e[:, s]) for s in hs]
    st = [st_scr[h] for h in heads]
    yh = [[] for _ in heads]
    for c in range(tb // C):
        cs = slice(c * C, (c + 1) * C)
        p_st = [eye_c * d_end[c * C:c * C + 1, hs[h]] + _dot_tn(beh[h][cs], gb[h][cs]) for h in heads]
        q_st = [_dot_tn(beh[h][cs], ub[h][cs]) + _dot_tn(keh[h][cs], vh[h][cs]) for h in heads]
        for h in heads:
            yh[h].append(_mm3(r_y[h][cs], st[h]) + y0[h][cs])
        st = [_mm3(p_st[h], st[h]) + q_st[h] for h in heads]
    ys = []
    for h in heads:
        st_scr[h] = st[h]
        y = jnp.concatenate(yh[h], axis=0)
        mean = jnp.mean(y, axis=-1, keepdims=True)
        yc = y - mean
        var = jnp.mean(yc * yc, axis=-1, keepdims=True)
        ys.append(yc * lax.rsqrt(var + GN_EPS))
    yn = jnp.concatenate(ys, axis=1)
    o_ref[0] = (yn * ln_w + ln_b + bonus) * gate


def _rwkv(r, mu, wup, aup, gup, vecs, bd, tri):
    B, S, _ = r.shape
    tb = RWKV_TB
    full = lambda a: pl.BlockSpec(a.shape, lambda b, i: (0,) * a.ndim)
    return pl.pallas_call(
        _rwkv_kernel,
        grid=(B, S // tb),
        in_specs=[pl.BlockSpec((1, tb, RWKV_COLS), lambda b, i: (b, i, 0)),
                  full(mu), full(wup), full(aup), full(gup), full(vecs), full(bd), full(tri)],
        out_specs=pl.BlockSpec((1, tb, RWKV_WIDTH), lambda b, i: (b, i, 0)),
        out_shape=jax.ShapeDtypeStruct((B, S, RWKV_WIDTH), F32),
        scratch_shapes=[pltpu.VMEM((8, RWKV_COLS), F32), pltpu.VMEM((RWKV_HEADS, HEAD_DIM, HEAD_DIM), F32)],
        compiler_params=_cparams(("arbitrary", "arbitrary")),
        name="rwkv7",
    )(r, mu, wup, aup, gup, vecs, bd, tri)


def _rwkv_constants():
    head = np.arange(RWKV_WIDTH) // HEAD_DIM
    bd = (head[:, None] == head[None, :]).astype(np.float32)
    t = np.arange(RWKV_TB)
    same = (t[None, :] // RWKV_CHUNK) == (t[:, None] // RWKV_CHUNK)
    tri = np.stack([same & (t[None, :] <= t[:, None]), same]).astype(np.float32)
    return jnp.asarray(bd, BF16), jnp.asarray(tri, BF16)


def _outproj_kernel(mn_ref, mm_ref, rw_ref, x_ref, g1_ref, gn_ref, gm_ref, wn_ref, wm_ref, wr_ref, o_ref):
    def head_norm(o, gain_ref, width):
        o = jnp.where(gain_ref[1:2, :] > 0.5, o, 0.0)
        ms = jnp.sum(o * o, axis=-1, keepdims=True) * (1.0 / width)
        return (o * lax.rsqrt(ms + NORM_EPS) * gain_ref[0:1, :]).astype(BF16)

    z = _dot(head_norm(mn_ref[0], gn_ref, NSA_WIDTH), wn_ref[...])
    z = z + _dot(head_norm(mm_ref[0], gm_ref, MOBA_WIDTH), wm_ref[...])
    z = z + _dot(rw_ref[0].astype(BF16), wr_ref[...])
    o_ref[0] = x_ref[0] + g1_ref[0] * z


def _outproj(mixn, mixm, orw, x, g1, gn, gm, wn, wm, wr):
    B, S, D = x.shape
    tm = 512
    full = lambda a: pl.BlockSpec(a.shape, lambda b, i: (0,) * a.ndim)
    tok = lambda a: pl.BlockSpec((1, tm, a.shape[2]), lambda b, i: (b, i, 0))
    return pl.pallas_call(
        _outproj_kernel,
        grid=(B, S // tm),
        in_specs=[tok(mixn), tok(mixm), tok(orw), tok(x), pl.BlockSpec((1, 1, D), lambda b, i: (b, 0, 0)),
                  full(gn), full(gm), full(wn), full(wm), full(wr)],
        out_specs=tok(x),
        out_shape=jax.ShapeDtypeStruct((B, S, D), F32),
        compiler_params=_cparams(("arbitrary", "arbitrary")),
        name="outproj",
    )(mixn, mixm, orw, x, g1, gn, gm, wn, wm, wr)


def _pad_moba(t):
    t = t.reshape((MOBA_HEADS, HEAD_DIM) + t.shape[1:])
    return jnp.concatenate([jnp.zeros_like(t), t], axis=1).reshape((MOBA_HEADS * LANES,) + t.shape[2:])


def _pad_nsa(t):
    w = NSA_GROUP * HEAD_DIM
    t = t.reshape((NSA_KV_HEADS, w) + t.shape[1:])
    pad = jnp.zeros((NSA_KV_HEADS, NSA_OUT_LANES - w) + t.shape[2:], t.dtype)
    return jnp.concatenate([t, pad], axis=1).reshape((NSA_KV_HEADS * NSA_OUT_LANES,) + t.shape[2:])


def _gain_and_mask(gain, pad):
    return jnp.stack([pad(gain), pad(jnp.ones_like(gain))])


def _route(logit_t, bias_col):
    aff = jax.nn.sigmoid(logit_t)
    biased = aff + bias_col
    row = lambda t, e: t[e:e + 1, :]
    gp = EXPERTS_PER_GROUP
    scores = []
    for g in range(N_EXPERT_GROUPS):
        a_, b_, c_, d_ = (row(biased, g * gp + j) for j in range(gp))
        hi1, lo1, hi2, lo2 = jnp.maximum(a_, b_), jnp.minimum(a_, b_), jnp.maximum(c_, d_), jnp.minimum(c_, d_)
        scores.append(jnp.maximum(hi1, hi2) + jnp.maximum(jnp.minimum(hi1, hi2), jnp.maximum(lo1, lo2)))
    best = jnp.zeros_like(scores[0], dtype=jnp.int32)
    best_s = scores[0]
    for g in range(1, N_EXPERT_GROUPS):
        better = scores[g] > best_s
        best = jnp.where(better, g, best)
        best_s = jnp.where(better, scores[g], best_s)

    def in_group(t, j):
        out = row(t, j)
        for g in range(1, N_EXPERT_GROUPS):
            out = jnp.where(best == g, row(t, g * gp + j), out)
        return out

    vals = [in_group(biased, j) for j in range(gp)]
    affs = [in_group(aff, j) for j in range(gp)]

    def first_argmax(vs):
        top = functools.reduce(jnp.maximum, vs)
        idx = jnp.full(top.shape, gp, jnp.int32)
        for j in reversed(range(gp)):
            idx = jnp.where(vs[j] == top, j, idx)
        return idx

    i1 = first_argmax(vals)
    i2 = first_argmax([jnp.where(i1 == j, -jnp.inf, vals[j]) for j in range(gp)])
    pick = lambda idx: functools.reduce(lambda acc, j: jnp.where(idx == j, affs[j], acc), range(gp), jnp.zeros_like(affs[0]))
    w1, w2 = pick(i1), pick(i2)
    tot = w1 + w2
    e_iota = lax.broadcasted_iota(jnp.int32, logit_t.shape, 0)
    e1 = best * gp + i1
    e2 = best * gp + i2
    return jnp.where(e_iota == e1, w1 / tot, 0.0) + jnp.where(e_iota == e2, w2 / tot, 0.0)


def _moe_kernel(x_ref, gain_ref, sc_ref, sh_ref, g2_ref, wrt_ref, rb_ref, wg_ref, wu_ref, wd_ref, fin_ref,
                o_ref, hb_scr, cb_scr, acc_scr, *, final):
    e = pl.program_id(1)
    tm = x_ref.shape[1]

    @pl.when(e == 0)
    def _():
        x = x_ref[0]
        ms = jnp.mean(x * x, axis=-1, keepdims=True)
        h = x * lax.rsqrt(ms + NORM_EPS) * gain_ref[...]
        h = h * (1.0 + sc_ref[0]) + sh_ref[0]
        hb_scr[...] = h.astype(BF16)
        comb = _route(_mm3(wrt_ref[...], h, _dot_nt), rb_ref[...])
        cb_scr[...] = jnp.concatenate([comb, jnp.zeros((LANES - N_EXPERTS, tm), F32)], axis=0).T
        acc_scr[...] = jnp.zeros(acc_scr.shape, F32)

    hb = hb_scr[...]
    comb_t = cb_scr[...]
    hes = []
    for j in range(MOE_EXPERTS_PER_STEP):
        hg = _dot(hb, wg_ref[j])
        hu = _dot(hb, wu_ref[j])
        cbe = jnp.broadcast_to(comb_t[:, j:j + 1], (tm, LANES))
        he = hg * jax.nn.sigmoid(hg) * hu * jnp.concatenate([cbe] * (EXPERT_FF // LANES), axis=1)
        hes.append(he.astype(BF16))
    cb_scr[...] = pltpu.roll(comb_t, LANES - MOE_EXPERTS_PER_STEP, 1)
    acc_scr[...] += _dot(jnp.concatenate(hes, axis=1), wd_ref[...])

    @pl.when(e == N_EXPERTS // MOE_EXPERTS_PER_STEP - 1)
    def _():
        out = x_ref[0] + g2_ref[0] * acc_scr[...]
        if final:
            ms = jnp.mean(out * out, axis=-1, keepdims=True)
            out = out * lax.rsqrt(ms + NORM_EPS) * fin_ref[...]
        o_ref[0] = out


def _moe(x, gain, sc, sh, g2, wrt, rb, wg, wu, wd, fin, final):
    B, S, D = x.shape
    tm = 1024
    tpb = S // tm
    eps = MOE_EXPERTS_PER_STEP
    wd = wd.reshape(N_EXPERTS * EXPERT_FF, D)
    tok = pl.BlockSpec((1, tm, D), lambda i, e: (i // tpb, i % tpb, 0))
    per_b = pl.BlockSpec((1, 1, D), lambda i, e: (i // tpb, 0, 0))
    full = lambda a: pl.BlockSpec(a.shape, lambda i, e: (0,) * a.ndim)
    return pl.pallas_call(
        functools.partial(_moe_kernel, final=final),
        grid=(B * tpb, N_EXPERTS // eps),
        in_specs=[tok, full(gain), per_b, per_b, per_b, full(wrt), full(rb),
                  pl.BlockSpec((eps, D, EXPERT_FF), lambda i, e: (e, 0, 0)),
                  pl.BlockSpec((eps, D, EXPERT_FF), lambda i, e: (e, 0, 0)),
                  pl.BlockSpec((eps * EXPERT_FF, D), lambda i, e: (e, 0)),
                  full(fin)],
        out_specs=tok,
        out_shape=jax.ShapeDtypeStruct((B, S, D), F32),
        scratch_shapes=[pltpu.VMEM((tm, D), BF16), pltpu.VMEM((tm, LANES), F32), pltpu.VMEM((tm, D), F32)],
        compiler_params=_cparams(("arbitrary", "arbitrary")),
        name="moe",
    )(x, gain, sc, sh, g2, wrt, rb, wg, wu, wd, fin)


def kernel(x, c, positions, w_mod, b_mod, norm_mix, norm_ffn, w_in, nsa_phi_w1, nsa_phi_w2, nsa_phi_pos, rwkv_mu, rwkv_w_up, rwkv_w0, rwkv_a_up, rwkv_a0, rwkv_g_up, rwkv_k_k, rwkv_k_a, rwkv_r_k, rwkv_ln_w, rwkv_ln_b, norm_nsa_out, norm_moba_out, w_out, w_router, router_bias, moe_w_gate, moe_w_up, moe_w_down, norm_final):
    B, S, D = x.shape
    depth = w_in.shape[0]
    assert S % NSA_TK == 0 and S % MOBA_BLOCK == 0 and S % RWKV_TB == 0 and S >= WINDOW + NSA_QB
    assert S // SLC_BLOCK <= LANES and S // MOBA_BLOCK <= MOBA_NBP

    inv = ROPE_THETA ** (-jnp.arange(0, HEAD_DIM, 2, dtype=F32) / HEAD_DIM)
    ang = positions.astype(F32)[..., None] * inv
    cos, sin = jnp.cos(ang), jnp.sin(ang)
    one, zero = jnp.ones((B, S, HEAD_DIM), F32), jnp.zeros((B, S, HEAD_DIM), F32)
    cos_t = jnp.concatenate([cos, cos, one], axis=-1)
    sin_s = jnp.concatenate([-sin, sin, zero], axis=-1)
    nc = S // CMP_STRIDE
    n_cmp = (S - CMP_BLOCK) // CMP_STRIDE + 1
    pad_c = lambda t, fill: jnp.concatenate([t[:, CMP_BLOCK - 1::CMP_STRIDE][:, :n_cmp],
                                             jnp.full((B, nc - n_cmp, LANES), fill, F32)], axis=1)
    cos_c, sin_c = pad_c(cos_t, 1.0), pad_c(sin_s, 0.0)

    nsa_consts = _nsa_constants(S)
    bd, tri = _rwkv_constants()
    mod = _modulation(c, w_mod, b_mod)
    wrt = w_router.T
    rb = router_bias.reshape(N_EXPERTS, 1)
    fin = norm_final.reshape(1, D)

    for l in range(depth):
        sh1, sc1, g1, sh2, sc2, g2 = (mod[l, :B, i * D:(i + 1) * D].reshape(B, 1, D) for i in range(6))
        p, r, km, vs_t, vw_t, mv_t = _inproj(x, norm_mix[l].reshape(1, D), sc1, sh1, cos_t, sin_s, _prep_w_in(w_in[l]))

        w1, w2, pos4 = _prep_compress(nsa_phi_w1[l], nsa_phi_w2[l], nsa_phi_pos[l])
        chunks = lambda g: p[:, :, g * LANES:(g + 1) * LANES].reshape(B, nc, CMP_STRIDE * LANES)
        kvcmp, kvcmp_t = _compress(chunks(G_KC), chunks(G_VC), pos4, w1, w2, cos_c, sin_c)
        mixn = _nsa(p, r, kvcmp, kvcmp_t, vs_t, vw_t, *nsa_consts)

        kmean = km[:, :, 0, :]
        kmean = jnp.concatenate([kmean, jnp.zeros((B, MOBA_NBP - kmean.shape[1], kmean.shape[2]), F32)], axis=1)
        mixm = _moba(p, mv_t, kmean)

        zl = jnp.zeros((DECAY_LORA, RWKV_WIDTH), F32)
        wup = jnp.concatenate([rwkv_w_up[l], zl], axis=0)
        aup = jnp.concatenate([zl, rwkv_a_up[l]], axis=0)
        vecs = jnp.stack([rwkv_w0[l], rwkv_a0[l], rwkv_k_k[l], rwkv_k_a[l], rwkv_r_k[l].reshape(-1),
                          rwkv_ln_w[l], rwkv_ln_b[l], jnp.zeros((RWKV_WIDTH,), F32)])
        orw = _rwkv(r, rwkv_mu[l].reshape(1, -1), wup, aup, rwkv_g_up[l], vecs, bd, tri)

        wo = w_out[l]
        x = _outproj(mixn, mixm, orw, x, g1,
                     _gain_and_mask(norm_nsa_out[l], _pad_nsa),
                     _gain_and_mask(norm_moba_out[l], _pad_moba),
                     _pad_nsa(wo[:NSA_WIDTH]).astype(BF16),
                     _pad_moba(wo[NSA_WIDTH:NSA_WIDTH + MOBA_WIDTH]).astype(BF16),
                     wo[NSA_WIDTH + MOBA_WIDTH:].astype(BF16))
        x = _moe(x, norm_ffn[l].reshape(1, D), sc2, sh2, g2, wrt, rb,
                 moe_w_gate[l].astype(BF16), moe_w_up[l].astype(BF16), moe_w_down[l].astype(BF16),
                 fin, final=(l == depth - 1))
    return x
```
